```python
import math
import jax, jax.numpy as jnp
from jax import lax
import numpy as np

D_MODEL = 1024
BATCH = 16
SEQ = 2048
DEPTH = 1

HEAD_DIM = 64
RET_HEADS = 8
ATTN_HEADS = 8
ATTN_KV_HEADS = 2
ATTN_GROUP = ATTN_HEADS // ATTN_KV_HEADS
RET_WIDTH = RET_HEADS * HEAD_DIM
ATTN_WIDTH = ATTN_HEADS * HEAD_DIM
KV_WIDTH = ATTN_KV_HEADS * HEAD_DIM
MIX_WIDTH = RET_WIDTH + ATTN_WIDTH
IN_PROJ_WIDTH = 4 * RET_WIDTH + ATTN_WIDTH + 2 * KV_WIDTH
CHUNK = 128
WINDOW = 128
BLOCK = 128
PLE_DIM = 256
FFN_HIDDEN = ((8 * D_MODEL + 767) // 768) * 256
ALPHA = (2.0 * DEPTH) ** 0.25
BETA = (8.0 * DEPTH) ** -0.25
LN_EPS = 1e-5
GN_EPS = 1e-5
NEG_INF = -1e30

kernel_name = "hybrid_retention_swa_deepnorm_encoder"


def _layer_norm(x, gain, bias):
    xf = x.astype(jnp.float32)
    mu = jnp.mean(xf, axis=-1, keepdims=True)
    var = jnp.mean(jnp.square(xf - mu), axis=-1, keepdims=True)
    return (xf - mu) * lax.rsqrt(var + LN_EPS) * gain.astype(jnp.float32) + bias.astype(jnp.float32)


def _retention_one_direction(q, k, v, log_gamma):
    b, h, s, d = q.shape
    n = s // CHUNK
    q = q.reshape(b, h, n, CHUNK, d)
    k = k.reshape(b, h, n, CHUNK, d)
    v = v.reshape(b, h, n, CHUNK, d)
    idx = jnp.arange(CHUNK, dtype=jnp.float32)
    diff = idx[:, None] - idx[None, :]
    lg = log_gamma[:, None, None]
    decay_in = jnp.where(diff >= 0, jnp.exp(lg * jnp.maximum(diff, 0.0)), 0.0)
    k_decay = jnp.exp(log_gamma[:, None] * (CHUNK - 1.0 - idx)[None, :])
    q_decay = jnp.exp(log_gamma[:, None] * (idx + 1.0)[None, :])
    chunk_decay = jnp.exp(log_gamma * CHUNK)
    scores = jnp.einsum('bhncd,bhnsd->bhncs', q, k) * decay_in[None, :, None]
    y_inner = jnp.einsum('bhncs,bhnse->bhnce', scores, v)
    kv = jnp.einsum('bhnsd,bhnse->bhnde', k * k_decay[None, :, None, :, None], v)

    def step(state, kv_n):
        return state * chunk_decay[None, :, None, None] + kv_n, state

    _, r_prev = lax.scan(step, jnp.zeros((b, h, d, d), q.dtype), jnp.moveaxis(kv, 2, 0))
    r_prev = jnp.moveaxis(r_prev, 0, 2)
    y_cross = jnp.einsum('bhncd,bhnde->bhnce', q * q_decay[None, :, None, :, None], r_prev)
    return (y_inner + y_cross).reshape(b, h, s, d)


def _bidirectional_retention(q, k, v, log_gamma_fwd, log_gamma_bwd, gn_gain, gate):
    b, s, _ = q.shape
    to_heads = lambda t: jnp.transpose(t.astype(jnp.float32).reshape(b, s, RET_HEADS, HEAD_DIM), (0, 2, 1, 3))
    qh, kh, vh = to_heads(q), to_heads(k) * (HEAD_DIM ** -0.5), to_heads(v)
    y_f = _retention_one_direction(qh, kh, vh, log_gamma_fwd)
    flip = lambda t: jnp.flip(t, axis=2)
    y_b = flip(_retention_one_direction(flip(qh), flip(kh), flip(vh), log_gamma_bwd))
    y = y_f + y_b
    mu = jnp.mean(y, axis=-1, keepdims=True)
    var = jnp.mean(jnp.square(y - mu), axis=-1, keepdims=True)
    y = (y - mu) * lax.rsqrt(var + GN_EPS)
    y = jnp.transpose(y, (0, 2, 1, 3)).reshape(b, s, RET_WIDTH) * gn_gain.astype(jnp.float32)
    return y * jax.nn.silu(gate.astype(jnp.float32))


def _alibi_slopes(n_heads):
    return 2.0 ** (-8.0 * (jnp.arange(n_heads, dtype=jnp.float32) + 1.0) / n_heads)


def _windowed_gqa(q, k, v, sink):
    b, s, _ = q.shape
    n = s // BLOCK
    qb = q.astype(jnp.float32).reshape(b, n, BLOCK, ATTN_KV_HEADS, ATTN_GROUP, HEAD_DIM)

    def neighbour_blocks(t):
        tp = jnp.pad(t.astype(jnp.float32).reshape(b, s, ATTN_KV_HEADS, HEAD_DIM),
                     ((0, 0), (BLOCK, BLOCK), (0, 0), (0, 0)))
        tp = tp.reshape(b, n + 2, BLOCK, ATTN_KV_HEADS, HEAD_DIM)
        return jnp.concatenate([tp[:, :-2], tp[:, 1:-1], tp[:, 2:]], axis=2)

    kb, vb = neighbour_blocks(k), neighbour_blocks(v)
    scores = jnp.einsum('bnqhgd,bnshd->bnhgqs', qb, kb) * (HEAD_DIM ** -0.5)
    qi = jnp.arange(BLOCK)
    kj = jnp.arange(3 * BLOCK)
    dist = jnp.abs(kj[None, :] - BLOCK - qi[:, None])
    key_pos = jnp.arange(n)[:, None] * BLOCK - BLOCK + kj[None, :]
    valid = (dist <= WINDOW)[None] & ((key_pos >= 0) & (key_pos < s))[:, None, :]
    slopes = _alibi_slopes(ATTN_HEADS).reshape(ATTN_KV_HEADS, ATTN_GROUP)
    alibi = -slopes[:, :, None, None] * dist.astype(jnp.float32)[None, None]
    scores = jnp.where(valid[None, :, None, None], scores + alibi[None, None], NEG_INF)
    sink_l = sink.astype(jnp.float32).reshape(ATTN_KV_HEADS, ATTN_GROUP)[None, None, :, :, None, None]
    m = jnp.maximum(jnp.max(scores, axis=-1, keepdims=True), sink_l)
    e = jnp.exp(scores - m)
    denom = jnp.sum(e, axis=-1, keepdims=True) + jnp.exp(sink_l - m)
    probs = e / denom
    out = jnp.einsum('bnhgqs,bnshd->bnqhgd', probs, vb)
    return out.reshape(b, s, ATTN_WIDTH)


def setup_inputs(seed: int = 0) -> dict:
    key = jax.random.key(seed)
    ks = jax.random.split(key, 18)
    nrm = lambda k, shape, scale: jax.random.normal(k, shape, jnp.float32) * scale
    base_log2 = -5.0 - jnp.arange(RET_HEADS, dtype=jnp.float32)
    return {
        "x": nrm(ks[0], (BATCH, SEQ, D_MODEL), 1.0),
        "p": nrm(ks[1], (DEPTH, BATCH, SEQ, PLE_DIM), 1.0),
        "w_in": nrm(ks[2], (DEPTH, D_MODEL, IN_PROJ_WIDTH), D_MODEL ** -0.5),
        "ret_decay_fwd": base_log2[None] + nrm(ks[3], (DEPTH, RET_HEADS), 0.1),
        "ret_decay_bwd": base_log2[None] + nrm(ks[4], (DEPTH, RET_HEADS), 0.1),
        "ret_gn_gain": 1.0 + nrm(ks[5], (DEPTH, RET_WIDTH), 0.02),
        "attn_sink": nrm(ks[6], (DEPTH, ATTN_HEADS), 0.5),
        "w_out": nrm(ks[7], (DEPTH, MIX_WIDTH, D_MODEL), BETA * MIX_WIDTH ** -0.5),
        "ln1_gain": 1.0 + nrm(ks[8], (DEPTH, D_MODEL), 0.02),
        "ln1_bias": nrm(ks[9], (DEPTH, D_MODEL), 0.02),
        "w_ffn_gate": nrm(ks[10], (DEPTH, D_MODEL, FFN_HIDDEN), D_MODEL ** -0.5),
        "w_ffn_up": nrm(ks[11], (DEPTH, D_MODEL, FFN_HIDDEN), D_MODEL ** -0.5),
        "w_ffn_down": nrm(ks[12], (DEPTH, FFN_HIDDEN, D_MODEL), BETA * FFN_HIDDEN ** -0.5),
        "w_ple_proj": nrm(ks[13], (DEPTH, PLE_DIM, D_MODEL), BETA * PLE_DIM ** -0.5),
        "w_ple_gate": nrm(ks[14], (DEPTH, D_MODEL, D_MODEL), D_MODEL ** -0.5),
        "ln2_gain": 1.0 + nrm(ks[15], (DEPTH, D_MODEL), 0.02),
        "ln2_bias": nrm(ks[16], (DEPTH, D_MODEL), 0.02),
    }


def reference(x, p, w_in, ret_decay_fwd, ret_decay_bwd, ret_gn_gain, attn_sink, w_out,
              ln1_gain, ln1_bias, w_ffn_gate, w_ffn_up, w_ffn_down, w_ple_proj, w_ple_gate,
              ln2_gain, ln2_bias):
    split_points = [RET_WIDTH, 2 * RET_WIDTH, 3 * RET_WIDTH, 4 * RET_WIDTH,
                    4 * RET_WIDTH + ATTN_WIDTH, 4 * RET_WIDTH + ATTN_WIDTH + KV_WIDTH]
    h = x.astype(jnp.float32)
    for i in range(DEPTH):
        u = jnp.einsum('bsd,de->bse', h, w_in[i].astype(jnp.float32))
        rq, rk, rv, rg, aq, ak, av = jnp.split(u, split_points, axis=-1)
        lg_f = jnp.log1p(-jnp.exp2(ret_decay_fwd[i].astype(jnp.float32)))
        lg_b = jnp.log1p(-jnp.exp2(ret_decay_bwd[i].astype(jnp.float32)))
        y_ret = _bidirectional_retention(rq, rk, rv, lg_f, lg_b, ret_gn_gain[i], rg)
        y_att = _windowed_gqa(aq, ak, av, attn_sink[i])
        mix = jnp.einsum('bse,ed->bsd', jnp.concatenate([y_ret, y_att], axis=-1),
                         w_out[i].astype(jnp.float32))
        h = _layer_norm(ALPHA * h + mix, ln1_gain[i], ln1_bias[i])
        g = jnp.einsum('bsd,df->bsf', h, w_ffn_gate[i].astype(jnp.float32))
        up = jnp.einsum('bsd,df->bsf', h, w_ffn_up[i].astype(jnp.float32))
        ffn = jnp.einsum('bsf,fd->bsd', jax.nn.silu(g) * up, w_ffn_down[i].astype(jnp.float32))
        ple = jnp.einsum('bsr,rd->bsd', p[i].astype(jnp.float32), w_ple_proj[i].astype(jnp.float32))
        ple_gate = jax.nn.sigmoid(jnp.einsum('bsd,de->bse', h, w_ple_gate[i].astype(jnp.float32)))
        h = _layer_norm(ALPHA * h + ffn + ple_gate * ple, ln2_gain[i], ln2_bias[i])
    return h.astype(x.dtype)
```

```python
import functools

import jax
import jax.numpy as jnp
from jax import lax
from jax.experimental import pallas as pl
from jax.experimental.pallas import tpu as pltpu

D_MODEL = 1024
HEAD_DIM = 64
RET_HEADS = 8
ATTN_HEADS = 8
ATTN_KV_HEADS = 2
ATTN_GROUP = ATTN_HEADS // ATTN_KV_HEADS
RET_WIDTH = RET_HEADS * HEAD_DIM
ATTN_WIDTH = ATTN_HEADS * HEAD_DIM
KV_WIDTH = ATTN_KV_HEADS * HEAD_DIM
IN_PROJ_WIDTH = 4 * RET_WIDTH + ATTN_WIDTH + 2 * KV_WIDTH
CHUNK = 128
WINDOW = 128
BLOCK = 128
PLE_DIM = 256
FFN_HIDDEN = ((8 * D_MODEL + 767) // 768) * 256
LN_EPS = 1e-5
GN_EPS = 1e-5
NEG_INF = -1e30
QK_SCALE = HEAD_DIM ** -0.5

LANES = 128
HEADS_PER_LANE_BLOCK = LANES // HEAD_DIM
RET_PAIRS = RET_HEADS // HEADS_PER_LANE_BLOCK
VMEM_LIMIT_BYTES = 56 * 1024 * 1024

F32 = jnp.float32
BF16 = jnp.bfloat16


def _dot(a, b):
    return jnp.dot(a, b, preferred_element_type=F32)


def _dot_nt(a, b):
    return lax.dot_general(a, b, (((1,), (1,)), ((), ())), preferred_element_type=F32)


def _dot_tn(a, b):
    return lax.dot_general(a, b, (((0,), (0,)), ((), ())), preferred_element_type=F32)


_IN_PROJ_COLS = 256
_RET_BLOCKS = 4 * RET_PAIRS


def _in_proj_kernel(x_ref, w_ref, ret_ref, aq_ref, akv_ref):
    xb = x_ref[0].astype(BF16)
    for c in range(IN_PROJ_WIDTH // _IN_PROJ_COLS):
        u = _dot(xb, w_ref[:, c * _IN_PROJ_COLS:(c + 1) * _IN_PROJ_COLS]).astype(BF16)
        if 2 * c < _RET_BLOCKS:
            ret_ref[0, 2 * c] = u[:, :LANES]
            ret_ref[0, 2 * c + 1] = u[:, LANES:]
        elif c < _RET_BLOCKS // 2 + ATTN_KV_HEADS:
            aq_ref[0, c - _RET_BLOCKS // 2] = u
        else:
            akv_ref[0, 0] = u[:, :LANES]
            akv_ref[0, 1] = u[:, LANES:]


def _in_proj(x, w, tm):
    b, s, d = x.shape
    return pl.pallas_call(
        _in_proj_kernel,
        grid=(b, s // tm),
        in_specs=[
            pl.BlockSpec((1, tm, d), lambda i, j: (i, j, 0)),
            pl.BlockSpec((d, IN_PROJ_WIDTH), lambda i, j: (0, 0)),
        ],
        out_specs=[
            pl.BlockSpec((1, _RET_BLOCKS, tm, LANES), lambda i, j: (i, 0, j, 0)),
            pl.BlockSpec((1, ATTN_KV_HEADS, tm, ATTN_GROUP * HEAD_DIM), lambda i, j: (i, 0, j, 0)),
            pl.BlockSpec((1, ATTN_KV_HEADS, tm, LANES), lambda i, j: (i, 0, j, 0)),
        ],
        out_shape=[
            jax.ShapeDtypeStruct((b, _RET_BLOCKS, s, LANES), BF16),
            jax.ShapeDtypeStruct((b, ATTN_KV_HEADS, s, ATTN_GROUP * HEAD_DIM), BF16),
            jax.ShapeDtypeStruct((b, ATTN_KV_HEADS, s, LANES), BF16),
        ],
        compiler_params=pltpu.CompilerParams(
            dimension_semantics=("arbitrary", "arbitrary"), vmem_limit_bytes=VMEM_LIMIT_BYTES),
        name="in_proj",
    )(x, w)


def _split_hi_lo(z):
    hi = z.astype(BF16)
    lo = (z - hi.astype(F32)).astype(BF16)
    return hi, lo


def _ret_kernel(q_ref, k_ref, v_ref, g_ref, decf_ref, decb_ref, gain_ref, o_ref, kv_ref, st_ref,
                *, seq):
    n_chunks = seq // CHUNK
    lane = lax.broadcasted_iota(jnp.int32, (1, LANES), 1)
    head0 = lane < HEAD_DIM

    def log_gamma(dec):
        return jnp.log1p(-jnp.exp2(dec))

    lgf0, lgf1 = log_gamma(decf_ref[0, 0:1, :]), log_gamma(decf_ref[0, 1:2, :])
    lgb0, lgb1 = log_gamma(decb_ref[0, 0:1, :]), log_gamma(decb_ref[0, 1:2, :])
    lgf = jnp.where(head0, lgf0, lgf1)
    lgb = jnp.where(head0, lgb0, lgb1)

    idx = lax.broadcasted_iota(jnp.int32, (CHUNK, LANES), 0).astype(F32)
    qdf = jnp.exp(lgf * (idx + 1.0))
    kdf = jnp.exp(lgf * (CHUNK - 1.0 - idx))
    qdb = jnp.exp(lgb * (CHUNK - idx))
    kdb = jnp.exp(lgb * idx)
    cdf = jnp.exp(lgf * CHUNK)
    cdb = jnp.exp(lgb * CHUNK)

    ri = lax.broadcasted_iota(jnp.int32, (CHUNK, CHUNK), 0)
    ci = lax.broadcasted_iota(jnp.int32, (CHUNK, CHUNK), 1)
    diff = (ri - ci).astype(F32)

    def intra_mask(lf, lb):
        fwd = jnp.exp(lf * jnp.maximum(diff, 0.0))
        bwd = jnp.exp(lb * jnp.maximum(-diff, 0.0))
        return jnp.where(diff > 0, fwd, jnp.where(diff < 0, bwd, fwd + bwd))

    mask2 = jnp.concatenate([intra_mask(lgf0, lgb0), intra_mask(lgf1, lgb1)], axis=0)
    same_head = (ri < HEAD_DIM) == (ci < HEAD_DIM)
    gn_avg = jnp.where(same_head, 1.0 / HEAD_DIM, 0.0).astype(BF16)

    for n in range(n_chunks):
        sl = pl.ds(n * CHUNK, CHUNK)
        kc = k_ref[0, 0, sl, :].astype(F32) * QK_SCALE
        kd = jnp.concatenate([(kc * kdf).astype(BF16), (kc * kdb).astype(BF16)], axis=1)
        kv = _dot_tn(kd, v_ref[0, 0, sl, :])
        kv_ref[n, 0:LANES, :] = jnp.where(same_head, kv[0:LANES], 0.0)
        kv_ref[n, LANES:2 * LANES, :] = jnp.where(same_head, kv[LANES:2 * LANES], 0.0)

    state = jnp.zeros((LANES, LANES), F32)
    for n in range(n_chunks):
        st_ref[n, 0:LANES, :] = state.astype(BF16)
        state = state * cdf + kv_ref[n, 0:LANES, :]
    state = jnp.zeros((LANES, LANES), F32)
    for n in reversed(range(n_chunks)):
        st_ref[n, LANES:2 * LANES, :] = state.astype(BF16)
        state = state * cdb + kv_ref[n, LANES:2 * LANES, :]

    gain = gain_ref[0]
    zero = jnp.zeros((CHUNK, LANES), BF16)
    for n in range(n_chunks):
        sl = pl.ds(n * CHUNK, CHUNK)
        qb = q_ref[0, 0, sl, :]
        kb = (k_ref[0, 0, sl, :].astype(F32) * QK_SCALE).astype(BF16)
        vb = v_ref[0, 0, sl, :]
        qs = jnp.concatenate([jnp.where(head0, qb, zero), jnp.where(head0, zero, qb)], axis=0)
        raw = _dot_nt(qs, kb)
        p = (raw * mask2).astype(BF16)
        pcat = jnp.concatenate([p[0:CHUNK], p[CHUNK:2 * CHUNK]], axis=1)
        vs = jnp.concatenate([jnp.where(head0, vb, zero), jnp.where(head0, zero, vb)], axis=0)
        y = _dot(pcat, vs)
        qc = qb.astype(F32)
        qd = jnp.concatenate([(qc * qdf).astype(BF16), (qc * qdb).astype(BF16)], axis=1)
        y = y + _dot(qd, st_ref[n])

        hi, lo = _split_hi_lo(y)
        mu = _dot(hi, gn_avg) + _dot(lo, gn_avg)
        dlt = y - mu
        hi, lo = _split_hi_lo(dlt * dlt)
        var = _dot(hi, gn_avg) + _dot(lo, gn_avg)
        yn = dlt * lax.rsqrt(var + GN_EPS) * gain
        gate = g_ref[0, 0, sl, :].astype(F32)
        o_ref[0, sl, :] = (yn * (gate * jax.nn.sigmoid(gate))).astype(BF16)


def _retention(ret, decf, decb, gain):
    b, _, s, _ = ret.shape
    n_chunks = s // CHUNK

    def part(kind):
        return pl.BlockSpec((1, 1, s, LANES), lambda i, j, kind=kind: (i, kind * RET_PAIRS + j, 0, 0))

    return pl.pallas_call(
        functools.partial(_ret_kernel, seq=s),
        grid=(b, RET_PAIRS),
        in_specs=[
            part(0), part(1), part(2), part(3),
            pl.BlockSpec((1, HEADS_PER_LANE_BLOCK, LANES), lambda i, j: (j, 0, 0)),
            pl.BlockSpec((1, HEADS_PER_LANE_BLOCK, LANES), lambda i, j: (j, 0, 0)),
            pl.BlockSpec((1, 1, LANES), lambda i, j: (j, 0, 0)),
        ],
        out_specs=pl.BlockSpec((1, s, LANES), lambda i, j: (i, 0, j)),
        out_shape=jax.ShapeDtypeStruct((b, s, RET_WIDTH), BF16),
        scratch_shapes=[
            pltpu.VMEM((n_chunks, 2 * LANES, LANES), F32),
            pltpu.VMEM((n_chunks, 2 * LANES, LANES), BF16),
        ],
        compiler_params=pltpu.CompilerParams(
            dimension_semantics=("arbitrary", "arbitrary"), vmem_limit_bytes=VMEM_LIMIT_BYTES),
        name="retention",
    )(ret, ret, ret, ret, decf, decb, gain)


_STACK = 2 * BLOCK
_KEYS = 3 * BLOCK


def _attn_kernel(q_ref, kv_ref, slope_ref, sink_ref, o_ref, kvs_ref, vk_ref, vo_ref, bias_ref,
                 *, seq):
    n_blocks = seq // BLOCK
    lane = lax.broadcasted_iota(jnp.int32, (1, LANES), 1)
    low = lane < HEAD_DIM

    kv = kv_ref[0, 0]
    swapped = pltpu.roll(kv.astype(F32), HEAD_DIM, 1).astype(BF16)
    one = jnp.ones_like(kv)
    kvs_ref[...] = swapped
    vk_ref[...] = jnp.where(low, swapped, one)
    vo_ref[...] = jnp.where(low, one, kv)

    r = lax.broadcasted_iota(jnp.int32, (_STACK, _KEYS), 0)
    qi = jnp.where(r < BLOCK, r, r - BLOCK)
    kj = lax.broadcasted_iota(jnp.int32, (_STACK, _KEYS), 1)
    dist = jnp.abs(kj - BLOCK - qi)
    in_window = dist <= WINDOW
    distf = dist.astype(F32)
    for par in range(2):
        slope = slope_ref[0, par]
        slope3 = jnp.concatenate([slope, slope, slope], axis=1)
        bias_ref[par] = jnp.where(in_window, -slope3 * distf, NEG_INF)

    zero = jnp.zeros((BLOCK, LANES), BF16)
    for n in range(n_blocks):
        lo_blk, hi_blk = max(n - 1, 0), min(n + 2, n_blocks)
        keys = pl.ds(lo_blk * BLOCK, (hi_blk - lo_blk) * BLOCK)
        c0 = (lo_blk - (n - 1)) * BLOCK
        width = (hi_blk - lo_blk) * BLOCK
        qblk = q_ref[0, 0, pl.ds(n * BLOCK, BLOCK), :]
        qa, qb = qblk[:, :LANES], qblk[:, LANES:]
        outs = []
        for par in range(2):
            if par == 0:
                qs = jnp.concatenate([jnp.where(low, qa, zero), jnp.where(low, qb, zero)], axis=0)
                kmat, vmat = kv_ref[0, 0, keys, :], vk_ref[keys, :]
            else:
                qs = jnp.concatenate([jnp.where(low, zero, qa), jnp.where(low, zero, qb)], axis=0)
                kmat, vmat = kvs_ref[keys, :], vo_ref[keys, :]
            s = _dot_nt(qs, kmat) * QK_SCALE + bias_ref[par, :, c0:c0 + width]
            sink = sink_ref[0, par]
            m = jnp.maximum(jnp.max(s, axis=1, keepdims=True), sink[:, 0:1])
            e = jnp.exp(s - m)
            pv = _dot(e.astype(BF16), vmat)
            denom = pltpu.roll(pv, HEAD_DIM, 1) + jnp.exp(sink - m)
            outs.append(pv / denom)
        y = jnp.where(low, outs[0], outs[1])
        o_ref[0, pl.ds(n * BLOCK, BLOCK), :] = jnp.concatenate(
            [y[0:BLOCK], y[BLOCK:_STACK]], axis=1).astype(BF16)


def _attention(aq, akv, slope_rows, sink_rows):
    b, _, s, qw = aq.shape
    return pl.pallas_call(
        functools.partial(_attn_kernel, seq=s),
        grid=(b, ATTN_KV_HEADS),
        in_specs=[
            pl.BlockSpec((1, 1, s, qw), lambda i, j: (i, j, 0, 0)),
            pl.BlockSpec((1, 1, s, LANES), lambda i, j: (i, j, 0, 0)),
            pl.BlockSpec((1, 2, _STACK, LANES), lambda i, j: (j, 0, 0, 0)),
            pl.BlockSpec((1, 2, _STACK, LANES), lambda i, j: (j, 0, 0, 0)),
        ],
        out_specs=pl.BlockSpec((1, s, qw), lambda i, j: (i, 0, j)),
        out_shape=jax.ShapeDtypeStruct((b, s, ATTN_WIDTH), BF16),
        scratch_shapes=[
            pltpu.VMEM((s, LANES), BF16),
            pltpu.VMEM((s, LANES), BF16),
            pltpu.VMEM((s, LANES), BF16),
            pltpu.VMEM((2, _STACK, _KEYS), F32),
        ],
        compiler_params=pltpu.CompilerParams(
            dimension_semantics=("arbitrary", "arbitrary"), vmem_limit_bytes=VMEM_LIMIT_BYTES),
        name="attention",
    )(aq, akv, slope_rows, sink_rows)


def _layer_norm(z, gain, bias):
    mu = jnp.mean(z, axis=-1, keepdims=True)
    dlt = z - mu
    var = jnp.mean(dlt * dlt, axis=-1, keepdims=True)
    return dlt * lax.rsqrt(var + LN_EPS) * gain + bias


def _ffn_kernel(x_ref, yr_ref, ya_ref, p_ref, wo_ref, g1_ref, b1_ref, wg_ref, wu_ref, wd_ref,
                wpe_ref, wpg_ref, g2_ref, b2_ref, o_ref, *, alpha):
    mix = _dot(yr_ref[...], wo_ref[0:RET_WIDTH, :]) + _dot(ya_ref[...], wo_ref[RET_WIDTH:, :])
    h = _layer_norm(alpha * x_ref[...] + mix, g1_ref[...], b1_ref[...])
    hb = h.astype(BF16)
    g = _dot(hb, wg_ref[...])
    up = _dot(hb, wu_ref[...])
    act = (g * jax.nn.sigmoid(g) * up).astype(BF16)
    ffn = _dot(act, wd_ref[...])
    ple = _dot(p_ref[...].astype(BF16), wpe_ref[...])
    ple_gate = jax.nn.sigmoid(_dot(hb, wpg_ref[...]))
    o_ref[...] = _layer_norm(alpha * h + ffn + ple_gate * ple, g2_ref[...], b2_ref[...])


def _ffn(x, yr, ya, p, wo, g1, b1, wg, wu, wd, wpe, wpg, g2, b2, *, alpha, tm):
    t, d = x.shape

    def rows(width):
        return pl.BlockSpec((tm, width), lambda i: (i, 0))

    def resident(arr):
        return pl.BlockSpec(arr.shape, lambda i: (0, 0), pipeline_mode=pl.Buffered(1))

    return pl.pallas_call(
        functools.partial(_ffn_kernel, alpha=alpha),
        grid=(t // tm,),
        in_specs=[rows(d), rows(RET_WIDTH), rows(ATTN_WIDTH), rows(PLE_DIM),
                  resident(wo), resident(g1), resident(b1), resident(wg), resident(wu),
                  resident(wd), resident(wpe), resident(wpg), resident(g2), resident(b2)],
        out_specs=rows(d),
        out_shape=jax.ShapeDtypeStruct((t, d), F32),
        compiler_params=pltpu.CompilerParams(
            dimension_semantics=("arbitrary",), vmem_limit_bytes=VMEM_LIMIT_BYTES),
        name="ffn",
    )(x, yr, ya, p, wo, g1, b1, wg, wu, wd, wpe, wpg, g2, b2)


def _arrange_w_in(w):
    kv0 = 4 * RET_WIDTH + ATTN_WIDTH
    ak, av = w[:, kv0:kv0 + KV_WIDTH], w[:, kv0 + KV_WIDTH:]
    kv_cols = []
    for h in range(ATTN_KV_HEADS):
        kv_cols += [ak[:, h * HEAD_DIM:(h + 1) * HEAD_DIM], av[:, h * HEAD_DIM:(h + 1) * HEAD_DIM]]
    return jnp.concatenate([w[:, :kv0]] + kv_cols, axis=1).astype(BF16)


def _stacked_head_rows(per_head):
    v = per_head.astype(F32).reshape(ATTN_KV_HEADS, 2, 2)
    v = jnp.transpose(v, (0, 2, 1))
    v = jnp.repeat(v, BLOCK, axis=2)
    return jnp.broadcast_to(v[..., None], v.shape + (LANES,))


def kernel(x, p, w_in, ret_decay_fwd, ret_decay_bwd, ret_gn_gain, attn_sink, w_out, ln1_gain, ln1_bias,
           w_ffn_gate, w_ffn_up, w_ffn_down, w_ple_proj, w_ple_gate, ln2_gain, ln2_bias):
    depth = w_in.shape[0]
    b, s, d = x.shape
    alpha = (2.0 * depth) ** 0.25
    slopes = 2.0 ** (-8.0 * (jnp.arange(ATTN_HEADS, dtype=F32) + 1.0) / ATTN_HEADS)
    slope_rows = _stacked_head_rows(slopes)

    def lane_rep(v):
        v = v.astype(F32).reshape(RET_PAIRS, HEADS_PER_LANE_BLOCK, 1)
        return jnp.broadcast_to(v, (RET_PAIRS, HEADS_PER_LANE_BLOCK, LANES))

    row = lambda v: v.astype(F32).reshape(1, -1)
    h = x.astype(F32)
    for i in range(depth):
        ret, aq, akv = _in_proj(h, _arrange_w_in(w_in[i]), tm=512)
        y_ret = _retention(ret, lane_rep(ret_decay_fwd[i]), lane_rep(ret_decay_bwd[i]),
                           ret_gn_gain[i].astype(F32).reshape(RET_PAIRS, 1, LANES))
        y_att = _attention(aq, akv, slope_rows, _stacked_head_rows(attn_sink[i]))
        h = _ffn(h.reshape(b * s, d), y_ret.reshape(b * s, RET_WIDTH), y_att.reshape(b * s, ATTN_WIDTH),
                 p[i].reshape(b * s, PLE_DIM),
                 w_out[i].astype(BF16), row(ln1_gain[i]), row(ln1_bias[i]),
                 w_ffn_gate[i].astype(BF16), w_ffn_up[i].astype(BF16), w_ffn_down[i].astype(BF16),
                 w_ple_proj[i].astype(BF16), w_ple_gate[i].astype(BF16),
                 row(ln2_gain[i]), row(ln2_bias[i]), alpha=alpha, tm=512).reshape(b, s, d)
    return h.astype(x.dtype)
```

```python
import functools

import jax
import jax.numpy as jnp
from jax import lax
from jax.experimental import pallas as pl
from jax.experimental.pallas import tpu as pltpu

D_MODEL = 1024
HEAD_DIM = 64
RET_HEADS = 8
ATTN_HEADS = 8
ATTN_KV_HEADS = 2
ATTN_GROUP = ATTN_HEADS // ATTN_KV_HEADS
RET_WIDTH = RET_HEADS * HEAD_DIM
ATTN_WIDTH = ATTN_HEADS * HEAD_DIM
KV_WIDTH = ATTN_KV_HEADS * HEAD_DIM
IN_PROJ_WIDTH = 4 * RET_WIDTH + ATTN_WIDTH + 2 * KV_WIDTH
CHUNK = 128
WINDOW = 128
BLOCK = 128
PLE_DIM = 256
FFN_HIDDEN = ((8 * D_MODEL + 767) // 768) * 256
LN_EPS = 1e-5
GN_EPS = 1e-5
NEG_INF = -1e30
QK_SCALE = HEAD_DIM ** -0.5

LANES = 128
HEADS_PER_LANE_BLOCK = LANES // HEAD_DIM
RET_PAIRS = RET_HEADS // HEADS_PER_LANE_BLOCK
VMEM_LIMIT_BYTES = 56 * 1024 * 1024

F32 = jnp.float32
BF16 = jnp.bfloat16


def _dot(a, b):
    return jnp.dot(a, b, preferred_element_type=F32)


def _dot_nt(a, b):
    return lax.dot_general(a, b, (((1,), (1,)), ((), ())), preferred_element_type=F32)


def _dot_tn(a, b):
    return lax.dot_general(a, b, (((0,), (0,)), ((), ())), preferred_element_type=F32)


_IN_PROJ_COLS = 256
_RET_BLOCKS = 4 * RET_PAIRS


def _in_proj_kernel(x_ref, w_ref, ret_ref, aq_ref, akv_ref):
    xb = x_ref[0].astype(BF16)
    for c in range(IN_PROJ_WIDTH // _IN_PROJ_COLS):
        u = _dot(xb, w_ref[:, c * _IN_PROJ_COLS:(c + 1) * _IN_PROJ_COLS]).astype(BF16)
        if 2 * c < _RET_BLOCKS:
            ret_ref[0, 2 * c] = u[:, :LANES]
            ret_ref[0, 2 * c + 1] = u[:, LANES:]
        elif c < _RET_BLOCKS // 2 + ATTN_KV_HEADS:
            aq_ref[0, c - _RET_BLOCKS // 2] = u
        else:
            akv_ref[0, 0] = u[:, :LANES]
            akv_ref[0, 1] = u[:, LANES:]


def _in_proj(x, w, tm):
    b, s, d = x.shape
    return pl.pallas_call(
        _in_proj_kernel,
        grid=(b, s // tm),
        in_specs=[
            pl.BlockSpec((1, tm, d), lambda i, j: (i, j, 0)),
            pl.BlockSpec((d, IN_PROJ_WIDTH), lambda i, j: (0, 0)),
        ],
        out_specs=[
            pl.BlockSpec((1, _RET_BLOCKS, tm, LANES), lambda i, j: (i, 0, j, 0)),
            pl.BlockSpec((1, ATTN_KV_HEADS, tm, ATTN_GROUP * HEAD_DIM), lambda i, j: (i, 0, j, 0)),
            pl.BlockSpec((1, ATTN_KV_HEADS, tm, LANES), lambda i, j: (i, 0, j, 0)),
        ],
        out_shape=[
            jax.ShapeDtypeStruct((b, _RET_BLOCKS, s, LANES), BF16),
            jax.ShapeDtypeStruct((b, ATTN_KV_HEADS, s, ATTN_GROUP * HEAD_DIM), BF16),
            jax.ShapeDtypeStruct((b, ATTN_KV_HEADS, s, LANES), BF16),
        ],
        compiler_params=pltpu.CompilerParams(
            dimension_semantics=("arbitrary", "arbitrary"), vmem_limit_bytes=VMEM_LIMIT_BYTES),
        name="in_proj",
    )(x, w)


def _split_hi_lo(z):
    hi = z.astype(BF16)
    lo = (z - hi.astype(F32)).astype(BF16)
    return hi, lo


def _ret_kernel(q_ref, k_ref, v_ref, g_ref, decf_ref, decb_ref, gain_ref, o_ref, kv_ref, st_ref,
                p_ref, y_ref, *, seq):
    n_chunks = seq // CHUNK
    lane = lax.broadcasted_iota(jnp.int32, (1, LANES), 1)
    head0 = lane < HEAD_DIM

    def log_gamma(dec):
        return jnp.log1p(-jnp.exp2(dec))

    lgf0, lgf1 = log_gamma(decf_ref[0, 0:1, :]), log_gamma(decf_ref[0, 1:2, :])
    lgb0, lgb1 = log_gamma(decb_ref[0, 0:1, :]), log_gamma(decb_ref[0, 1:2, :])
    lgf = jnp.where(head0, lgf0, lgf1)
    lgb = jnp.where(head0, lgb0, lgb1)

    idx = lax.broadcasted_iota(jnp.int32, (CHUNK, LANES), 0).astype(F32)
    qdf = jnp.exp(lgf * (idx + 1.0))
    kdf = jnp.exp(lgf * (CHUNK - 1.0 - idx))
    qdb = jnp.exp(lgb * (CHUNK - idx))
    kdb = jnp.exp(lgb * idx)
    cdf = jnp.exp(lgf * CHUNK)
    cdb = jnp.exp(lgb * CHUNK)

    ri = lax.broadcasted_iota(jnp.int32, (CHUNK, CHUNK), 0)
    ci = lax.broadcasted_iota(jnp.int32, (CHUNK, CHUNK), 1)
    diff = (ri - ci).astype(F32)

    def intra_mask(lf, lb):
        fwd = jnp.exp(lf * jnp.maximum(diff, 0.0))
        bwd = jnp.exp(lb * jnp.maximum(-diff, 0.0))
        return jnp.where(diff > 0, fwd, jnp.where(diff < 0, bwd, fwd + bwd))

    mask2 = jnp.concatenate([intra_mask(lgf0, lgb0), intra_mask(lgf1, lgb1)], axis=0)
    same_head = (ri < HEAD_DIM) == (ci < HEAD_DIM)
    gn_avg = jnp.where(same_head, 1.0 / HEAD_DIM, 0.0).astype(BF16)

    for n in range(n_chunks):
        sl = pl.ds(n * CHUNK, CHUNK)
        kc = k_ref[0, 0, sl, :].astype(F32) * QK_SCALE
        kd = jnp.concatenate([(kc * kdf).astype(BF16), (kc * kdb).astype(BF16)], axis=1)
        kv = _dot_tn(kd, v_ref[0, 0, sl, :])
        kv_ref[n, 0:LANES, :] = jnp.where(same_head, kv[0:LANES], 0.0)
        kv_ref[n, LANES:2 * LANES, :] = jnp.where(same_head, kv[LANES:2 * LANES], 0.0)

    state = jnp.zeros((LANES, LANES), F32)
    for n in range(n_chunks):
        st_ref[n, 0:LANES, :] = state.astype(BF16)
        state = state * cdf + kv_ref[n, 0:LANES, :]
    state = jnp.zeros((LANES, LANES), F32)
    for n in reversed(range(n_chunks)):
        st_ref[n, LANES:2 * LANES, :] = state.astype(BF16)
        state = state * cdb + kv_ref[n, LANES:2 * LANES, :]

    zero = jnp.zeros((CHUNK, LANES), BF16)
    for n in range(n_chunks):
        sl = pl.ds(n * CHUNK, CHUNK)
        qb = q_ref[0, 0, sl, :]
        kb = k_ref[0, 0, sl, :] * QK_SCALE
        qs = jnp.concatenate([jnp.where(head0, qb, zero), jnp.where(head0, zero, qb)], axis=0)
        p = (_dot_nt(qs, kb) * mask2).astype(BF16)
        p_ref[n, :, 0:CHUNK] = p[0:CHUNK]
        p_ref[n, :, CHUNK:2 * CHUNK] = p[CHUNK:2 * CHUNK]

    for n in range(n_chunks):
        sl = pl.ds(n * CHUNK, CHUNK)
        vb = v_ref[0, 0, sl, :]
        qc = q_ref[0, 0, sl, :].astype(F32)
        lhs = jnp.concatenate([p_ref[n], (qc * qdf).astype(BF16), (qc * qdb).astype(BF16)], axis=1)
        rhs = jnp.concatenate([jnp.where(head0, vb, zero), jnp.where(head0, zero, vb), st_ref[n]],
                              axis=0)
        y_ref[sl, :] = _dot(lhs, rhs)

    gn_avg2 = jnp.concatenate([gn_avg, gn_avg], axis=0)

    def group_mean(z):
        hi, lo = _split_hi_lo(z)
        return _dot(jnp.concatenate([hi, lo], axis=1), gn_avg2)

    for n in range(n_chunks):
        sl = pl.ds(n * CHUNK, CHUNK)
        y = y_ref[sl, :]
        y_ref[sl, :] = y - group_mean(y)

    gain = gain_ref[0]
    for n in range(n_chunks):
        sl = pl.ds(n * CHUNK, CHUNK)
        dlt = y_ref[sl, :]
        var = group_mean(dlt * dlt)
        gate = g_ref[0, 0, sl, :].astype(F32)
        o_ref[0, sl, :] = (dlt * lax.rsqrt(var + GN_EPS) * gain
                           * (gate * jax.nn.sigmoid(gate))).astype(BF16)


def _retention(ret, decf, decb, gain):
    b, _, s, _ = ret.shape
    n_chunks = s // CHUNK

    def part(kind):
        return pl.BlockSpec((1, 1, s, LANES), lambda i, j, kind=kind: (i, kind * RET_PAIRS + j, 0, 0))

    return pl.pallas_call(
        functools.partial(_ret_kernel, seq=s),
        grid=(b, RET_PAIRS),
        in_specs=[
            part(0), part(1), part(2), part(3),
            pl.BlockSpec((1, HEADS_PER_LANE_BLOCK, LANES), lambda i, j: (j, 0, 0)),
            pl.BlockSpec((1, HEADS_PER_LANE_BLOCK, LANES), lambda i, j: (j, 0, 0)),
            pl.BlockSpec((1, 1, LANES), lambda i, j: (j, 0, 0)),
        ],
        out_specs=pl.BlockSpec((1, s, LANES), lambda i, j: (i, 0, j)),
        out_shape=jax.ShapeDtypeStruct((b, s, RET_WIDTH), BF16),
        scratch_shapes=[
            pltpu.VMEM((n_chunks, 2 * LANES, LANES), F32),
            pltpu.VMEM((n_chunks, 2 * LANES, LANES), BF16),
            pltpu.VMEM((n_chunks, CHUNK, 2 * CHUNK), BF16),
            pltpu.VMEM((s, LANES), F32),
        ],
        compiler_params=pltpu.CompilerParams(
            dimension_semantics=("arbitrary", "arbitrary"), vmem_limit_bytes=VMEM_LIMIT_BYTES),
        name="retention",
    )(ret, ret, ret, ret, decf, decb, gain)


_STACK = 2 * BLOCK
_KEYS = 3 * BLOCK


def _attn_kernel(q_ref, kv_ref, slope_ref, sink_ref, o_ref, kvs_ref, vk_ref, vo_ref, bias_ref,
                 *, seq):
    n_blocks = seq // BLOCK
    lane = lax.broadcasted_iota(jnp.int32, (1, LANES), 1)
    low = lane < HEAD_DIM

    kv = kv_ref[0, 0]
    swapped = pltpu.roll(kv.astype(F32), HEAD_DIM, 1).astype(BF16)
    one = jnp.ones_like(kv)
    kvs_ref[...] = swapped
    vk_ref[...] = jnp.where(low, swapped, one)
    vo_ref[...] = jnp.where(low, one, kv)

    r = lax.broadcasted_iota(jnp.int32, (_STACK, _KEYS), 0)
    qi = jnp.where(r < BLOCK, r, r - BLOCK)
    kj = lax.broadcasted_iota(jnp.int32, (_STACK, _KEYS), 1)
    dist = jnp.abs(kj - BLOCK - qi)
    in_window = dist <= WINDOW
    distf = dist.astype(F32)
    for par in range(2):
        slope = slope_ref[0, par]
        slope3 = jnp.concatenate([slope, slope, slope], axis=1)
        bias_ref[par] = jnp.where(in_window, -slope3 * distf, NEG_INF)

    zero = jnp.zeros((BLOCK, LANES), BF16)

    def key_range(n):
        lo_blk, hi_blk = max(n - 1, 0), min(n + 2, n_blocks)
        return pl.ds(lo_blk * BLOCK, (hi_blk - lo_blk) * BLOCK), (lo_blk - (n - 1)) * BLOCK, \
            (hi_blk - lo_blk) * BLOCK

    units = [(n, par) for n in range(n_blocks) for par in range(2)]
    live = [dict() for _ in units]

    def stage_scores(u):
        n, par = units[u]
        keys, c0, width = key_range(n)
        qblk = q_ref[0, 0, pl.ds(n * BLOCK, BLOCK), :] * QK_SCALE
        qa, qb = qblk[:, :LANES], qblk[:, LANES:]
        if par == 0:
            qs = jnp.concatenate([jnp.where(low, qa, zero), jnp.where(low, qb, zero)], axis=0)
            kmat = kv_ref[0, 0, keys, :]
        else:
            qs = jnp.concatenate([jnp.where(low, zero, qa), jnp.where(low, zero, qb)], axis=0)
            kmat = kvs_ref[keys, :]
        live[u]["s"] = _dot_nt(qs, kmat) + bias_ref[par, :, c0:c0 + width]

    def stage_max(u):
        _, par = units[u]
        live[u]["m"] = jnp.maximum(jnp.max(live[u]["s"], axis=1, keepdims=True),
                                   sink_ref[0, par][:, 0:1])

    def stage_exp(u):
        st = live[u]
        st["e"] = jnp.exp(st.pop("s") - st["m"]).astype(BF16)

    def stage_values(u):
        n, par = units[u]
        keys, _, _ = key_range(n)
        vmat = vk_ref[keys, :] if par == 0 else vo_ref[keys, :]
        live[u]["pv"] = _dot(live[u].pop("e"), vmat)

    def stage_out(u):
        n, par = units[u]
        st = live[u]
        pv = st.pop("pv")
        denom = pltpu.roll(pv, HEAD_DIM, 1) + jnp.exp(sink_ref[0, par] - st.pop("m"))
        st["out"] = pv / denom
        if par == 1:
            y = jnp.where(low, live[u - 1].pop("out"), st.pop("out"))
            o_ref[0, pl.ds(n * BLOCK, BLOCK), :] = jnp.concatenate(
                [y[0:BLOCK], y[BLOCK:_STACK]], axis=1).astype(BF16)

    stages = (stage_scores, stage_max, stage_exp, stage_values, stage_out)
    for i in range(len(units) + len(stages) - 1):
        for k, stage in enumerate(stages):
            if 0 <= i - k < len(units):
                stage(i - k)


def _attention(aq, akv, slope_rows, sink_rows):
    b, _, s, qw = aq.shape
    return pl.pallas_call(
        functools.partial(_attn_kernel, seq=s),
        grid=(b, ATTN_KV_HEADS),
        in_specs=[
            pl.BlockSpec((1, 1, s, qw), lambda i, j: (i, j, 0, 0)),
            pl.BlockSpec((1, 1, s, LANES), lambda i, j: (i, j, 0, 0)),
            pl.BlockSpec((1, 2, _STACK, LANES), lambda i, j: (j, 0, 0, 0)),
            pl.BlockSpec((1, 2, _STACK, LANES), lambda i, j: (j, 0, 0, 0)),
        ],
        out_specs=pl.BlockSpec((1, s, qw), lambda i, j: (i, 0, j)),
        out_shape=jax.ShapeDtypeStruct((b, s, ATTN_WIDTH), BF16),
        scratch_shapes=[
            pltpu.VMEM((s, LANES), BF16),
            pltpu.VMEM((s, LANES), BF16),
            pltpu.VMEM((s, LANES), BF16),
            pltpu.VMEM((2, _STACK, _KEYS), F32),
        ],
        compiler_params=pltpu.CompilerParams(
            dimension_semantics=("arbitrary", "arbitrary"), vmem_limit_bytes=VMEM_LIMIT_BYTES),
        name="attention",
    )(aq, akv, slope_rows, sink_rows)


def _layer_norm(z, gain, bias):
    mu = jnp.mean(z, axis=-1, keepdims=True)
    dlt = z - mu
    var = jnp.mean(dlt * dlt, axis=-1, keepdims=True)
    return dlt * lax.rsqrt(var + LN_EPS) * gain + bias


def _ffn_kernel(x_ref, yr_ref, ya_ref, p_ref, wo_ref, g1_ref, b1_ref, wg_ref, wu_ref, wd_ref,
                wpe_ref, wpg_ref, g2_ref, b2_ref, o_ref, *, alpha):
    mix = _dot(yr_ref[...], wo_ref[0:RET_WIDTH, :]) + _dot(ya_ref[...], wo_ref[RET_WIDTH:, :])
    h = _layer_norm(alpha * x_ref[...] + mix, g1_ref[...], b1_ref[...])
    hb = h.astype(BF16)
    g = _dot(hb, wg_ref[...])
    up = _dot(hb, wu_ref[...])
    act = (g * jax.nn.sigmoid(g) * up).astype(BF16)
    ffn = _dot(act, wd_ref[...])
    ple = _dot(p_ref[...].astype(BF16), wpe_ref[...])
    ple_gate = jax.nn.sigmoid(_dot(hb, wpg_ref[...]))
    o_ref[...] = _layer_norm(alpha * h + ffn + ple_gate * ple, g2_ref[...], b2_ref[...])


def _ffn(x, yr, ya, p, wo, g1, b1, wg, wu, wd, wpe, wpg, g2, b2, *, alpha, tm):
    t, d = x.shape

    def rows(width):
        return pl.BlockSpec((tm, width), lambda i: (i, 0))

    def resident(arr):
        return pl.BlockSpec(arr.shape, lambda i: (0, 0), pipeline_mode=pl.Buffered(1))

    return pl.pallas_call(
        functools.partial(_ffn_kernel, alpha=alpha),
        grid=(t // tm,),
        in_specs=[rows(d), rows(RET_WIDTH), rows(ATTN_WIDTH), rows(PLE_DIM),
                  resident(wo), resident(g1), resident(b1), resident(wg), resident(wu),
                  resident(wd), resident(wpe), resident(wpg), resident(g2), resident(b2)],
        out_specs=rows(d),
        out_shape=jax.ShapeDtypeStruct((t, d), F32),
        compiler_params=pltpu.CompilerParams(
            dimension_semantics=("arbitrary",), vmem_limit_bytes=VMEM_LIMIT_BYTES),
        name="ffn",
    )(x, yr, ya, p, wo, g1, b1, wg, wu, wd, wpe, wpg, g2, b2)


def _arrange_w_in(w):
    kv0 = 4 * RET_WIDTH + ATTN_WIDTH
    ak, av = w[:, kv0:kv0 + KV_WIDTH], w[:, kv0 + KV_WIDTH:]
    kv_cols = []
    for h in range(ATTN_KV_HEADS):
        kv_cols += [ak[:, h * HEAD_DIM:(h + 1) * HEAD_DIM], av[:, h * HEAD_DIM:(h + 1) * HEAD_DIM]]
    return jnp.concatenate([w[:, :kv0]] + kv_cols, axis=1).astype(BF16)


def _stacked_head_rows(per_head):
    v = per_head.astype(F32).reshape(ATTN_KV_HEADS, 2, 2)
    v = jnp.transpose(v, (0, 2, 1))
    v = jnp.repeat(v, BLOCK, axis=2)
    return jnp.broadcast_to(v[..., None], v.shape + (LANES,))


def kernel(x, p, w_in, ret_decay_fwd, ret_decay_bwd, ret_gn_gain, attn_sink, w_out, ln1_gain, ln1_bias,
           w_ffn_gate, w_ffn_up, w_ffn_down, w_ple_proj, w_ple_gate, ln2_gain, ln2_bias):
    depth = w_in.shape[0]
    b, s, d = x.shape
    alpha = (2.0 * depth) ** 0.25
    slopes = 2.0 ** (-8.0 * (jnp.arange(ATTN_HEADS, dtype=F32) + 1.0) / ATTN_HEADS)
    slope_rows = _stacked_head_rows(slopes)

    def lane_rep(v):
        v = v.astype(F32).reshape(RET_PAIRS, HEADS_PER_LANE_BLOCK, 1)
        return jnp.broadcast_to(v, (RET_PAIRS, HEADS_PER_LANE_BLOCK, LANES))

    row = lambda v: v.astype(F32).reshape(1, -1)
    h = x.astype(F32)
    for i in range(depth):
        ret, aq, akv = _in_proj(h, _arrange_w_in(w_in[i]), tm=512)
        y_ret = _retention(ret, lane_rep(ret_decay_fwd[i]), lane_rep(ret_decay_bwd[i]),
                           ret_gn_gain[i].astype(F32).reshape(RET_PAIRS, 1, LANES))
        y_att = _attention(aq, akv, slope_rows, _stacked_head_rows(attn_sink[i]))
        h = _ffn(h.reshape(b * s, d), y_ret.reshape(b * s, RET_WIDTH), y_att.reshape(b * s, ATTN_WIDTH),
                 p[i].reshape(b * s, PLE_DIM),
                 w_out[i].astype(BF16), row(ln1_gain[i]), row(ln1_bias[i]),
                 w_ffn_gate[i].astype(BF16), w_ffn_up[i].astype(BF16), w_ffn_down[i].astype(BF16),
                 w_ple_proj[i].astype(BF16), w_ple_gate[i].astype(BF16),
                 row(ln2_gain[i]), row(ln2_bias[i]), alpha=alpha, tm=512).reshape(b, s, d)
    return h.astype(x.dtype)
```

```python
import functools

import jax
import jax.numpy as jnp
from jax import lax
from jax.experimental import pallas as pl
from jax.experimental.pallas import tpu as pltpu

D_MODEL = 1024
HEAD_DIM = 64
RET_HEADS = 8
ATTN_HEADS = 8
ATTN_KV_HEADS = 2
ATTN_GROUP = ATTN_HEADS // ATTN_KV_HEADS
RET_WIDTH = RET_HEADS * HEAD_DIM
ATTN_WIDTH = ATTN_HEADS * HEAD_DIM
KV_WIDTH = ATTN_KV_HEADS * HEAD_DIM
IN_PROJ_WIDTH = 4 * RET_WIDTH + ATTN_WIDTH + 2 * KV_WIDTH
CHUNK = 128
WINDOW = 128
BLOCK = 128
PLE_DIM = 256
FFN_HIDDEN = ((8 * D_MODEL + 767) // 768) * 256
LN_EPS = 1e-5
GN_EPS = 1e-5
NEG_INF = -1e30
QK_SCALE = HEAD_DIM ** -0.5

LANES = 128
HEADS_PER_LANE_BLOCK = LANES // HEAD_DIM
RET_PAIRS = RET_HEADS // HEADS_PER_LANE_BLOCK
VMEM_LIMIT_BYTES = 56 * 1024 * 1024

F32 = jnp.float32
BF16 = jnp.bfloat16


def _dot(a, b):
    return jnp.dot(a, b, preferred_element_type=F32)


def _dot_nt(a, b):
    return lax.dot_general(a, b, (((1,), (1,)), ((), ())), preferred_element_type=F32)


def _dot_tn(a, b):
    return lax.dot_general(a, b, (((0,), (0,)), ((), ())), preferred_element_type=F32)


_IN_PROJ_COLS = 256
_RET_BLOCKS = 4 * RET_PAIRS


def _in_proj_kernel(x_ref, w_ref, ret_ref, aq_ref, akv_ref):
    xb = x_ref[0].astype(BF16)
    for c in range(IN_PROJ_WIDTH // _IN_PROJ_COLS):
        u = _dot(xb, w_ref[:, c * _IN_PROJ_COLS:(c + 1) * _IN_PROJ_COLS]).astype(BF16)
        if 2 * c < _RET_BLOCKS:
            ret_ref[0, 2 * c] = u[:, :LANES]
            ret_ref[0, 2 * c + 1] = u[:, LANES:]
        elif c < _RET_BLOCKS // 2 + ATTN_KV_HEADS:
            aq_ref[0, c - _RET_BLOCKS // 2] = u
        else:
            akv_ref[0, 0] = u[:, :LANES]
            akv_ref[0, 1] = u[:, LANES:]


def _in_proj(x, w, tm):
    b, s, d = x.shape
    return pl.pallas_call(
        _in_proj_kernel,
        grid=(b, s // tm),
        in_specs=[
            pl.BlockSpec((1, tm, d), lambda i, j: (i, j, 0)),
            pl.BlockSpec((d, IN_PROJ_WIDTH), lambda i, j: (0, 0)),
        ],
        out_specs=[
            pl.BlockSpec((1, _RET_BLOCKS, tm, LANES), lambda i, j: (i, 0, j, 0)),
            pl.BlockSpec((1, ATTN_KV_HEADS, tm, ATTN_GROUP * HEAD_DIM), lambda i, j: (i, 0, j, 0)),
            pl.BlockSpec((1, ATTN_KV_HEADS, tm, LANES), lambda i, j: (i, 0, j, 0)),
        ],
        out_shape=[
            jax.ShapeDtypeStruct((b, _RET_BLOCKS, s, LANES), BF16),
            jax.ShapeDtypeStruct((b, ATTN_KV_HEADS, s, ATTN_GROUP * HEAD_DIM), BF16),
            jax.ShapeDtypeStruct((b, ATTN_KV_HEADS, s, LANES), BF16),
        ],
        compiler_params=pltpu.CompilerParams(
            dimension_semantics=("arbitrary", "arbitrary"), vmem_limit_bytes=VMEM_LIMIT_BYTES),
        name="in_proj",
    )(x, w)


def _split_hi_lo(z):
    hi = z.astype(BF16)
    lo = (z - hi.astype(F32)).astype(BF16)
    return hi, lo


def _ret_kernel(q_ref, k_ref, v_ref, g_ref, decf_ref, decb_ref, gain_ref, o_ref, kv_ref, st_ref,
                p_ref, y_ref, *, seq):
    n_chunks = seq // CHUNK
    lane = lax.broadcasted_iota(jnp.int32, (1, LANES), 1)
    head0 = lane < HEAD_DIM

    def log_gamma(dec):
        return jnp.log1p(-jnp.exp2(dec))

    lgf0, lgf1 = log_gamma(decf_ref[0, 0:1, :]), log_gamma(decf_ref[0, 1:2, :])
    lgb0, lgb1 = log_gamma(decb_ref[0, 0:1, :]), log_gamma(decb_ref[0, 1:2, :])
    lgf = jnp.where(head0, lgf0, lgf1)
    lgb = jnp.where(head0, lgb0, lgb1)

    idx = lax.broadcasted_iota(jnp.int32, (CHUNK, LANES), 0).astype(F32)
    qdf = jnp.exp(lgf * (idx + 1.0))
    kdf = jnp.exp(lgf * (CHUNK - 1.0 - idx))
    qdb = jnp.exp(lgb * (CHUNK - idx))
    kdb = jnp.exp(lgb * idx)
    cdf = jnp.exp(lgf * CHUNK)
    cdb = jnp.exp(lgb * CHUNK)

    ri = lax.broadcasted_iota(jnp.int32, (CHUNK, CHUNK), 0)
    ci = lax.broadcasted_iota(jnp.int32, (CHUNK, CHUNK), 1)
    diff = (ri - ci).astype(F32)

    def intra_mask(lf, lb):
        fwd = jnp.exp(lf * jnp.maximum(diff, 0.0))
        bwd = jnp.exp(lb * jnp.maximum(-diff, 0.0))
        return jnp.where(diff > 0, fwd, jnp.where(diff < 0, bwd, fwd + bwd))

    mask2 = jnp.concatenate([intra_mask(lgf0, lgb0), intra_mask(lgf1, lgb1)], axis=0)
    same_head = (ri < HEAD_DIM) == (ci < HEAD_DIM)
    gn_avg = jnp.where(same_head, 1.0 / HEAD_DIM, 0.0).astype(BF16)

    for n in range(n_chunks):
        sl = pl.ds(n * CHUNK, CHUNK)
        kc = k_ref[0, 0, sl, :].astype(F32) * QK_SCALE
        kd = jnp.concatenate([(kc * kdf).astype(BF16), (kc * kdb).astype(BF16)], axis=1)
        kv = _dot_tn(kd, v_ref[0, 0, sl, :])
        kv_ref[n, 0:LANES, :] = jnp.where(same_head, kv[0:LANES], 0.0)
        kv_ref[n, LANES:2 * LANES, :] = jnp.where(same_head, kv[LANES:2 * LANES], 0.0)

    state = jnp.zeros((LANES, LANES), F32)
    for n in range(n_chunks):
        st_ref[n, 0:LANES, :] = state.astype(BF16)
        state = state * cdf + kv_ref[n, 0:LANES, :]
    state = jnp.zeros((LANES, LANES), F32)
    for n in reversed(range(n_chunks)):
        st_ref[n, LANES:2 * LANES, :] = state.astype(BF16)
        state = state * cdb + kv_ref[n, LANES:2 * LANES, :]

    zero = jnp.zeros((CHUNK, LANES), BF16)
    for n in range(n_chunks):
        sl = pl.ds(n * CHUNK, CHUNK)
        qb = q_ref[0, 0, sl, :]
        kb = k_ref[0, 0, sl, :] * QK_SCALE
        qs = jnp.concatenate([jnp.where(head0, qb, zero), jnp.where(head0, zero, qb)], axis=0)
        p = (_dot_nt(qs, kb) * mask2).astype(BF16)
        p_ref[n, :, 0:CHUNK] = p[0:CHUNK]
        p_ref[n, :, CHUNK:2 * CHUNK] = p[CHUNK:2 * CHUNK]

    for n in range(n_chunks):
        sl = pl.ds(n * CHUNK, CHUNK)
        vb = v_ref[0, 0, sl, :]
        qc = q_ref[0, 0, sl, :].astype(F32)
        lhs = jnp.concatenate([p_ref[n], (qc * qdf).astype(BF16), (qc * qdb).astype(BF16)], axis=1)
        rhs = jnp.concatenate([jnp.where(head0, vb, zero), jnp.where(head0, zero, vb), st_ref[n]],
                              axis=0)
        y_ref[sl, :] = _dot(lhs, rhs)

    gn_avg2 = jnp.concatenate([gn_avg, gn_avg], axis=0)

    def group_mean(z):
        hi, lo = _split_hi_lo(z)
        return _dot(jnp.concatenate([hi, lo], axis=1), gn_avg2)

    for n in range(n_chunks):
        sl = pl.ds(n * CHUNK, CHUNK)
        y = y_ref[sl, :]
        y_ref[sl, :] = y - group_mean(y)

    gain = gain_ref[0]
    for n in range(n_chunks):
        sl = pl.ds(n * CHUNK, CHUNK)
        dlt = y_ref[sl, :]
        var = group_mean(dlt * dlt)
        gate = g_ref[0, 0, sl, :].astype(F32)
        o_ref[0, sl, :] = (dlt * lax.rsqrt(var + GN_EPS) * gain
                           * (gate * jax.nn.sigmoid(gate))).astype(BF16)


def _retention(ret, decf, decb, gain):
    b, _, s, _ = ret.shape
    n_chunks = s // CHUNK

    def part(kind):
        return pl.BlockSpec((1, 1, s, LANES), lambda i, j, kind=kind: (i, kind * RET_PAIRS + j, 0, 0))

    return pl.pallas_call(
        functools.partial(_ret_kernel, seq=s),
        grid=(b, RET_PAIRS),
        in_specs=[
            part(0), part(1), part(2), part(3),
            pl.BlockSpec((1, HEADS_PER_LANE_BLOCK, LANES), lambda i, j: (j, 0, 0)),
            pl.BlockSpec((1, HEADS_PER_LANE_BLOCK, LANES), lambda i, j: (j, 0, 0)),
            pl.BlockSpec((1, 1, LANES), lambda i, j: (j, 0, 0)),
        ],
        out_specs=pl.BlockSpec((1, s, LANES), lambda i, j: (i, 0, j)),
        out_shape=jax.ShapeDtypeStruct((b, s, RET_WIDTH), BF16),
        scratch_shapes=[
            pltpu.VMEM((n_chunks, 2 * LANES, LANES), F32),
            pltpu.VMEM((n_chunks, 2 * LANES, LANES), BF16),
            pltpu.VMEM((n_chunks, CHUNK, 2 * CHUNK), BF16),
            pltpu.VMEM((s, LANES), F32),
        ],
        compiler_params=pltpu.CompilerParams(
            dimension_semantics=("arbitrary", "arbitrary"), vmem_limit_bytes=VMEM_LIMIT_BYTES),
        name="retention",
    )(ret, ret, ret, ret, decf, decb, gain)


_STACK = 2 * BLOCK
_KEYS = 3 * BLOCK
_SUBLANES = 8


def _attn_kernel(q_ref, kv_ref, slope_ref, sink_ref, o_ref, kvs_ref, vt_ref, bias_ref, *, seq):
    n_blocks = seq // BLOCK
    lane = lax.broadcasted_iota(jnp.int32, (1, LANES), 1)
    low = lane < HEAD_DIM

    kv = kv_ref[0, 0]
    kvs_ref[...] = pltpu.roll(kv.astype(F32), HEAD_DIM, 1).astype(BF16)
    kv_t = jnp.transpose(kv.astype(F32))
    vt_ref[...] = jnp.concatenate(
        [kv_t[HEAD_DIM:], jnp.ones((HEAD_DIM, seq), F32)], axis=0).astype(BF16)

    kj = lax.broadcasted_iota(jnp.int32, (_KEYS, _STACK), 0)
    c = lax.broadcasted_iota(jnp.int32, (_KEYS, _STACK), 1)
    qi = jnp.where(c < BLOCK, c, c - BLOCK)
    dist = jnp.abs(kj - BLOCK - qi)
    for par in range(2):
        slope = slope_ref[0, par, 0:1, :]
        bias_ref[par] = jnp.where(dist <= WINDOW, -slope * dist.astype(F32), NEG_INF)

    zero = jnp.zeros((BLOCK, LANES), BF16)

    def key_range(n):
        lo_blk, hi_blk = max(n - 1, 0), min(n + 2, n_blocks)
        return pl.ds(lo_blk * BLOCK, (hi_blk - lo_blk) * BLOCK), (lo_blk - (n - 1)) * BLOCK, \
            (hi_blk - lo_blk) * BLOCK

    units = [(n, par) for n in range(n_blocks) for par in range(2)]
    live = [dict() for _ in units]

    def stage_scores(u):
        n, par = units[u]
        keys, c0, width = key_range(n)
        qblk = q_ref[0, 0, pl.ds(n * BLOCK, BLOCK), :] * QK_SCALE
        qa, qb = qblk[:, :LANES], qblk[:, LANES:]
        if par == 0:
            qs = jnp.concatenate([jnp.where(low, qa, zero), jnp.where(low, qb, zero)], axis=0)
            kmat = kv_ref[0, 0, keys, :]
        else:
            qs = jnp.concatenate([jnp.where(low, zero, qa), jnp.where(low, zero, qb)], axis=0)
            kmat = kvs_ref[keys, :]
        live[u]["s"] = _dot_nt(kmat, qs) + bias_ref[par, c0:c0 + width, :]

    def stage_softmax(u):
        _, par = units[u]
        st = live[u]
        s = st.pop("s")
        m = jnp.maximum(jnp.max(s, axis=0, keepdims=True), sink_ref[0, par, 0:1, :])
        st["e"] = jnp.exp(s - m).astype(BF16)
        st["esink"] = jnp.exp(sink_ref[0, par, 0:1, :] - m)

    def stage_values(u):
        n, _ = units[u]
        keys, _, _ = key_range(n)
        live[u]["pv"] = _dot(vt_ref[:, keys], live[u].pop("e"))

    def stage_out(u):
        n, par = units[u]
        st = live[u]
        pv = st.pop("pv")
        st["out"] = pv[:HEAD_DIM] / (pv[HEAD_DIM:] + st.pop("esink"))
        if par == 1:
            even, odd = live[u - 1].pop("out"), st.pop("out")
            y_t = jnp.concatenate(
                [even[:, :BLOCK], odd[:, :BLOCK], even[:, BLOCK:], odd[:, BLOCK:]], axis=0)
            o_ref[0, pl.ds(n * BLOCK, BLOCK), :] = jnp.transpose(y_t).astype(BF16)

    stages = (stage_scores, stage_softmax, stage_values, stage_out)
    for i in range(len(units) + len(stages) - 1):
        for k, stage in enumerate(stages):
            if 0 <= i - k < len(units):
                stage(i - k)


def _attention(aq, akv, slope_cols, sink_cols):
    b, _, s, qw = aq.shape
    return pl.pallas_call(
        functools.partial(_attn_kernel, seq=s),
        grid=(b, ATTN_KV_HEADS),
        in_specs=[
            pl.BlockSpec((1, 1, s, qw), lambda i, j: (i, j, 0, 0)),
            pl.BlockSpec((1, 1, s, LANES), lambda i, j: (i, j, 0, 0)),
            pl.BlockSpec((1, 2, _SUBLANES, _STACK), lambda i, j: (j, 0, 0, 0)),
            pl.BlockSpec((1, 2, _SUBLANES, _STACK), lambda i, j: (j, 0, 0, 0)),
        ],
        out_specs=pl.BlockSpec((1, s, qw), lambda i, j: (i, 0, j)),
        out_shape=jax.ShapeDtypeStruct((b, s, ATTN_WIDTH), BF16),
        scratch_shapes=[
            pltpu.VMEM((s, LANES), BF16),
            pltpu.VMEM((LANES, s), BF16),
            pltpu.VMEM((2, _KEYS, _STACK), F32),
        ],
        compiler_params=pltpu.CompilerParams(
            dimension_semantics=("arbitrary", "arbitrary"), vmem_limit_bytes=VMEM_LIMIT_BYTES),
        name="attention",
    )(aq, akv, slope_cols, sink_cols)


def _layer_norm(z, gain, bias):
    mu = jnp.mean(z, axis=-1, keepdims=True)
    dlt = z - mu
    var = jnp.mean(dlt * dlt, axis=-1, keepdims=True)
    return dlt * lax.rsqrt(var + LN_EPS) * gain + bias


def _ffn_kernel(x_ref, yr_ref, ya_ref, p_ref, wo_ref, g1_ref, b1_ref, wg_ref, wu_ref, wd_ref,
                wpe_ref, wpg_ref, g2_ref, b2_ref, o_ref, *, alpha):
    mix = _dot(yr_ref[...], wo_ref[0:RET_WIDTH, :]) + _dot(ya_ref[...], wo_ref[RET_WIDTH:, :])
    h = _layer_norm(alpha * x_ref[...] + mix, g1_ref[...], b1_ref[...])
    hb = h.astype(BF16)
    g = _dot(hb, wg_ref[...])
    up = _dot(hb, wu_ref[...])
    act = (g * jax.nn.sigmoid(g) * up).astype(BF16)
    ffn = _dot(act, wd_ref[...])
    ple = _dot(p_ref[...].astype(BF16), wpe_ref[...])
    ple_gate = jax.nn.sigmoid(_dot(hb, wpg_ref[...]))
    o_ref[...] = _layer_norm(alpha * h + ffn + ple_gate * ple, g2_ref[...], b2_ref[...])


def _ffn(x, yr, ya, p, wo, g1, b1, wg, wu, wd, wpe, wpg, g2, b2, *, alpha, tm):
    t, d = x.shape

    def rows(width):
        return pl.BlockSpec((tm, width), lambda i: (i, 0))

    def resident(arr):
        return pl.BlockSpec(arr.shape, lambda i: (0, 0), pipeline_mode=pl.Buffered(1))

    return pl.pallas_call(
        functools.partial(_ffn_kernel, alpha=alpha),
        grid=(t // tm,),
        in_specs=[rows(d), rows(RET_WIDTH), rows(ATTN_WIDTH), rows(PLE_DIM),
                  resident(wo), resident(g1), resident(b1), resident(wg), resident(wu),
                  resident(wd), resident(wpe), resident(wpg), resident(g2), resident(b2)],
        out_specs=rows(d),
        out_shape=jax.ShapeDtypeStruct((t, d), F32),
        compiler_params=pltpu.CompilerParams(
            dimension_semantics=("arbitrary",), vmem_limit_bytes=VMEM_LIMIT_BYTES),
        name="ffn",
    )(x, yr, ya, p, wo, g1, b1, wg, wu, wd, wpe, wpg, g2, b2)


def _arrange_w_in(w):
    kv0 = 4 * RET_WIDTH + ATTN_WIDTH
    ak, av = w[:, kv0:kv0 + KV_WIDTH], w[:, kv0 + KV_WIDTH:]
    kv_cols = []
    for h in range(ATTN_KV_HEADS):
        kv_cols += [ak[:, h * HEAD_DIM:(h + 1) * HEAD_DIM], av[:, h * HEAD_DIM:(h + 1) * HEAD_DIM]]
    return jnp.concatenate([w[:, :kv0]] + kv_cols, axis=1).astype(BF16)


def _stacked_head_cols(per_head):
    v = per_head.astype(F32).reshape(ATTN_KV_HEADS, 2, 2)
    v = jnp.transpose(v, (0, 2, 1))
    v = jnp.repeat(v, BLOCK, axis=2)
    return jnp.broadcast_to(v[:, :, None, :], v.shape[:2] + (_SUBLANES, _STACK))


def kernel(x, p, w_in, ret_decay_fwd, ret_decay_bwd, ret_gn_gain, attn_sink, w_out, ln1_gain, ln1_bias,
           w_ffn_gate, w_ffn_up, w_ffn_down, w_ple_proj, w_ple_gate, ln2_gain, ln2_bias):
    depth = w_in.shape[0]
    b, s, d = x.shape
    alpha = (2.0 * depth) ** 0.25
    slopes = 2.0 ** (-8.0 * (jnp.arange(ATTN_HEADS, dtype=F32) + 1.0) / ATTN_HEADS)
    slope_cols = _stacked_head_cols(slopes)

    def lane_rep(v):
        v = v.astype(F32).reshape(RET_PAIRS, HEADS_PER_LANE_BLOCK, 1)
        return jnp.broadcast_to(v, (RET_PAIRS, HEADS_PER_LANE_BLOCK, LANES))

    row = lambda v: v.astype(F32).reshape(1, -1)
    h = x.astype(F32)
    for i in range(depth):
        ret, aq, akv = _in_proj(h, _arrange_w_in(w_in[i]), tm=512)
        y_ret = _retention(ret, lane_rep(ret_decay_fwd[i]), lane_rep(ret_decay_bwd[i]),
                           ret_gn_gain[i].astype(F32).reshape(RET_PAIRS, 1, LANES))
        y_att = _attention(aq, akv, slope_cols, _stacked_head_cols(attn_sink[i]))
        h = _ffn(h.reshape(b * s, d), y_ret.reshape(b * s, RET_WIDTH), y_att.reshape(b * s, ATTN_WIDTH),
                 p[i].reshape(b * s, PLE_DIM),
                 w_out[i].astype(BF16), row(ln1_gain[i]), row(ln1_bias[i]),
                 w_ffn_gate[i].astype(BF16), w_ffn_up[i].astype(BF16), w_ffn_down[i].astype(BF16),
                 w_ple_proj[i].astype(BF16), w_ple_gate[i].astype(BF16),
                 row(ln2_gain[i]), row(ln2_bias[i]), alpha=alpha, tm=512).reshape(b, s, d)
    return h.astype(x.dtype)
```

```python
import functools

import jax
import jax.numpy as jnp
from jax import lax
from jax.experimental import pallas as pl
from jax.experimental.pallas import tpu as pltpu

D_MODEL = 1024
HEAD_DIM = 64
RET_HEADS = 8
ATTN_HEADS = 8
ATTN_KV_HEADS = 2
ATTN_GROUP = ATTN_HEADS // ATTN_KV_HEADS
RET_WIDTH = RET_HEADS * HEAD_DIM
ATTN_WIDTH = ATTN_HEADS * HEAD_DIM
KV_WIDTH = ATTN_KV_HEADS * HEAD_DIM
IN_PROJ_WIDTH = 4 * RET_WIDTH + ATTN_WIDTH + 2 * KV_WIDTH
CHUNK = 128
WINDOW = 128
BLOCK = 128
PLE_DIM = 256
FFN_HIDDEN = ((8 * D_MODEL + 767) // 768) * 256
LN_EPS = 1e-5
GN_EPS = 1e-5
NEG_INF = -1e30
QK_SCALE = HEAD_DIM ** -0.5

LANES = 128
HEADS_PER_LANE_BLOCK = LANES // HEAD_DIM
RET_PAIRS = RET_HEADS // HEADS_PER_LANE_BLOCK
VMEM_LIMIT_BYTES = 56 * 1024 * 1024

F32 = jnp.float32
BF16 = jnp.bfloat16


def _dot(a, b):
    return jnp.dot(a, b, preferred_element_type=F32)


def _dot_nt(a, b):
    return lax.dot_general(a, b, (((1,), (1,)), ((), ())), preferred_element_type=F32)


def _dot_tn(a, b):
    return lax.dot_general(a, b, (((0,), (0,)), ((), ())), preferred_element_type=F32)


_IN_PROJ_COLS = 256
_RET_BLOCKS = 4 * RET_PAIRS


def _in_proj_kernel(x_ref, w_ref, ret_ref, aq_ref, akv_ref):
    xb = x_ref[0].astype(BF16)
    for c in range(IN_PROJ_WIDTH // _IN_PROJ_COLS):
        u = _dot(xb, w_ref[:, c * _IN_PROJ_COLS:(c + 1) * _IN_PROJ_COLS]).astype(BF16)
        if 2 * c < _RET_BLOCKS:
            ret_ref[0, 2 * c] = u[:, :LANES]
            ret_ref[0, 2 * c + 1] = u[:, LANES:]
        elif c < _RET_BLOCKS // 2 + ATTN_KV_HEADS:
            aq_ref[0, c - _RET_BLOCKS // 2] = u
        else:
            akv_ref[0, 0] = u[:, :LANES]
            akv_ref[0, 1] = u[:, LANES:]


def _in_proj(x, w, tm):
    b, s, d = x.shape
    return pl.pallas_call(
        _in_proj_kernel,
        grid=(b, s // tm),
        in_specs=[
            pl.BlockSpec((1, tm, d), lambda i, j: (i, j, 0)),
            pl.BlockSpec((d, IN_PROJ_WIDTH), lambda i, j: (0, 0)),
        ],
        out_specs=[
            pl.BlockSpec((1, _RET_BLOCKS, tm, LANES), lambda i, j: (i, 0, j, 0)),
            pl.BlockSpec((1, ATTN_KV_HEADS, tm, ATTN_GROUP * HEAD_DIM), lambda i, j: (i, 0, j, 0)),
            pl.BlockSpec((1, ATTN_KV_HEADS, tm, LANES), lambda i, j: (i, 0, j, 0)),
        ],
        out_shape=[
            jax.ShapeDtypeStruct((b, _RET_BLOCKS, s, LANES), BF16),
            jax.ShapeDtypeStruct((b, ATTN_KV_HEADS, s, ATTN_GROUP * HEAD_DIM), BF16),
            jax.ShapeDtypeStruct((b, ATTN_KV_HEADS, s, LANES), BF16),
        ],
        compiler_params=pltpu.CompilerParams(
            dimension_semantics=("arbitrary", "arbitrary"), vmem_limit_bytes=VMEM_LIMIT_BYTES),
        name="in_proj",
    )(x, w)


def _split_hi_lo(z):
    hi = z.astype(BF16)
    lo = (z - hi.astype(F32)).astype(BF16)
    return hi, lo


def _ret_kernel(q_ref, k_ref, v_ref, g_ref, decf_ref, decb_ref, gain_ref, o_ref, kv_ref, st_ref,
                p_ref, y_ref, *, seq):
    n_chunks = seq // CHUNK
    lane = lax.broadcasted_iota(jnp.int32, (1, LANES), 1)
    head0 = lane < HEAD_DIM

    def log_gamma(dec):
        return jnp.log1p(-jnp.exp2(dec))

    lgf0, lgf1 = log_gamma(decf_ref[0, 0:1, :]), log_gamma(decf_ref[0, 1:2, :])
    lgb0, lgb1 = log_gamma(decb_ref[0, 0:1, :]), log_gamma(decb_ref[0, 1:2, :])
    lgf = jnp.where(head0, lgf0, lgf1)
    lgb = jnp.where(head0, lgb0, lgb1)

    idx = lax.broadcasted_iota(jnp.int32, (CHUNK, LANES), 0).astype(F32)
    qdf = jnp.exp(lgf * (idx + 1.0))
    kdf = jnp.exp(lgf * (CHUNK - 1.0 - idx))
    qdb = jnp.exp(lgb * (CHUNK - idx))
    kdb = jnp.exp(lgb * idx)
    cdf = jnp.exp(lgf * CHUNK)
    cdb = jnp.exp(lgb * CHUNK)

    ri = lax.broadcasted_iota(jnp.int32, (CHUNK, CHUNK), 0)
    ci = lax.broadcasted_iota(jnp.int32, (CHUNK, CHUNK), 1)
    diff = (ri - ci).astype(F32)

    def intra_mask(lf, lb):
        fwd = jnp.exp(lf * jnp.maximum(diff, 0.0))
        bwd = jnp.exp(lb * jnp.maximum(-diff, 0.0))
        return jnp.where(diff > 0, fwd, jnp.where(diff < 0, bwd, fwd + bwd))

    mask2 = jnp.concatenate([intra_mask(lgf0, lgb0), intra_mask(lgf1, lgb1)], axis=0)
    same_head = (ri < HEAD_DIM) == (ci < HEAD_DIM)
    gn_avg = jnp.where(same_head, 1.0 / HEAD_DIM, 0.0).astype(BF16)

    for n in range(n_chunks):
        sl = pl.ds(n * CHUNK, CHUNK)
        kc = k_ref[0, 0, sl, :].astype(F32) * QK_SCALE
        kd = jnp.concatenate([(kc * kdf).astype(BF16), (kc * kdb).astype(BF16)], axis=1)
        kv = _dot_tn(kd, v_ref[0, 0, sl, :])
        kv_ref[n, 0:LANES, :] = jnp.where(same_head, kv[0:LANES], 0.0)
        kv_ref[n, LANES:2 * LANES, :] = jnp.where(same_head, kv[LANES:2 * LANES], 0.0)

    state = jnp.zeros((LANES, LANES), F32)
    for n in range(n_chunks):
        st_ref[n, 0:LANES, :] = state.astype(BF16)
        state = state * cdf + kv_ref[n, 0:LANES, :]
    state = jnp.zeros((LANES, LANES), F32)
    for n in reversed(range(n_chunks)):
        st_ref[n, LANES:2 * LANES, :] = state.astype(BF16)
        state = state * cdb + kv_ref[n, LANES:2 * LANES, :]

    zero = jnp.zeros((CHUNK, LANES), BF16)
    for n in range(n_chunks):
        sl = pl.ds(n * CHUNK, CHUNK)
        qb = q_ref[0, 0, sl, :]
        kb = k_ref[0, 0, sl, :] * QK_SCALE
        qs = jnp.concatenate([jnp.where(head0, qb, zero), jnp.where(head0, zero, qb)], axis=0)
        p = (_dot_nt(qs, kb) * mask2).astype(BF16)
        p_ref[n, :, 0:CHUNK] = p[0:CHUNK]
        p_ref[n, :, CHUNK:2 * CHUNK] = p[CHUNK:2 * CHUNK]

    for n in range(n_chunks):
        sl = pl.ds(n * CHUNK, CHUNK)
        vb = v_ref[0, 0, sl, :]
        qc = q_ref[0, 0, sl, :].astype(F32)
        lhs = jnp.concatenate([p_ref[n], (qc * qdf).astype(BF16), (qc * qdb).astype(BF16)], axis=1)
        rhs = jnp.concatenate([jnp.where(head0, vb, zero), jnp.where(head0, zero, vb), st_ref[n]],
                              axis=0)
        y_ref[sl, :] = _dot(lhs, rhs)

    gn_avg2 = jnp.concatenate([gn_avg, gn_avg], axis=0)

    def group_mean(z):
        hi, lo = _split_hi_lo(z)
        return _dot(jnp.concatenate([hi, lo], axis=1), gn_avg2)

    for n in range(n_chunks):
        sl = pl.ds(n * CHUNK, CHUNK)
        y = y_ref[sl, :]
        y_ref[sl, :] = y - group_mean(y)

    gain = gain_ref[0]
    for n in range(n_chunks):
        sl = pl.ds(n * CHUNK, CHUNK)
        dlt = y_ref[sl, :]
        var = group_mean(dlt * dlt)
        gate = g_ref[0, 0, sl, :].astype(F32)
        o_ref[0, sl, :] = (dlt * lax.rsqrt(var + GN_EPS) * gain
                           * (gate * jax.nn.sigmoid(gate))).astype(BF16)


def _retention(ret, decf, decb, gain):
    b, _, s, _ = ret.shape
    n_chunks = s // CHUNK

    def part(kind):
        return pl.BlockSpec((1, 1, s, LANES), lambda i, j, kind=kind: (i, kind * RET_PAIRS + j, 0, 0))

    return pl.pallas_call(
        functools.partial(_ret_kernel, seq=s),
        grid=(b, RET_PAIRS),
        in_specs=[
            part(0), part(1), part(2), part(3),
            pl.BlockSpec((1, HEADS_PER_LANE_BLOCK, LANES), lambda i, j: (j, 0, 0)),
            pl.BlockSpec((1, HEADS_PER_LANE_BLOCK, LANES), lambda i, j: (j, 0, 0)),
            pl.BlockSpec((1, 1, LANES), lambda i, j: (j, 0, 0)),
        ],
        out_specs=pl.BlockSpec((1, s, LANES), lambda i, j: (i, 0, j)),
        out_shape=jax.ShapeDtypeStruct((b, s, RET_WIDTH), BF16),
        scratch_shapes=[
            pltpu.VMEM((n_chunks, 2 * LANES, LANES), F32),
            pltpu.VMEM((n_chunks, 2 * LANES, LANES), BF16),
            pltpu.VMEM((n_chunks, CHUNK, 2 * CHUNK), BF16),
            pltpu.VMEM((s, LANES), F32),
        ],
        compiler_params=pltpu.CompilerParams(
            dimension_semantics=("arbitrary", "arbitrary"), vmem_limit_bytes=VMEM_LIMIT_BYTES),
        name="retention",
    )(ret, ret, ret, ret, decf, decb, gain)


_STACK = 2 * BLOCK
_KEYS = 3 * BLOCK
_SUBLANES = 8
_LOG2E = 1.4426950408889634


def _attn_kernel(q_ref, kv_ref, slope_ref, sink_ref, o_ref, ks_ref, kss_ref, vt_ref, bias_ref,
                 *, seq):
    n_blocks = seq // BLOCK
    lane = lax.broadcasted_iota(jnp.int32, (1, LANES), 1)
    low = lane < HEAD_DIM

    kv = kv_ref[0, 0].astype(F32)
    ks = kv * (QK_SCALE * _LOG2E)
    ks_ref[...] = ks.astype(BF16)
    kss_ref[...] = pltpu.roll(ks, HEAD_DIM, 1).astype(BF16)
    kv_t = jnp.transpose(kv)
    vt_ref[...] = jnp.concatenate(
        [kv_t[HEAD_DIM:], jnp.ones((HEAD_DIM, seq), F32)], axis=0).astype(BF16)

    kj = lax.broadcasted_iota(jnp.int32, (_KEYS, _STACK), 0)
    c = lax.broadcasted_iota(jnp.int32, (_KEYS, _STACK), 1)
    qi = jnp.where(c < BLOCK, c, c - BLOCK)
    dist = jnp.abs(kj - BLOCK - qi)
    for par in range(2):
        slope = slope_ref[0, par, 0:1, :]
        bias_ref[par] = jnp.where(dist <= WINDOW, -slope * dist.astype(F32), NEG_INF) * _LOG2E

    zero = jnp.zeros((BLOCK, LANES), BF16)

    def key_range(n):
        lo_blk, hi_blk = max(n - 1, 0), min(n + 2, n_blocks)
        return pl.ds(lo_blk * BLOCK, (hi_blk - lo_blk) * BLOCK), (lo_blk - (n - 1)) * BLOCK, \
            (hi_blk - lo_blk) * BLOCK

    units = [(n, par) for n in range(n_blocks) for par in range(2)]
    live = [dict() for _ in units]

    def stage_scores(u):
        n, par = units[u]
        keys, c0, width = key_range(n)
        qblk = q_ref[0, 0, pl.ds(n * BLOCK, BLOCK), :]
        qa, qb = qblk[:, :LANES], qblk[:, LANES:]
        if par == 0:
            qs = jnp.concatenate([jnp.where(low, qa, zero), jnp.where(low, qb, zero)], axis=0)
            kmat = ks_ref[keys, :]
        else:
            qs = jnp.concatenate([jnp.where(low, zero, qa), jnp.where(low, zero, qb)], axis=0)
            kmat = kss_ref[keys, :]
        live[u]["s"] = _dot_nt(kmat, qs) + bias_ref[par, c0:c0 + width, :]

    def stage_softmax(u):
        _, par = units[u]
        st = live[u]
        s = st.pop("s")
        sink = sink_ref[0, par, 0:1, :] * _LOG2E
        m = jnp.maximum(jnp.max(s, axis=0, keepdims=True), sink)
        st["e"] = jnp.exp2(s - m).astype(BF16)
        st["esink"] = jnp.exp2(sink - m)

    def stage_values(u):
        n, _ = units[u]
        keys, _, _ = key_range(n)
        live[u]["pv"] = _dot(vt_ref[:, keys], live[u].pop("e"))

    def stage_out(u):
        n, par = units[u]
        st = live[u]
        pv = st.pop("pv")
        st["out"] = pv[:HEAD_DIM] / (pv[HEAD_DIM:] + st.pop("esink"))
        if par == 1:
            even, odd = live[u - 1].pop("out"), st.pop("out")
            y_t = jnp.concatenate(
                [even[:, :BLOCK], odd[:, :BLOCK], even[:, BLOCK:], odd[:, BLOCK:]], axis=0)
            o_ref[0, pl.ds(n * BLOCK, BLOCK), :] = jnp.transpose(y_t).astype(BF16)

    stages = (stage_scores, stage_softmax, stage_values, stage_out)
    for i in range(len(units) + len(stages) - 1):
        for k, stage in enumerate(stages):
            if 0 <= i - k < len(units):
                stage(i - k)


def _attention(aq, akv, slope_cols, sink_cols):
    b, _, s, qw = aq.shape
    return pl.pallas_call(
        functools.partial(_attn_kernel, seq=s),
        grid=(b, ATTN_KV_HEADS),
        in_specs=[
            pl.BlockSpec((1, 1, s, qw), lambda i, j: (i, j, 0, 0)),
            pl.BlockSpec((1, 1, s, LANES), lambda i, j: (i, j, 0, 0)),
            pl.BlockSpec((1, 2, _SUBLANES, _STACK), lambda i, j: (j, 0, 0, 0)),
            pl.BlockSpec((1, 2, _SUBLANES, _STACK), lambda i, j: (j, 0, 0, 0)),
        ],
        out_specs=pl.BlockSpec((1, s, qw), lambda i, j: (i, 0, j)),
        out_shape=jax.ShapeDtypeStruct((b, s, ATTN_WIDTH), BF16),
        scratch_shapes=[
            pltpu.VMEM((s, LANES), BF16),
            pltpu.VMEM((s, LANES), BF16),
            pltpu.VMEM((LANES, s), BF16),
            pltpu.VMEM((2, _KEYS, _STACK), F32),
        ],
        compiler_params=pltpu.CompilerParams(
            dimension_semantics=("arbitrary", "arbitrary"), vmem_limit_bytes=VMEM_LIMIT_BYTES),
        name="attention",
    )(aq, akv, slope_cols, sink_cols)


_FFN_SUB_TILES = 2


def _layer_norm(z, gain, bias):
    mu = jnp.mean(z, axis=-1, keepdims=True)
    dlt = z - mu
    var = jnp.mean(dlt * dlt, axis=-1, keepdims=True)
    return dlt * lax.rsqrt(var + LN_EPS) * gain + bias


def _ffn_kernel(x_ref, yr_ref, ya_ref, p_ref, wo_ref, g1_ref, b1_ref, wg_ref, wu_ref, wd_ref,
                wpe_ref, wpg_ref, g2_ref, b2_ref, o_ref, *, alpha):
    tm = x_ref.shape[0]
    rows = [pl.ds(k * (tm // _FFN_SUB_TILES), tm // _FFN_SUB_TILES) for k in range(_FFN_SUB_TILES)]
    mix = [_dot(yr_ref[r, :], wo_ref[0:RET_WIDTH, :]) + _dot(ya_ref[r, :], wo_ref[RET_WIDTH:, :])
           for r in rows]
    h = [_layer_norm(alpha * x_ref[r, :] + m, g1_ref[...], b1_ref[...]) for r, m in zip(rows, mix)]
    z = []
    for r, hk in zip(rows, h):
        hb = hk.astype(BF16)
        g = _dot(hb, wg_ref[...])
        up = _dot(hb, wu_ref[...])
        act = (g * jax.nn.sigmoid(g) * up).astype(BF16)
        ffn = _dot(act, wd_ref[...])
        ple = _dot(p_ref[r, :].astype(BF16), wpe_ref[...])
        ple_gate = jax.nn.sigmoid(_dot(hb, wpg_ref[...]))
        z.append(alpha * hk + ffn + ple_gate * ple)
    for r, zk in zip(rows, z):
        o_ref[r, :] = _layer_norm(zk, g2_ref[...], b2_ref[...])


def _ffn(x, yr, ya, p, wo, g1, b1, wg, wu, wd, wpe, wpg, g2, b2, *, alpha, tm):
    t, d = x.shape

    def rows(width):
        return pl.BlockSpec((tm, width), lambda i: (i, 0))

    def resident(arr):
        return pl.BlockSpec(arr.shape, lambda i: (0, 0), pipeline_mode=pl.Buffered(1))

    return pl.pallas_call(
        functools.partial(_ffn_kernel, alpha=alpha),
        grid=(t // tm,),
        in_specs=[rows(d), rows(RET_WIDTH), rows(ATTN_WIDTH), rows(PLE_DIM),
                  resident(wo), resident(g1), resident(b1), resident(wg), resident(wu),
                  resident(wd), resident(wpe), resident(wpg), resident(g2), resident(b2)],
        out_specs=rows(d),
        out_shape=jax.ShapeDtypeStruct((t, d), F32),
        compiler_params=pltpu.CompilerParams(
            dimension_semantics=("arbitrary",), vmem_limit_bytes=VMEM_LIMIT_BYTES),
        name="ffn",
    )(x, yr, ya, p, wo, g1, b1, wg, wu, wd, wpe, wpg, g2, b2)


def _arrange_w_in(w):
    kv0 = 4 * RET_WIDTH + ATTN_WIDTH
    ak, av = w[:, kv0:kv0 + KV_WIDTH], w[:, kv0 + KV_WIDTH:]
    kv_cols = []
    for h in range(ATTN_KV_HEADS):
        kv_cols += [ak[:, h * HEAD_DIM:(h + 1) * HEAD_DIM], av[:, h * HEAD_DIM:(h + 1) * HEAD_DIM]]
    return jnp.concatenate([w[:, :kv0]] + kv_cols, axis=1).astype(BF16)


def _stacked_head_cols(per_head):
    v = per_head.astype(F32).reshape(ATTN_KV_HEADS, 2, 2)
    v = jnp.transpose(v, (0, 2, 1))
    v = jnp.repeat(v, BLOCK, axis=2)
    return jnp.broadcast_to(v[:, :, None, :], v.shape[:2] + (_SUBLANES, _STACK))


def kernel(x, p, w_in, ret_decay_fwd, ret_decay_bwd, ret_gn_gain, attn_sink, w_out, ln1_gain, ln1_bias,
           w_ffn_gate, w_ffn_up, w_ffn_down, w_ple_proj, w_ple_gate, ln2_gain, ln2_bias):
    depth = w_in.shape[0]
    b, s, d = x.shape
    alpha = (2.0 * depth) ** 0.25
    slopes = 2.0 ** (-8.0 * (jnp.arange(ATTN_HEADS, dtype=F32) + 1.0) / ATTN_HEADS)
    slope_cols = _stacked_head_cols(slopes)

    def lane_rep(v):
        v = v.astype(F32).reshape(RET_PAIRS, HEADS_PER_LANE_BLOCK, 1)
        return jnp.broadcast_to(v, (RET_PAIRS, HEADS_PER_LANE_BLOCK, LANES))

    row = lambda v: v.astype(F32).reshape(1, -1)
    h = x.astype(F32)
    for i in range(depth):
        ret, aq, akv = _in_proj(h, _arrange_w_in(w_in[i]), tm=512)
        y_ret = _retention(ret, lane_rep(ret_decay_fwd[i]), lane_rep(ret_decay_bwd[i]),
                           ret_gn_gain[i].astype(F32).reshape(RET_PAIRS, 1, LANES))
        y_att = _attention(aq, akv, slope_cols, _stacked_head_cols(attn_sink[i]))
        h = _ffn(h.reshape(b * s, d), y_ret.reshape(b * s, RET_WIDTH), y_att.reshape(b * s, ATTN_WIDTH),
                 p[i].reshape(b * s, PLE_DIM),
                 w_out[i].astype(BF16), row(ln1_gain[i]), row(ln1_bias[i]),
                 w_ffn_gate[i].astype(BF16), w_ffn_up[i].astype(BF16), w_ffn_down[i].astype(BF16),
                 w_ple_proj[i].astype(BF16), w_ple_gate[i].astype(BF16),
                 row(ln2_gain[i]), row(ln2_bias[i]), alpha=alpha, tm=512).reshape(b, s, d)
    return h.astype(x.dtype)
```

```python
import functools

import jax
import jax.numpy as jnp
from jax import lax
from jax.experimental import pallas as pl
from jax.experimental.pallas import tpu as pltpu

D_MODEL = 1024
HEAD_DIM = 64
RET_HEADS = 8
ATTN_HEADS = 8
ATTN_KV_HEADS = 2
ATTN_GROUP = ATTN_HEADS // ATTN_KV_HEADS
RET_WIDTH = RET_HEADS * HEAD_DIM
ATTN_WIDTH = ATTN_HEADS * HEAD_DIM
KV_WIDTH = ATTN_KV_HEADS * HEAD_DIM
IN_PROJ_WIDTH = 4 * RET_WIDTH + ATTN_WIDTH + 2 * KV_WIDTH
CHUNK = 128
WINDOW = 128
BLOCK = 128
PLE_DIM = 256
FFN_HIDDEN = ((8 * D_MODEL + 767) // 768) * 256
LN_EPS = 1e-5
GN_EPS = 1e-5
NEG_INF = -1e30
QK_SCALE = HEAD_DIM ** -0.5

LANES = 128
HEADS_PER_LANE_BLOCK = LANES // HEAD_DIM
RET_PAIRS = RET_HEADS // HEADS_PER_LANE_BLOCK
VMEM_LIMIT_BYTES = 56 * 1024 * 1024

F32 = jnp.float32
BF16 = jnp.bfloat16


def _dot(a, b):
    return jnp.dot(a, b, preferred_element_type=F32)


def _dot_nt(a, b):
    return lax.dot_general(a, b, (((1,), (1,)), ((), ())), preferred_element_type=F32)


def _dot_tn(a, b):
    return lax.dot_general(a, b, (((0,), (0,)), ((), ())), preferred_element_type=F32)


_IN_PROJ_COLS = 256
_RET_BLOCKS = 4 * RET_PAIRS
_KV_COL0 = 4 * RET_WIDTH + ATTN_WIDTH
_BF16_ROWS = 16


def _cast_rows_per_step(n_rows, n_steps):
    for rows in range(_BF16_ROWS, n_rows + 1, _BF16_ROWS):
        if n_rows % rows == 0 and n_rows // rows <= n_steps:
            return rows
    raise ValueError(f"no bf16-tile row split of {n_rows} rows over {n_steps} steps")


def _in_proj_kernel(x_ref, w_ref, *refs, n_cast):
    cast_in, (ret_ref, aq_ref, akv_ref) = refs[:n_cast], refs[n_cast:n_cast + 3]
    cast_out, wb_ref = refs[n_cast + 3:2 * n_cast + 3], refs[2 * n_cast + 3]

    @pl.when((pl.program_id(0) == 0) & (pl.program_id(1) == 0))
    def _():
        wb_ref[:, 0:_KV_COL0] = w_ref[:, 0:_KV_COL0].astype(BF16)
        low = lax.broadcasted_iota(jnp.int32, (1, LANES), 1) < HEAD_DIM
        ak = w_ref[:, _KV_COL0:_KV_COL0 + KV_WIDTH]
        av = w_ref[:, _KV_COL0 + KV_WIDTH:_KV_COL0 + 2 * KV_WIDTH]
        wb_ref[:, _KV_COL0:_KV_COL0 + LANES] = jnp.where(
            low, ak, pltpu.roll(av, HEAD_DIM, 1)).astype(BF16)
        wb_ref[:, _KV_COL0 + LANES:_KV_COL0 + 2 * LANES] = jnp.where(
            low, pltpu.roll(ak, HEAD_DIM, 1), av).astype(BF16)

    for src, dst in zip(cast_in, cast_out):
        dst[...] = src[...].astype(BF16)

    xb = x_ref[0].astype(BF16)
    for c in range(IN_PROJ_WIDTH // _IN_PROJ_COLS):
        u = _dot(xb, wb_ref[:, c * _IN_PROJ_COLS:(c + 1) * _IN_PROJ_COLS]).astype(BF16)
        if 2 * c < _RET_BLOCKS:
            ret_ref[0, 2 * c] = u[:, :LANES]
            ret_ref[0, 2 * c + 1] = u[:, LANES:]
        elif c < _RET_BLOCKS // 2 + ATTN_KV_HEADS:
            aq_ref[0, c - _RET_BLOCKS // 2] = u
        else:
            akv_ref[0, 0] = u[:, :LANES]
            akv_ref[0, 1] = u[:, LANES:]


def _in_proj(x, w_in, layer, cast_weights, tm):
    b, s, d = x.shape
    n_j = s // tm
    n_steps = b * n_j

    def cast_spec(w):
        rows = _cast_rows_per_step(w.shape[1], n_steps)
        last = w.shape[1] // rows - 1
        return rows, lambda i, j: (layer, jnp.minimum(i * n_j + j, last), 0)

    cast_in_specs, cast_out_specs, cast_out_shapes = [], [], []
    for w in cast_weights:
        rows, imap = cast_spec(w)
        cast_in_specs.append(pl.BlockSpec((None, rows, w.shape[2]), imap))
        cast_out_specs.append(pl.BlockSpec((rows, w.shape[2]), lambda i, j, imap=imap: imap(i, j)[1:]))
        cast_out_shapes.append(jax.ShapeDtypeStruct(w.shape[1:], BF16))

    outs = pl.pallas_call(
        functools.partial(_in_proj_kernel, n_cast=len(cast_weights)),
        grid=(b, n_j),
        in_specs=[
            pl.BlockSpec((1, tm, d), lambda i, j: (i, j, 0)),
            pl.BlockSpec((None, d, IN_PROJ_WIDTH), lambda i, j: (layer, 0, 0),
                         pipeline_mode=pl.Buffered(1)),
        ] + cast_in_specs,
        out_specs=[
            pl.BlockSpec((1, _RET_BLOCKS, tm, LANES), lambda i, j: (i, 0, j, 0)),
            pl.BlockSpec((1, ATTN_KV_HEADS, tm, ATTN_GROUP * HEAD_DIM), lambda i, j: (i, 0, j, 0)),
            pl.BlockSpec((1, ATTN_KV_HEADS, tm, LANES), lambda i, j: (i, 0, j, 0)),
        ] + cast_out_specs,
        out_shape=[
            jax.ShapeDtypeStruct((b, _RET_BLOCKS, s, LANES), BF16),
            jax.ShapeDtypeStruct((b, ATTN_KV_HEADS, s, ATTN_GROUP * HEAD_DIM), BF16),
            jax.ShapeDtypeStruct((b, ATTN_KV_HEADS, s, LANES), BF16),
        ] + cast_out_shapes,
        scratch_shapes=[pltpu.VMEM((d, IN_PROJ_WIDTH), BF16)],
        compiler_params=pltpu.CompilerParams(
            dimension_semantics=("arbitrary", "arbitrary"), vmem_limit_bytes=VMEM_LIMIT_BYTES),
        name="in_proj",
    )(x, w_in, *cast_weights)
    return outs[:3], outs[3:]


def _split_hi_lo(z):
    hi = z.astype(BF16)
    lo = (z - hi.astype(F32)).astype(BF16)
    return hi, lo


def _ret_kernel(q_ref, k_ref, v_ref, g_ref, decf_ref, decb_ref, gain_ref, o_ref, kv_ref, st_ref,
                p_ref, y_ref, *, seq):
    n_chunks = seq // CHUNK
    lane = lax.broadcasted_iota(jnp.int32, (1, LANES), 1)
    head0 = lane < HEAD_DIM

    def log_gamma(dec):
        return jnp.log1p(-jnp.exp2(dec))

    lgf0, lgf1 = log_gamma(decf_ref[0, 0:1, :]), log_gamma(decf_ref[0, 1:2, :])
    lgb0, lgb1 = log_gamma(decb_ref[0, 0:1, :]), log_gamma(decb_ref[0, 1:2, :])
    lgf = jnp.where(head0, lgf0, lgf1)
    lgb = jnp.where(head0, lgb0, lgb1)

    idx = lax.broadcasted_iota(jnp.int32, (CHUNK, LANES), 0).astype(F32)
    qdf = jnp.exp(lgf * (idx + 1.0))
    kdf = jnp.exp(lgf * (CHUNK - 1.0 - idx))
    qdb = jnp.exp(lgb * (CHUNK - idx))
    kdb = jnp.exp(lgb * idx)
    cdf = jnp.exp(lgf * CHUNK)
    cdb = jnp.exp(lgb * CHUNK)

    ri = lax.broadcasted_iota(jnp.int32, (CHUNK, CHUNK), 0)
    ci = lax.broadcasted_iota(jnp.int32, (CHUNK, CHUNK), 1)
    diff = (ri - ci).astype(F32)

    def intra_mask(lf, lb):
        fwd = jnp.exp(lf * jnp.maximum(diff, 0.0))
        bwd = jnp.exp(lb * jnp.maximum(-diff, 0.0))
        return jnp.where(diff > 0, fwd, jnp.where(diff < 0, bwd, fwd + bwd))

    mask2 = jnp.concatenate([intra_mask(lgf0, lgb0), intra_mask(lgf1, lgb1)], axis=0)
    same_head = (ri < HEAD_DIM) == (ci < HEAD_DIM)
    gn_avg = jnp.where(same_head, 1.0 / HEAD_DIM, 0.0).astype(BF16)

    for n in range(n_chunks):
        sl = pl.ds(n * CHUNK, CHUNK)
        kc = k_ref[0, 0, sl, :].astype(F32) * QK_SCALE
        kd = jnp.concatenate([(kc * kdf).astype(BF16), (kc * kdb).astype(BF16)], axis=1)
        kv = _dot_tn(kd, v_ref[0, 0, sl, :])
        kv_ref[n, 0:LANES, :] = jnp.where(same_head, kv[0:LANES], 0.0)
        kv_ref[n, LANES:2 * LANES, :] = jnp.where(same_head, kv[LANES:2 * LANES], 0.0)

    state = jnp.zeros((LANES, LANES), F32)
    for n in range(n_chunks):
        st_ref[n, 0:LANES, :] = state.astype(BF16)
        state = state * cdf + kv_ref[n, 0:LANES, :]
    state = jnp.zeros((LANES, LANES), F32)
    for n in reversed(range(n_chunks)):
        st_ref[n, LANES:2 * LANES, :] = state.astype(BF16)
        state = state * cdb + kv_ref[n, LANES:2 * LANES, :]

    zero = jnp.zeros((CHUNK, LANES), BF16)
    for n in range(n_chunks):
        sl = pl.ds(n * CHUNK, CHUNK)
        qb = q_ref[0, 0, sl, :]
        kb = k_ref[0, 0, sl, :] * QK_SCALE
        qs = jnp.concatenate([jnp.where(head0, qb, zero), jnp.where(head0, zero, qb)], axis=0)
        p = (_dot_nt(qs, kb) * mask2).astype(BF16)
        p_ref[n, :, 0:CHUNK] = p[0:CHUNK]
        p_ref[n, :, CHUNK:2 * CHUNK] = p[CHUNK:2 * CHUNK]

    for n in range(n_chunks):
        sl = pl.ds(n * CHUNK, CHUNK)
        vb = v_ref[0, 0, sl, :]
        qc = q_ref[0, 0, sl, :].astype(F32)
        lhs = jnp.concatenate([p_ref[n], (qc * qdf).astype(BF16), (qc * qdb).astype(BF16)], axis=1)
        rhs = jnp.concatenate([jnp.where(head0, vb, zero), jnp.where(head0, zero, vb), st_ref[n]],
                              axis=0)
        y_ref[sl, :] = _dot(lhs, rhs)

    gn_avg2 = jnp.concatenate([gn_avg, gn_avg], axis=0)

    def group_mean(z):
        hi, lo = _split_hi_lo(z)
        return _dot(jnp.concatenate([hi, lo], axis=1), gn_avg2)

    for n in range(n_chunks):
        sl = pl.ds(n * CHUNK, CHUNK)
        y = y_ref[sl, :]
        y_ref[sl, :] = y - group_mean(y)

    gain = gain_ref[0]
    for n in range(n_chunks):
        sl = pl.ds(n * CHUNK, CHUNK)
        dlt = y_ref[sl, :]
        var = group_mean(dlt * dlt)
        gate = g_ref[0, 0, sl, :].astype(F32)
        o_ref[0, sl, :] = (dlt * lax.rsqrt(var + GN_EPS) * gain
                           * (gate * jax.nn.sigmoid(gate))).astype(BF16)


def _retention(ret, decf, decb, gain):
    b, _, s, _ = ret.shape
    n_chunks = s // CHUNK

    def part(kind):
        return pl.BlockSpec((1, 1, s, LANES), lambda i, j, kind=kind: (i, kind * RET_PAIRS + j, 0, 0))

    return pl.pallas_call(
        functools.partial(_ret_kernel, seq=s),
        grid=(b, RET_PAIRS),
        in_specs=[
            part(0), part(1), part(2), part(3),
            pl.BlockSpec((1, HEADS_PER_LANE_BLOCK, LANES), lambda i, j: (j, 0, 0)),
            pl.BlockSpec((1, HEADS_PER_LANE_BLOCK, LANES), lambda i, j: (j, 0, 0)),
            pl.BlockSpec((1, 1, LANES), lambda i, j: (j, 0, 0)),
        ],
        out_specs=pl.BlockSpec((1, s, LANES), lambda i, j: (i, 0, j)),
        out_shape=jax.ShapeDtypeStruct((b, s, RET_WIDTH), BF16),
        scratch_shapes=[
            pltpu.VMEM((n_chunks, 2 * LANES, LANES), F32),
            pltpu.VMEM((n_chunks, 2 * LANES, LANES), BF16),
            pltpu.VMEM((n_chunks, CHUNK, 2 * CHUNK), BF16),
            pltpu.VMEM((s, LANES), F32),
        ],
        compiler_params=pltpu.CompilerParams(
            dimension_semantics=("arbitrary", "arbitrary"), vmem_limit_bytes=VMEM_LIMIT_BYTES),
        name="retention",
    )(ret, ret, ret, ret, decf, decb, gain)


_STACK = 2 * BLOCK
_KEYS = 3 * BLOCK
_SUBLANES = 8
_LOG2E = 1.4426950408889634


def _attn_kernel(q_ref, kv_ref, slope_ref, sink_ref, o_ref, ks_ref, kss_ref, vt_ref, bias_ref,
                 *, seq):
    n_blocks = seq // BLOCK
    lane = lax.broadcasted_iota(jnp.int32, (1, LANES), 1)
    low = lane < HEAD_DIM

    kv = kv_ref[0, 0].astype(F32)
    ks = kv * (QK_SCALE * _LOG2E)
    ks_ref[...] = ks.astype(BF16)
    kss_ref[...] = pltpu.roll(ks, HEAD_DIM, 1).astype(BF16)
    kv_t = jnp.transpose(kv)
    vt_ref[...] = jnp.concatenate(
        [kv_t[HEAD_DIM:], jnp.ones((HEAD_DIM, seq), F32)], axis=0).astype(BF16)

    kj = lax.broadcasted_iota(jnp.int32, (_KEYS, _STACK), 0)
    c = lax.broadcasted_iota(jnp.int32, (_KEYS, _STACK), 1)
    qi = jnp.where(c < BLOCK, c, c - BLOCK)
    dist = jnp.abs(kj - BLOCK - qi)
    for par in range(2):
        slope = slope_ref[0, par, 0:1, :]
        bias_ref[par] = jnp.where(dist <= WINDOW, -slope * dist.astype(F32), NEG_INF) * _LOG2E

    zero = jnp.zeros((BLOCK, LANES), BF16)

    def key_range(n):
        lo_blk, hi_blk = max(n - 1, 0), min(n + 2, n_blocks)
        return pl.ds(lo_blk * BLOCK, (hi_blk - lo_blk) * BLOCK), (lo_blk - (n - 1)) * BLOCK, \
            (hi_blk - lo_blk) * BLOCK

    units = [(n, par) for n in range(n_blocks) for par in range(2)]
    live = [dict() for _ in units]

    def stage_scores(u):
        n, par = units[u]
        keys, c0, width = key_range(n)
        qblk = q_ref[0, 0, pl.ds(n * BLOCK, BLOCK), :]
        qa, qb = qblk[:, :LANES], qblk[:, LANES:]
        if par == 0:
            qs = jnp.concatenate([jnp.where(low, qa, zero), jnp.where(low, qb, zero)], axis=0)
            kmat = ks_ref[keys, :]
        else:
            qs = jnp.concatenate([jnp.where(low, zero, qa), jnp.where(low, zero, qb)], axis=0)
            kmat = kss_ref[keys, :]
        live[u]["s"] = _dot_nt(kmat, qs) + bias_ref[par, c0:c0 + width, :]

    def stage_softmax(u):
        _, par = units[u]
        st = live[u]
        s = st.pop("s")
        sink = sink_ref[0, par, 0:1, :] * _LOG2E
        m = jnp.maximum(jnp.max(s, axis=0, keepdims=True), sink)
        st["e"] = jnp.exp2(s - m).astype(BF16)
        st["esink"] = jnp.exp2(sink - m)

    def stage_values(u):
        n, _ = units[u]
        keys, _, _ = key_range(n)
        live[u]["pv"] = _dot(vt_ref[:, keys], live[u].pop("e"))

    def stage_out(u):
        n, par = units[u]
        st = live[u]
        pv = st.pop("pv")
        st["out"] = pv[:HEAD_DIM] / (pv[HEAD_DIM:] + st.pop("esink"))
        if par == 1:
            even, odd = live[u - 1].pop("out"), st.pop("out")
            y_t = jnp.concatenate(
                [even[:, :BLOCK], odd[:, :BLOCK], even[:, BLOCK:], odd[:, BLOCK:]], axis=0)
            o_ref[0, pl.ds(n * BLOCK, BLOCK), :] = jnp.transpose(y_t).astype(BF16)

    stages = (stage_scores, stage_softmax, stage_values, stage_out)
    for i in range(len(units) + len(stages) - 1):
        for k, stage in enumerate(stages):
            if 0 <= i - k < len(units):
                stage(i - k)


def _attention(aq, akv, slope_cols, sink_cols):
    b, _, s, qw = aq.shape
    return pl.pallas_call(
        functools.partial(_attn_kernel, seq=s),
        grid=(b, ATTN_KV_HEADS),
        in_specs=[
            pl.BlockSpec((1, 1, s, qw), lambda i, j: (i, j, 0, 0)),
            pl.BlockSpec((1, 1, s, LANES), lambda i, j: (i, j, 0, 0)),
            pl.BlockSpec((1, 2, _SUBLANES, _STACK), lambda i, j: (j, 0, 0, 0)),
            pl.BlockSpec((1, 2, _SUBLANES, _STACK), lambda i, j: (j, 0, 0, 0)),
        ],
        out_specs=pl.BlockSpec((1, s, qw), lambda i, j: (i, 0, j)),
        out_shape=jax.ShapeDtypeStruct((b, s, ATTN_WIDTH), BF16),
        scratch_shapes=[
            pltpu.VMEM((s, LANES), BF16),
            pltpu.VMEM((s, LANES), BF16),
            pltpu.VMEM((LANES, s), BF16),
            pltpu.VMEM((2, _KEYS, _STACK), F32),
        ],
        compiler_params=pltpu.CompilerParams(
            dimension_semantics=("arbitrary", "arbitrary"), vmem_limit_bytes=VMEM_LIMIT_BYTES),
        name="attention",
    )(aq, akv, slope_cols, sink_cols)


_FFN_SUB_TILES = 2


def _layer_norm(z, gain, bias):
    mu = jnp.mean(z, axis=-1, keepdims=True)
    dlt = z - mu
    var = jnp.mean(dlt * dlt, axis=-1, keepdims=True)
    return dlt * lax.rsqrt(var + LN_EPS) * gain + bias


def _ffn_kernel(x_ref, yr_ref, ya_ref, p_ref, wo_ref, g1_ref, b1_ref, wg_ref, wu_ref, wd_ref,
                wpe_ref, wpg_ref, g2_ref, b2_ref, o_ref, *, alpha):
    tm = x_ref.shape[0]
    rows = [pl.ds(k * (tm // _FFN_SUB_TILES), tm // _FFN_SUB_TILES) for k in range(_FFN_SUB_TILES)]
    mix = [_dot(yr_ref[r, :], wo_ref[0:RET_WIDTH, :]) + _dot(ya_ref[r, :], wo_ref[RET_WIDTH:, :])
           for r in rows]
    h = [_layer_norm(alpha * x_ref[r, :] + m, g1_ref[...], b1_ref[...]) for r, m in zip(rows, mix)]
    z = []
    for r, hk in zip(rows, h):
        hb = hk.astype(BF16)
        g = _dot(hb, wg_ref[...])
        up = _dot(hb, wu_ref[...])
        act = (g * jax.nn.sigmoid(g) * up).astype(BF16)
        ffn = _dot(act, wd_ref[...])
        ple = _dot(p_ref[r, :].astype(BF16), wpe_ref[...])
        ple_gate = jax.nn.sigmoid(_dot(hb, wpg_ref[...]))
        z.append(alpha * hk + ffn + ple_gate * ple)
    for r, zk in zip(rows, z):
        o_ref[r, :] = _layer_norm(zk, g2_ref[...], b2_ref[...])


def _ffn(x, yr, ya, p, wo, g1, b1, wg, wu, wd, wpe, wpg, g2, b2, *, alpha, tm):
    t, d = x.shape

    def rows(width):
        return pl.BlockSpec((tm, width), lambda i: (i, 0))

    def resident(arr):
        return pl.BlockSpec(arr.shape, lambda i: (0, 0), pipeline_mode=pl.Buffered(1))

    return pl.pallas_call(
        functools.partial(_ffn_kernel, alpha=alpha),
        grid=(t // tm,),
        in_specs=[rows(d), rows(RET_WIDTH), rows(ATTN_WIDTH), rows(PLE_DIM),
                  resident(wo), resident(g1), resident(b1), resident(wg), resident(wu),
                  resident(wd), resident(wpe), resident(wpg), resident(g2), resident(b2)],
        out_specs=rows(d),
        out_shape=jax.ShapeDtypeStruct((t, d), F32),
        compiler_params=pltpu.CompilerParams(
            dimension_semantics=("arbitrary",), vmem_limit_bytes=VMEM_LIMIT_BYTES),
        name="ffn",
    )(x, yr, ya, p, wo, g1, b1, wg, wu, wd, wpe, wpg, g2, b2)


def _stacked_head_cols(per_head):
    v = per_head.astype(F32).reshape(ATTN_KV_HEADS, 2, 2)
    v = jnp.transpose(v, (0, 2, 1))
    v = jnp.repeat(v, BLOCK, axis=2)
    return jnp.broadcast_to(v[:, :, None, :], v.shape[:2] + (_SUBLANES, _STACK))


def kernel(x, p, w_in, ret_decay_fwd, ret_decay_bwd, ret_gn_gain, attn_sink, w_out, ln1_gain, ln1_bias,
           w_ffn_gate, w_ffn_up, w_ffn_down, w_ple_proj, w_ple_gate, ln2_gain, ln2_bias):
    depth = w_in.shape[0]
    b, s, d = x.shape
    alpha = (2.0 * depth) ** 0.25
    slopes = 2.0 ** (-8.0 * (jnp.arange(ATTN_HEADS, dtype=F32) + 1.0) / ATTN_HEADS)
    slope_cols = _stacked_head_cols(slopes)

    def lane_rep(v):
        v = v.astype(F32).reshape(RET_PAIRS, HEADS_PER_LANE_BLOCK, 1)
        return jnp.broadcast_to(v, (RET_PAIRS, HEADS_PER_LANE_BLOCK, LANES))

    row = lambda v: v.astype(F32).reshape(1, -1)
    h = x.astype(F32)
    for i in range(depth):
        (ret, aq, akv), (wo, wg, wu, wd, wpe, wpg) = _in_proj(
            h, w_in, i, (w_out, w_ffn_gate, w_ffn_up, w_ffn_down, w_ple_proj, w_ple_gate), tm=512)
        y_ret = _retention(ret, lane_rep(ret_decay_fwd[i]), lane_rep(ret_decay_bwd[i]),
                           ret_gn_gain[i].astype(F32).reshape(RET_PAIRS, 1, LANES))
        y_att = _attention(aq, akv, slope_cols, _stacked_head_cols(attn_sink[i]))
        h = _ffn(h.reshape(b * s, d), y_ret.reshape(b * s, RET_WIDTH), y_att.reshape(b * s, ATTN_WIDTH),
                 p[i].reshape(b * s, PLE_DIM),
                 wo, row(ln1_gain[i]), row(ln1_bias[i]), wg, wu, wd, wpe, wpg,
                 row(ln2_gain[i]), row(ln2_bias[i]), alpha=alpha, tm=512).reshape(b, s, d)
    return h.astype(x.dtype)
```

```python
import functools

import jax
import jax.numpy as jnp
from jax import lax
from jax.experimental import pallas as pl
from jax.experimental.pallas import tpu as pltpu

D_MODEL = 1024
HEAD_DIM = 64
RET_HEADS = 8
ATTN_HEADS = 8
ATTN_KV_HEADS = 2
ATTN_GROUP = ATTN_HEADS // ATTN_KV_HEADS
RET_WIDTH = RET_HEADS * HEAD_DIM
ATTN_WIDTH = ATTN_HEADS * HEAD_DIM
KV_WIDTH = ATTN_KV_HEADS * HEAD_DIM
IN_PROJ_WIDTH = 4 * RET_WIDTH + ATTN_WIDTH + 2 * KV_WIDTH
CHUNK = 128
WINDOW = 128
BLOCK = 128
PLE_DIM = 256
FFN_HIDDEN = ((8 * D_MODEL + 767) // 768) * 256
LN_EPS = 1e-5
GN_EPS = 1e-5
NEG_INF = -1e30
QK_SCALE = HEAD_DIM ** -0.5

LANES = 128
HEADS_PER_LANE_BLOCK = LANES // HEAD_DIM
RET_PAIRS = RET_HEADS // HEADS_PER_LANE_BLOCK
VMEM_LIMIT_BYTES = 56 * 1024 * 1024

F32 = jnp.float32
BF16 = jnp.bfloat16


def _dot(a, b):
    return jnp.dot(a, b, preferred_element_type=F32)


def _dot_nt(a, b):
    return lax.dot_general(a, b, (((1,), (1,)), ((), ())), preferred_element_type=F32)


def _dot_tn(a, b):
    return lax.dot_general(a, b, (((0,), (0,)), ((), ())), preferred_element_type=F32)


_IN_PROJ_COLS = 256
_RET_BLOCKS = 4 * RET_PAIRS
_KV_COL0 = 4 * RET_WIDTH + ATTN_WIDTH
_BF16_ROWS = 16


def _cast_rows_per_step(n_rows, n_steps):
    for rows in range(_BF16_ROWS, n_rows + 1, _BF16_ROWS):
        if n_rows % rows == 0 and n_rows // rows <= n_steps:
            return rows
    raise ValueError(f"no bf16-tile row split of {n_rows} rows over {n_steps} steps")


def _in_proj_kernel(x_ref, w_ref, *refs, n_cast):
    cast_in, (ret_ref, aq_ref, akv_ref) = refs[:n_cast], refs[n_cast:n_cast + 3]
    cast_out, wb_ref = refs[n_cast + 3:2 * n_cast + 3], refs[2 * n_cast + 3]

    @pl.when((pl.program_id(0) == 0) & (pl.program_id(1) == 0))
    def _():
        wb_ref[:, 0:_KV_COL0] = w_ref[:, 0:_KV_COL0].astype(BF16)
        low = lax.broadcasted_iota(jnp.int32, (1, LANES), 1) < HEAD_DIM
        ak = w_ref[:, _KV_COL0:_KV_COL0 + KV_WIDTH]
        av = w_ref[:, _KV_COL0 + KV_WIDTH:_KV_COL0 + 2 * KV_WIDTH]
        wb_ref[:, _KV_COL0:_KV_COL0 + LANES] = jnp.where(
            low, ak, pltpu.roll(av, HEAD_DIM, 1)).astype(BF16)
        wb_ref[:, _KV_COL0 + LANES:_KV_COL0 + 2 * LANES] = jnp.where(
            low, pltpu.roll(ak, HEAD_DIM, 1), av).astype(BF16)

    for src, dst in zip(cast_in, cast_out):
        dst[...] = src[...].astype(BF16)

    xb = x_ref[0].astype(BF16)
    for c in range(IN_PROJ_WIDTH // _IN_PROJ_COLS):
        u = _dot(xb, wb_ref[:, c * _IN_PROJ_COLS:(c + 1) * _IN_PROJ_COLS]).astype(BF16)
        if 2 * c < _RET_BLOCKS:
            ret_ref[0, 2 * c] = u[:, :LANES]
            ret_ref[0, 2 * c + 1] = u[:, LANES:]
        elif c < _RET_BLOCKS // 2 + ATTN_KV_HEADS:
            aq_ref[0, c - _RET_BLOCKS // 2] = u
        else:
            akv_ref[0, 0] = u[:, :LANES]
            akv_ref[0, 1] = u[:, LANES:]


def _in_proj(x, w_in, layer, cast_weights, tm):
    b, s, d = x.shape
    n_j = s // tm
    n_steps = b * n_j

    def cast_spec(w):
        rows = _cast_rows_per_step(w.shape[1], n_steps)
        last = w.shape[1] // rows - 1
        return rows, lambda i, j: (layer, jnp.minimum(i * n_j + j, last), 0)

    cast_in_specs, cast_out_specs, cast_out_shapes = [], [], []
    for w in cast_weights:
        rows, imap = cast_spec(w)
        cast_in_specs.append(pl.BlockSpec((None, rows, w.shape[2]), imap))
        cast_out_specs.append(pl.BlockSpec((rows, w.shape[2]), lambda i, j, imap=imap: imap(i, j)[1:]))
        cast_out_shapes.append(jax.ShapeDtypeStruct(w.shape[1:], BF16))

    outs = pl.pallas_call(
        functools.partial(_in_proj_kernel, n_cast=len(cast_weights)),
        grid=(b, n_j),
        in_specs=[
            pl.BlockSpec((1, tm, d), lambda i, j: (i, j, 0)),
            pl.BlockSpec((None, d, IN_PROJ_WIDTH), lambda i, j: (layer, 0, 0),
                         pipeline_mode=pl.Buffered(1)),
        ] + cast_in_specs,
        out_specs=[
            pl.BlockSpec((1, _RET_BLOCKS, tm, LANES), lambda i, j: (i, 0, j, 0)),
            pl.BlockSpec((1, ATTN_KV_HEADS, tm, ATTN_GROUP * HEAD_DIM), lambda i, j: (i, 0, j, 0)),
            pl.BlockSpec((1, ATTN_KV_HEADS, tm, LANES), lambda i, j: (i, 0, j, 0)),
        ] + cast_out_specs,
        out_shape=[
            jax.ShapeDtypeStruct((b, _RET_BLOCKS, s, LANES), BF16),
            jax.ShapeDtypeStruct((b, ATTN_KV_HEADS, s, ATTN_GROUP * HEAD_DIM), BF16),
            jax.ShapeDtypeStruct((b, ATTN_KV_HEADS, s, LANES), BF16),
        ] + cast_out_shapes,
        scratch_shapes=[pltpu.VMEM((d, IN_PROJ_WIDTH), BF16)],
        compiler_params=pltpu.CompilerParams(
            dimension_semantics=("arbitrary", "arbitrary"), vmem_limit_bytes=VMEM_LIMIT_BYTES),
        name="in_proj",
    )(x, w_in, *cast_weights)
    return outs[:3], outs[3:]


def _ret_kernel(q_ref, k_ref, v_ref, g_ref, decf_ref, decb_ref, gain_ref, o_ref, kv_ref, st_ref,
                p_ref, y_ref, *, seq):
    n_chunks = seq // CHUNK
    lane = lax.broadcasted_iota(jnp.int32, (1, LANES), 1)
    head0 = lane < HEAD_DIM

    def log_gamma(dec):
        return jnp.log1p(-jnp.exp2(dec))

    lgf0, lgf1 = log_gamma(decf_ref[0, 0:1, :]), log_gamma(decf_ref[0, 1:2, :])
    lgb0, lgb1 = log_gamma(decb_ref[0, 0:1, :]), log_gamma(decb_ref[0, 1:2, :])
    lgf = jnp.where(head0, lgf0, lgf1)
    lgb = jnp.where(head0, lgb0, lgb1)

    idx = lax.broadcasted_iota(jnp.int32, (CHUNK, LANES), 0).astype(F32)
    qdf = jnp.exp(lgf * (idx + 1.0))
    kdf = jnp.exp(lgf * (CHUNK - 1.0 - idx)) * QK_SCALE
    qdb = jnp.exp(lgb * (CHUNK - idx))
    kdb = jnp.exp(lgb * idx) * QK_SCALE
    cdf = jnp.exp(lgf * CHUNK)
    cdb = jnp.exp(lgb * CHUNK)

    ri = lax.broadcasted_iota(jnp.int32, (CHUNK, CHUNK), 0)
    ci = lax.broadcasted_iota(jnp.int32, (CHUNK, CHUNK), 1)
    diff = (ri - ci).astype(F32)

    def intra_mask(lf, lb):
        fwd = jnp.exp(lf * jnp.maximum(diff, 0.0))
        bwd = jnp.exp(lb * jnp.maximum(-diff, 0.0))
        return jnp.where(diff > 0, fwd, jnp.where(diff < 0, bwd, fwd + bwd))

    mask2 = jnp.concatenate([intra_mask(lgf0, lgb0), intra_mask(lgf1, lgb1)], axis=0) * QK_SCALE
    same_head = (ri < HEAD_DIM) == (ci < HEAD_DIM)
    gn_avg = jnp.where(same_head, 1.0 / HEAD_DIM, 0.0).astype(BF16)

    for n in range(n_chunks):
        sl = pl.ds(n * CHUNK, CHUNK)
        kc = k_ref[0, 0, sl, :].astype(F32)
        kd = jnp.concatenate([(kc * kdf).astype(BF16), (kc * kdb).astype(BF16)], axis=1)
        kv = _dot_tn(kd, v_ref[0, 0, sl, :])
        kv_ref[n] = kv

    zero_state = jnp.zeros((LANES, LANES), BF16)
    state = jnp.zeros((LANES, LANES), F32)
    for n in range(n_chunks):
        st_ref[n, 0:LANES, :] = jnp.where(same_head, state.astype(BF16), zero_state)
        state = state * cdf + kv_ref[n, 0:LANES, :]
    state = jnp.zeros((LANES, LANES), F32)
    for n in reversed(range(n_chunks)):
        st_ref[n, LANES:2 * LANES, :] = jnp.where(same_head, state.astype(BF16), zero_state)
        state = state * cdb + kv_ref[n, LANES:2 * LANES, :]

    zero = jnp.zeros((CHUNK, LANES), BF16)
    for n in range(n_chunks):
        sl = pl.ds(n * CHUNK, CHUNK)
        qb = q_ref[0, 0, sl, :]
        kb = k_ref[0, 0, sl, :]
        qs = jnp.concatenate([jnp.where(head0, qb, zero), jnp.where(head0, zero, qb)], axis=0)
        p = (_dot_nt(qs, kb) * mask2).astype(BF16)
        p_ref[n, :, 0:CHUNK] = p[0:CHUNK]
        p_ref[n, :, CHUNK:2 * CHUNK] = p[CHUNK:2 * CHUNK]

    for n in range(n_chunks):
        sl = pl.ds(n * CHUNK, CHUNK)
        vb = v_ref[0, 0, sl, :]
        qc = q_ref[0, 0, sl, :].astype(F32)
        lhs = jnp.concatenate([p_ref[n], (qc * qdf).astype(BF16), (qc * qdb).astype(BF16)], axis=1)
        rhs = jnp.concatenate([jnp.where(head0, vb, zero), jnp.where(head0, zero, vb), st_ref[n]],
                              axis=0)
        y_ref[sl, :] = _dot(lhs, rhs)

    def group_mean(z):
        return _dot(z.astype(BF16), gn_avg)

    for n in range(n_chunks):
        sl = pl.ds(n * CHUNK, CHUNK)
        y = y_ref[sl, :]
        y_ref[sl, :] = y - group_mean(y)

    gain = gain_ref[0]
    for n in range(n_chunks):
        sl = pl.ds(n * CHUNK, CHUNK)
        dlt = y_ref[sl, :]
        var = group_mean(dlt * dlt)
        gate = g_ref[0, 0, sl, :].astype(F32)
        o_ref[0, sl, :] = (dlt * lax.rsqrt(var + GN_EPS) * gain
                           * (gate * jax.nn.sigmoid(gate))).astype(BF16)


def _retention(ret, decf, decb, gain):
    b, _, s, _ = ret.shape
    n_chunks = s // CHUNK

    def part(kind):
        return pl.BlockSpec((1, 1, s, LANES), lambda i, j, kind=kind: (i, kind * RET_PAIRS + j, 0, 0))

    return pl.pallas_call(
        functools.partial(_ret_kernel, seq=s),
        grid=(b, RET_PAIRS),
        in_specs=[
            part(0), part(1), part(2), part(3),
            pl.BlockSpec((1, HEADS_PER_LANE_BLOCK, LANES), lambda i, j: (j, 0, 0)),
            pl.BlockSpec((1, HEADS_PER_LANE_BLOCK, LANES), lambda i, j: (j, 0, 0)),
            pl.BlockSpec((1, 1, LANES), lambda i, j: (j, 0, 0)),
        ],
        out_specs=pl.BlockSpec((1, s, LANES), lambda i, j: (i, 0, j)),
        out_shape=jax.ShapeDtypeStruct((b, s, RET_WIDTH), BF16),
        scratch_shapes=[
            pltpu.VMEM((n_chunks, 2 * LANES, LANES), F32),
            pltpu.VMEM((n_chunks, 2 * LANES, LANES), BF16),
            pltpu.VMEM((n_chunks, CHUNK, 2 * CHUNK), BF16),
            pltpu.VMEM((s, LANES), F32),
        ],
        compiler_params=pltpu.CompilerParams(
            dimension_semantics=("arbitrary", "arbitrary"), vmem_limit_bytes=VMEM_LIMIT_BYTES),
        name="retention",
    )(ret, ret, ret, ret, decf, decb, gain)


_STACK = 2 * BLOCK
_KEYS = 3 * BLOCK
_SUBLANES = 8
_LOG2E = 1.4426950408889634
_ATTN_STAGE_LAG = (0, 2, 4, 5)


def _attn_kernel(q_ref, kv_ref, slope_ref, sink_ref, o_ref, ks_ref, kss_ref, vt_ref, bias_ref,
                 *, seq):
    n_blocks = seq // BLOCK
    lane = lax.broadcasted_iota(jnp.int32, (1, LANES), 1)
    low = lane < HEAD_DIM

    kv = kv_ref[0, 0].astype(F32)
    ks = kv * (QK_SCALE * _LOG2E)
    ks_ref[...] = ks.astype(BF16)
    kss_ref[...] = pltpu.roll(ks, HEAD_DIM, 1).astype(BF16)
    kv_t = jnp.transpose(kv)
    vt_ref[...] = jnp.concatenate(
        [kv_t[HEAD_DIM:], jnp.ones((HEAD_DIM, seq), F32)], axis=0).astype(BF16)

    kj = lax.broadcasted_iota(jnp.int32, (_KEYS, _STACK), 0)
    c = lax.broadcasted_iota(jnp.int32, (_KEYS, _STACK), 1)
    qi = jnp.where(c < BLOCK, c, c - BLOCK)
    dist = jnp.abs(kj - BLOCK - qi)
    for par in range(2):
        slope = slope_ref[0, par, 0:1, :]
        bias_ref[par] = jnp.where(dist <= WINDOW, -slope * dist.astype(F32), NEG_INF) * _LOG2E

    zero = jnp.zeros((BLOCK, LANES), BF16)

    def key_range(n):
        lo_blk, hi_blk = max(n - 1, 0), min(n + 2, n_blocks)
        return pl.ds(lo_blk * BLOCK, (hi_blk - lo_blk) * BLOCK), (lo_blk - (n - 1)) * BLOCK, \
            (hi_blk - lo_blk) * BLOCK

    units = [(n, par) for n in range(n_blocks) for par in range(2)]
    live = [dict() for _ in units]

    def stage_scores(u):
        n, par = units[u]
        keys, c0, width = key_range(n)
        qblk = q_ref[0, 0, pl.ds(n * BLOCK, BLOCK), :]
        qa, qb = qblk[:, :LANES], qblk[:, LANES:]
        if par == 0:
            qs = jnp.concatenate([jnp.where(low, qa, zero), jnp.where(low, qb, zero)], axis=0)
            kmat = ks_ref[keys, :]
        else:
            qs = jnp.concatenate([jnp.where(low, zero, qa), jnp.where(low, zero, qb)], axis=0)
            kmat = kss_ref[keys, :]
        live[u]["s"] = _dot_nt(kmat, qs) + bias_ref[par, c0:c0 + width, :]

    def stage_softmax(u):
        _, par = units[u]
        st = live[u]
        s = st.pop("s")
        sink = sink_ref[0, par, 0:1, :] * _LOG2E
        m = jnp.maximum(jnp.max(s, axis=0, keepdims=True), sink)
        st["e"] = jnp.exp2(s - m).astype(BF16)
        st["esink"] = jnp.exp2(sink - m)

    def stage_values(u):
        n, _ = units[u]
        keys, _, _ = key_range(n)
        live[u]["pv"] = _dot(vt_ref[:, keys], live[u].pop("e"))

    def stage_out(u):
        n, par = units[u]
        st = live[u]
        pv = st.pop("pv")
        st["out"] = pv[:HEAD_DIM] / (pv[HEAD_DIM:] + st.pop("esink"))
        if par == 1:
            even, odd = live[u - 1].pop("out"), st.pop("out")
            y_t = jnp.concatenate(
                [even[:, :BLOCK], odd[:, :BLOCK], even[:, BLOCK:], odd[:, BLOCK:]], axis=0)
            o_ref[0, pl.ds(n * BLOCK, BLOCK), :] = jnp.transpose(y_t).astype(BF16)

    stages = (stage_scores, stage_softmax, stage_values, stage_out)
    for i in range(len(units) + _ATTN_STAGE_LAG[-1]):
        for lag, stage in zip(_ATTN_STAGE_LAG, stages):
            if 0 <= i - lag < len(units):
                stage(i - lag)


def _attention(aq, akv, slope_cols, sink_cols):
    b, _, s, qw = aq.shape
    return pl.pallas_call(
        functools.partial(_attn_kernel, seq=s),
        grid=(b, ATTN_KV_HEADS),
        in_specs=[
            pl.BlockSpec((1, 1, s, qw), lambda i, j: (i, j, 0, 0)),
            pl.BlockSpec((1, 1, s, LANES), lambda i, j: (i, j, 0, 0)),
            pl.BlockSpec((1, 2, _SUBLANES, _STACK), lambda i, j: (j, 0, 0, 0)),
            pl.BlockSpec((1, 2, _SUBLANES, _STACK), lambda i, j: (j, 0, 0, 0)),
        ],
        out_specs=pl.BlockSpec((1, s, qw), lambda i, j: (i, 0, j)),
        out_shape=jax.ShapeDtypeStruct((b, s, ATTN_WIDTH), BF16),
        scratch_shapes=[
            pltpu.VMEM((s, LANES), BF16),
            pltpu.VMEM((s, LANES), BF16),
            pltpu.VMEM((LANES, s), BF16),
            pltpu.VMEM((2, _KEYS, _STACK), F32),
        ],
        compiler_params=pltpu.CompilerParams(
            dimension_semantics=("arbitrary", "arbitrary"), vmem_limit_bytes=VMEM_LIMIT_BYTES),
        name="attention",
    )(aq, akv, slope_cols, sink_cols)


_FFN_SUB_TILES = 2


def _layer_norm(z, gain, bias):
    mu = jnp.mean(z, axis=-1, keepdims=True)
    dlt = z - mu
    var = jnp.mean(dlt * dlt, axis=-1, keepdims=True)
    return dlt * lax.rsqrt(var + LN_EPS) * gain + bias


def _ffn_kernel(x_ref, yr_ref, ya_ref, p_ref, wo_ref, g1_ref, b1_ref, wg_ref, wu_ref, wd_ref,
                wpe_ref, wpg_ref, g2_ref, b2_ref, o_ref, *, alpha):
    tm = x_ref.shape[0]
    rows = [pl.ds(k * (tm // _FFN_SUB_TILES), tm // _FFN_SUB_TILES) for k in range(_FFN_SUB_TILES)]
    mix = [_dot(yr_ref[r, :], wo_ref[0:RET_WIDTH, :]) + _dot(ya_ref[r, :], wo_ref[RET_WIDTH:, :])
           for r in rows]
    h = [_layer_norm(alpha * x_ref[r, :] + m, g1_ref[...], b1_ref[...]) for r, m in zip(rows, mix)]
    z = []
    for r, hk in zip(rows, h):
        hb = hk.astype(BF16)
        g = _dot(hb, wg_ref[...])
        up = _dot(hb, wu_ref[...])
        act = (g * jax.nn.sigmoid(g) * up).astype(BF16)
        ffn = _dot(act, wd_ref[...])
        ple = _dot(p_ref[r, :].astype(BF16), wpe_ref[...])
        ple_gate = jax.nn.sigmoid(_dot(hb, wpg_ref[...]))
        z.append(alpha * hk + ffn + ple_gate * ple)
    for r, zk in zip(rows, z):
        o_ref[r, :] = _layer_norm(zk, g2_ref[...], b2_ref[...])


def _ffn(x, yr, ya, p, wo, g1, b1, wg, wu, wd, wpe, wpg, g2, b2, *, alpha, tm):
    t, d = x.shape

    def rows(width):
        return pl.BlockSpec((tm, width), lambda i: (i, 0))

    def resident(arr):
        return pl.BlockSpec(arr.shape, lambda i: (0, 0), pipeline_mode=pl.Buffered(1))

    return pl.pallas_call(
        functools.partial(_ffn_kernel, alpha=alpha),
        grid=(t // tm,),
        in_specs=[rows(d), rows(RET_WIDTH), rows(ATTN_WIDTH), rows(PLE_DIM),
                  resident(wo), resident(g1), resident(b1), resident(wg), resident(wu),
                  resident(wd), resident(wpe), resident(wpg), resident(g2), resident(b2)],
        out_specs=rows(d),
        out_shape=jax.ShapeDtypeStruct((t, d), F32),
        compiler_params=pltpu.CompilerParams(
            dimension_semantics=("arbitrary",), vmem_limit_bytes=VMEM_LIMIT_BYTES),
        name="ffn",
    )(x, yr, ya, p, wo, g1, b1, wg, wu, wd, wpe, wpg, g2, b2)


def _stacked_head_cols(per_head):
    v = per_head.astype(F32).reshape(ATTN_KV_HEADS, 2, 2)
    v = jnp.transpose(v, (0, 2, 1))
    v = jnp.repeat(v, BLOCK, axis=2)
    return jnp.broadcast_to(v[:, :, None, :], v.shape[:2] + (_SUBLANES, _STACK))


def kernel(x, p, w_in, ret_decay_fwd, ret_decay_bwd, ret_gn_gain, attn_sink, w_out, ln1_gain, ln1_bias,
           w_ffn_gate, w_ffn_up, w_ffn_down, w_ple_proj, w_ple_gate, ln2_gain, ln2_bias):
    depth = w_in.shape[0]
    b, s, d = x.shape
    alpha = (2.0 * depth) ** 0.25
    slopes = 2.0 ** (-8.0 * (jnp.arange(ATTN_HEADS, dtype=F32) + 1.0) / ATTN_HEADS)
    slope_cols = _stacked_head_cols(slopes)

    def lane_rep(v):
        v = v.astype(F32).reshape(RET_PAIRS, HEADS_PER_LANE_BLOCK, 1)
        return jnp.broadcast_to(v, (RET_PAIRS, HEADS_PER_LANE_BLOCK, LANES))

    row = lambda v: v.astype(F32).reshape(1, -1)
    h = x.astype(F32)
    for i in range(depth):
        (ret, aq, akv), (wo, wg, wu, wd, wpe, wpg) = _in_proj(
            h, w_in, i, (w_out, w_ffn_gate, w_ffn_up, w_ffn_down, w_ple_proj, w_ple_gate), tm=512)
        y_ret = _retention(ret, lane_rep(ret_decay_fwd[i]), lane_rep(ret_decay_bwd[i]),
                           ret_gn_gain[i].astype(F32).reshape(RET_PAIRS, 1, LANES))
        y_att = _attention(aq, akv, slope_cols, _stacked_head_cols(attn_sink[i]))
        h = _ffn(h.reshape(b * s, d), y_ret.reshape(b * s, RET_WIDTH), y_att.reshape(b * s, ATTN_WIDTH),
                 p[i].reshape(b * s, PLE_DIM),
                 wo, row(ln1_gain[i]), row(ln1_bias[i]), wg, wu, wd, wpe, wpg,
                 row(ln2_gain[i]), row(ln2_bias[i]), alpha=alpha, tm=512).reshape(b, s, d)
    return h.astype(x.dtype)
```

```python
import functools

import jax
import jax.numpy as jnp
from jax import lax
from jax.experimental import pallas as pl
from jax.experimental.pallas import tpu as pltpu

D_MODEL = 1024
HEAD_DIM = 64
RET_HEADS = 8
ATTN_HEADS = 8
ATTN_KV_HEADS = 2
ATTN_GROUP = ATTN_HEADS // ATTN_KV_HEADS
RET_WIDTH = RET_HEADS * HEAD_DIM
ATTN_WIDTH = ATTN_HEADS * HEAD_DIM
KV_WIDTH = ATTN_KV_HEADS * HEAD_DIM
IN_PROJ_WIDTH = 4 * RET_WIDTH + ATTN_WIDTH + 2 * KV_WIDTH
CHUNK = 128
WINDOW = 128
BLOCK = 128
PLE_DIM = 256
FFN_HIDDEN = ((8 * D_MODEL + 767) // 768) * 256
LN_EPS = 1e-5
GN_EPS = 1e-5
NEG_INF = -1e30
QK_SCALE = HEAD_DIM ** -0.5

LANES = 128
HEADS_PER_LANE_BLOCK = LANES // HEAD_DIM
RET_PAIRS = RET_HEADS // HEADS_PER_LANE_BLOCK
VMEM_LIMIT_BYTES = 56 * 1024 * 1024
BATCH_GROUPS = 4
SEQ_TILE = 512

F32 = jnp.float32
BF16 = jnp.bfloat16


def _dot(a, b):
    return jnp.dot(a, b, preferred_element_type=F32)


def _dot_nt(a, b):
    return lax.dot_general(a, b, (((1,), (1,)), ((), ())), preferred_element_type=F32)


def _dot_tn(a, b):
    return lax.dot_general(a, b, (((0,), (0,)), ((), ())), preferred_element_type=F32)


def _run(body):
    for _ in body:
        pass


def _interleave(bodies):
    progress = [0] * len(bodies)
    alive = [True] * len(bodies)
    while any(alive):
        k = min((i for i in range(len(bodies)) if alive[i]),
                key=lambda i: progress[i] / max(bodies[i][1], 1))
        try:
            next(bodies[k][0])
            progress[k] += 1
        except StopIteration:
            alive[k] = False


_IN_PROJ_COLS = 256
_IN_PROJ_CHUNKS = IN_PROJ_WIDTH // _IN_PROJ_COLS
_RET_BLOCKS = 4 * RET_PAIRS
_KV_COL0 = 4 * RET_WIDTH + ATTN_WIDTH
_BF16_ROWS = 16


def _cast_rows_per_step(n_rows, n_steps):
    for rows in range(_BF16_ROWS, n_rows + 1, _BF16_ROWS):
        if n_rows % rows == 0 and n_rows // rows <= n_steps:
            return rows
    raise ValueError(f"no bf16-tile row split of {n_rows} rows over {n_steps} steps")


def _in_proj_body(x_ref, wb_ref, ret_ref, aq_ref, akv_ref):
    xb = x_ref[0].astype(BF16)
    for c in range(_IN_PROJ_CHUNKS):
        u = _dot(xb, wb_ref[:, c * _IN_PROJ_COLS:(c + 1) * _IN_PROJ_COLS]).astype(BF16)
        if 2 * c < _RET_BLOCKS:
            ret_ref[0, 2 * c] = u[:, :LANES]
            ret_ref[0, 2 * c + 1] = u[:, LANES:]
        elif c < _RET_BLOCKS // 2 + ATTN_KV_HEADS:
            aq_ref[0, c - _RET_BLOCKS // 2] = u
        else:
            akv_ref[0, 0] = u[:, :LANES]
            akv_ref[0, 1] = u[:, LANES:]
        yield


def _in_proj_first_kernel(x_ref, w_ref, *refs, n_cast):
    cast_in, (ret_ref, aq_ref, akv_ref, wb_ref) = refs[:n_cast], refs[n_cast:n_cast + 4]
    cast_out = refs[n_cast + 4:]

    @pl.when((pl.program_id(0) == 0) & (pl.program_id(1) == 0))
    def _():
        wb_ref[:, 0:_KV_COL0] = w_ref[:, 0:_KV_COL0].astype(BF16)
        low = lax.broadcasted_iota(jnp.int32, (1, LANES), 1) < HEAD_DIM
        ak = w_ref[:, _KV_COL0:_KV_COL0 + KV_WIDTH]
        av = w_ref[:, _KV_COL0 + KV_WIDTH:_KV_COL0 + 2 * KV_WIDTH]
        wb_ref[:, _KV_COL0:_KV_COL0 + LANES] = jnp.where(
            low, ak, pltpu.roll(av, HEAD_DIM, 1)).astype(BF16)
        wb_ref[:, _KV_COL0 + LANES:_KV_COL0 + 2 * LANES] = jnp.where(
            low, pltpu.roll(ak, HEAD_DIM, 1), av).astype(BF16)

    for src, dst in zip(cast_in, cast_out):
        dst[...] = src[...].astype(BF16)

    _run(_in_proj_body(x_ref, wb_ref, ret_ref, aq_ref, akv_ref))


def _u_specs(group_batch, s):
    specs = [
        pl.BlockSpec((1, _RET_BLOCKS, SEQ_TILE, LANES), lambda i, j: (i, 0, j, 0)),
        pl.BlockSpec((1, ATTN_KV_HEADS, SEQ_TILE, ATTN_GROUP * HEAD_DIM), lambda i, j: (i, 0, j, 0)),
        pl.BlockSpec((1, ATTN_KV_HEADS, SEQ_TILE, LANES), lambda i, j: (i, 0, j, 0)),
    ]
    shapes = [
        jax.ShapeDtypeStruct((group_batch, _RET_BLOCKS, s, LANES), BF16),
        jax.ShapeDtypeStruct((group_batch, ATTN_KV_HEADS, s, ATTN_GROUP * HEAD_DIM), BF16),
        jax.ShapeDtypeStruct((group_batch, ATTN_KV_HEADS, s, LANES), BF16),
    ]
    return specs, shapes


def _x_spec(batch0, d):
    return pl.BlockSpec((1, SEQ_TILE, d), lambda i, j: (batch0 + i, j, 0))


def _in_proj_first(x, w_in, layer, cast_weights, group_batch):
    _, s, d = x.shape
    n_j = s // SEQ_TILE
    n_steps = group_batch * n_j
    cast_in_specs, cast_out_specs, cast_out_shapes = [], [], []
    for w in cast_weights:
        rows = _cast_rows_per_step(w.shape[1], n_steps)
        last = w.shape[1] // rows - 1
        cast_in_specs.append(pl.BlockSpec(
            (None, rows, w.shape[2]),
            lambda i, j, last=last: (layer, jnp.minimum(i * n_j + j, last), 0)))
        cast_out_specs.append(pl.BlockSpec(
            (rows, w.shape[2]), lambda i, j, last=last: (jnp.minimum(i * n_j + j, last), 0)))
        cast_out_shapes.append(jax.ShapeDtypeStruct(w.shape[1:], BF16))
    u_specs, u_shapes = _u_specs(group_batch, s)
    outs = pl.pallas_call(
        functools.partial(_in_proj_first_kernel, n_cast=len(cast_weights)),
        grid=(group_batch, n_j),
        in_specs=[
            _x_spec(0, d),
            pl.BlockSpec((None, d, IN_PROJ_WIDTH), lambda i, j: (layer, 0, 0),
                         pipeline_mode=pl.Buffered(1)),
        ] + cast_in_specs,
        out_specs=u_specs + [pl.BlockSpec((d, IN_PROJ_WIDTH), lambda i, j: (0, 0))] + cast_out_specs,
        out_shape=u_shapes + [jax.ShapeDtypeStruct((d, IN_PROJ_WIDTH), BF16)] + cast_out_shapes,
        compiler_params=pltpu.CompilerParams(
            dimension_semantics=("arbitrary", "arbitrary"), vmem_limit_bytes=VMEM_LIMIT_BYTES),
        name="in_proj_first",
    )(x, w_in, *cast_weights)
    return outs[:3], outs[3], outs[4:]


def _ret_body(q_ref, k_ref, v_ref, g_ref, decf_ref, decb_ref, gain_ref, o_ref, kv_ref, st_ref,
              p_ref, y_ref, *, seq):
    n_chunks = seq // CHUNK
    lane = lax.broadcasted_iota(jnp.int32, (1, LANES), 1)
    head0 = lane < HEAD_DIM

    def log_gamma(dec):
        return jnp.log1p(-jnp.exp2(dec))

    lgf0, lgf1 = log_gamma(decf_ref[0, 0:1, :]), log_gamma(decf_ref[0, 1:2, :])
    lgb0, lgb1 = log_gamma(decb_ref[0, 0:1, :]), log_gamma(decb_ref[0, 1:2, :])
    lgf = jnp.where(head0, lgf0, lgf1)
    lgb = jnp.where(head0, lgb0, lgb1)

    idx = lax.broadcasted_iota(jnp.int32, (CHUNK, LANES), 0).astype(F32)
    qdf = jnp.exp(lgf * (idx + 1.0))
    kdf = jnp.exp(lgf * (CHUNK - 1.0 - idx)) * QK_SCALE
    qdb = jnp.exp(lgb * (CHUNK - idx))
    kdb = jnp.exp(lgb * idx) * QK_SCALE
    cdf = jnp.exp(lgf * CHUNK)
    cdb = jnp.exp(lgb * CHUNK)

    ri = lax.broadcasted_iota(jnp.int32, (CHUNK, CHUNK), 0)
    ci = lax.broadcasted_iota(jnp.int32, (CHUNK, CHUNK), 1)
    diff = (ri - ci).astype(F32)

    def intra_mask(lf, lb):
        fwd = jnp.exp(lf * jnp.maximum(diff, 0.0))
        bwd = jnp.exp(lb * jnp.maximum(-diff, 0.0))
        return jnp.where(diff > 0, fwd, jnp.where(diff < 0, bwd, fwd + bwd))

    mask2 = jnp.concatenate([intra_mask(lgf0, lgb0), intra_mask(lgf1, lgb1)], axis=0) * QK_SCALE
    same_head = (ri < HEAD_DIM) == (ci < HEAD_DIM)
    gn_avg = jnp.where(same_head, 1.0 / HEAD_DIM, 0.0).astype(BF16)

    for n in range(n_chunks):
        sl = pl.ds(n * CHUNK, CHUNK)
        kc = k_ref[0, 0, sl, :].astype(F32)
        kd = jnp.concatenate([(kc * kdf).astype(BF16), (kc * kdb).astype(BF16)], axis=1)
        kv_ref[n] = _dot_tn(kd, v_ref[0, 0, sl, :])
        yield

    zero_state = jnp.zeros((LANES, LANES), BF16)
    state = jnp.zeros((LANES, LANES), F32)
    for n in range(n_chunks):
        st_ref[n, 0:LANES, :] = jnp.where(same_head, state.astype(BF16), zero_state)
        state = state * cdf + kv_ref[n, 0:LANES, :]
    yield
    state = jnp.zeros((LANES, LANES), F32)
    for n in reversed(range(n_chunks)):
        st_ref[n, LANES:2 * LANES, :] = jnp.where(same_head, state.astype(BF16), zero_state)
        state = state * cdb + kv_ref[n, LANES:2 * LANES, :]
    yield

    zero = jnp.zeros((CHUNK, LANES), BF16)
    for n in range(n_chunks):
        sl = pl.ds(n * CHUNK, CHUNK)
        qb = q_ref[0, 0, sl, :]
        kb = k_ref[0, 0, sl, :]
        qs = jnp.concatenate([jnp.where(head0, qb, zero), jnp.where(head0, zero, qb)], axis=0)
        p = (_dot_nt(qs, kb) * mask2).astype(BF16)
        p_ref[n, :, 0:CHUNK] = p[0:CHUNK]
        p_ref[n, :, CHUNK:2 * CHUNK] = p[CHUNK:2 * CHUNK]
        yield

    for n in range(n_chunks):
        sl = pl.ds(n * CHUNK, CHUNK)
        vb = v_ref[0, 0, sl, :]
        qc = q_ref[0, 0, sl, :].astype(F32)
        lhs = jnp.concatenate([p_ref[n], (qc * qdf).astype(BF16), (qc * qdb).astype(BF16)], axis=1)
        rhs = jnp.concatenate([jnp.where(head0, vb, zero), jnp.where(head0, zero, vb), st_ref[n]],
                              axis=0)
        y_ref[sl, :] = _dot(lhs, rhs)
        yield

    def group_mean(z):
        return _dot(z.astype(BF16), gn_avg)

    for n in range(n_chunks):
        sl = pl.ds(n * CHUNK, CHUNK)
        y = y_ref[sl, :]
        y_ref[sl, :] = y - group_mean(y)
        yield

    gain = gain_ref[0]
    for n in range(n_chunks):
        sl = pl.ds(n * CHUNK, CHUNK)
        dlt = y_ref[sl, :]
        var = group_mean(dlt * dlt)
        gate = g_ref[0, 0, sl, :].astype(F32)
        o_ref[0, sl, :] = (dlt * lax.rsqrt(var + GN_EPS) * gain
                           * (gate * jax.nn.sigmoid(gate))).astype(BF16)
        yield


def _ret_yields(seq):
    return 5 * (seq // CHUNK) + 2


def _ret_in_specs(s):
    def part(kind):
        return pl.BlockSpec((1, 1, s, LANES), lambda i, j, kind=kind: (i, kind * RET_PAIRS + j, 0, 0))

    return [
        part(0), part(1), part(2), part(3),
        pl.BlockSpec((1, HEADS_PER_LANE_BLOCK, LANES), lambda i, j: (j, 0, 0)),
        pl.BlockSpec((1, HEADS_PER_LANE_BLOCK, LANES), lambda i, j: (j, 0, 0)),
        pl.BlockSpec((1, 1, LANES), lambda i, j: (j, 0, 0)),
    ]


def _ret_out(group_batch, s):
    return (pl.BlockSpec((1, s, LANES), lambda i, j: (i, 0, j)),
            jax.ShapeDtypeStruct((group_batch, s, RET_WIDTH), BF16))


def _ret_scratch(s):
    n_chunks = s // CHUNK
    return [
        pltpu.VMEM((n_chunks, 2 * LANES, LANES), F32),
        pltpu.VMEM((n_chunks, 2 * LANES, LANES), BF16),
        pltpu.VMEM((n_chunks, CHUNK, 2 * CHUNK), BF16),
        pltpu.VMEM((s, LANES), F32),
    ]


def _ret_kernel(*refs, seq):
    _run(_ret_body(*refs, seq=seq))


def _retention(ret, decf, decb, gain):
    group_batch, _, s, _ = ret.shape
    out_spec, out_shape = _ret_out(group_batch, s)
    return pl.pallas_call(
        functools.partial(_ret_kernel, seq=s),
        grid=(group_batch, RET_PAIRS),
        in_specs=_ret_in_specs(s),
        out_specs=out_spec,
        out_shape=out_shape,
        scratch_shapes=_ret_scratch(s),
        compiler_params=pltpu.CompilerParams(
            dimension_semantics=("arbitrary", "arbitrary"), vmem_limit_bytes=VMEM_LIMIT_BYTES),
        name="retention",
    )(ret, ret, ret, ret, decf, decb, gain)


_STACK = 2 * BLOCK
_KEYS = 3 * BLOCK
_SUBLANES = 8
_LOG2E = 1.4426950408889634
_ATTN_STAGE_LAG = (0, 2, 4, 5)
_SEQ_HALVES = 2
_BIAS_MID, _BIAS_FIRST, _BIAS_LAST = 0, 1, 2


def _attn_body(q_ref, kv_ref, slope_ref, sink_ref, o_ref, ks_ref, kss_ref, vt_ref, bias_ref,
               *, seq, half):
    n_blocks = seq // BLOCK // _SEQ_HALVES
    rows = n_blocks * BLOCK
    lane = lax.broadcasted_iota(jnp.int32, (1, LANES), 1)
    low = lane < HEAD_DIM

    base = half * rows
    pieces = ((jnp.maximum(base - BLOCK, 0), 0, BLOCK),
              (base, BLOCK, rows),
              (jnp.minimum(base + rows, seq - BLOCK), BLOCK + rows, BLOCK))
    for start, loc, size in pieces:
        kv = kv_ref[0, 0, pl.ds(pl.multiple_of(start, BLOCK), size), :].astype(F32)
        ks = kv * (QK_SCALE * _LOG2E)
        ks_ref[loc:loc + size, :] = ks.astype(BF16)
        kss_ref[loc:loc + size, :] = pltpu.roll(ks, HEAD_DIM, 1).astype(BF16)
        kv_t = jnp.transpose(kv)
        vt_ref[:, loc:loc + size] = jnp.concatenate(
            [kv_t[HEAD_DIM:], jnp.ones((HEAD_DIM, size), F32)], axis=0).astype(BF16)

    kj = lax.broadcasted_iota(jnp.int32, (_KEYS, _STACK), 0)
    c = lax.broadcasted_iota(jnp.int32, (_KEYS, _STACK), 1)
    qi = jnp.where(c < BLOCK, c, c - BLOCK)
    dist = jnp.abs(kj - BLOCK - qi)
    for par in range(2):
        slope = slope_ref[0, par, 0:1, :]
        mid = jnp.where(dist <= WINDOW, -slope * dist.astype(F32), NEG_INF) * _LOG2E
        bias_ref[par, _BIAS_MID] = mid
        bias_ref[par, _BIAS_FIRST] = jnp.where(kj >= BLOCK, mid, NEG_INF * _LOG2E)
        bias_ref[par, _BIAS_LAST] = jnp.where(kj < 2 * BLOCK, mid, NEG_INF * _LOG2E)
    yield

    zero = jnp.zeros((BLOCK, LANES), BF16)

    units = [(n, par) for n in range(n_blocks) for par in range(2)]
    live = [dict() for _ in units]

    def bias_variant(n):
        if n == 0:
            return jnp.where(half == 0, _BIAS_FIRST, _BIAS_MID)
        if n == n_blocks - 1:
            return jnp.where(half == _SEQ_HALVES - 1, _BIAS_LAST, _BIAS_MID)
        return _BIAS_MID

    def stage_scores(u):
        n, par = units[u]
        keys = pl.ds(n * BLOCK, _KEYS)
        qblk = q_ref[0, 0, pl.ds(n * BLOCK, BLOCK), :]
        qa, qb = qblk[:, :LANES], qblk[:, LANES:]
        if par == 0:
            qs = jnp.concatenate([jnp.where(low, qa, zero), jnp.where(low, qb, zero)], axis=0)
            kmat = ks_ref[keys, :]
        else:
            qs = jnp.concatenate([jnp.where(low, zero, qa), jnp.where(low, zero, qb)], axis=0)
            kmat = kss_ref[keys, :]
        live[u]["s"] = _dot_nt(kmat, qs) + bias_ref[par, bias_variant(n)]

    def stage_softmax(u):
        _, par = units[u]
        st = live[u]
        s = st.pop("s")
        sink = sink_ref[0, par, 0:1, :] * _LOG2E
        m = jnp.maximum(jnp.max(s, axis=0, keepdims=True), sink)
        st["e"] = jnp.exp2(s - m).astype(BF16)
        st["esink"] = jnp.exp2(sink - m)

    def stage_values(u):
        n, _ = units[u]
        live[u]["pv"] = _dot(vt_ref[:, pl.ds(n * BLOCK, _KEYS)], live[u].pop("e"))

    def stage_out(u):
        n, par = units[u]
        st = live[u]
        pv = st.pop("pv")
        st["out"] = pv[:HEAD_DIM] / (pv[HEAD_DIM:] + st.pop("esink"))
        if par == 1:
            even, odd = live[u - 1].pop("out"), st.pop("out")
            y_t = jnp.concatenate(
                [even[:, :BLOCK], odd[:, :BLOCK], even[:, BLOCK:], odd[:, BLOCK:]], axis=0)
            o_ref[0, pl.ds(n * BLOCK, BLOCK), :] = jnp.transpose(y_t).astype(BF16)

    stages = (stage_scores, stage_softmax, stage_values, stage_out)
    for i in range(len(units) + _ATTN_STAGE_LAG[-1]):
        for lag, stage in zip(_ATTN_STAGE_LAG, stages):
            if 0 <= i - lag < len(units):
                stage(i - lag)
        yield


def _attn_yields(seq):
    return 1 + 2 * (seq // BLOCK // _SEQ_HALVES) + _ATTN_STAGE_LAG[-1]


def _attn_in_specs(s):
    qw = ATTN_GROUP * HEAD_DIM
    return [
        pl.BlockSpec((1, 1, s // _SEQ_HALVES, qw),
                     lambda i, j: (i, j // _SEQ_HALVES, j % _SEQ_HALVES, 0)),
        pl.BlockSpec((1, 1, s, LANES), lambda i, j: (i, j // _SEQ_HALVES, 0, 0)),
        pl.BlockSpec((1, 2, _SUBLANES, _STACK), lambda i, j: (j // _SEQ_HALVES, 0, 0, 0)),
        pl.BlockSpec((1, 2, _SUBLANES, _STACK), lambda i, j: (j // _SEQ_HALVES, 0, 0, 0)),
    ]


def _attn_out(group_batch, s):
    qw = ATTN_GROUP * HEAD_DIM
    return (pl.BlockSpec((1, s // _SEQ_HALVES, qw),
                         lambda i, j: (i, j % _SEQ_HALVES, j // _SEQ_HALVES)),
            jax.ShapeDtypeStruct((group_batch, s, ATTN_WIDTH), BF16))


def _attn_scratch(s):
    local_keys = s // _SEQ_HALVES + 2 * BLOCK
    return [
        pltpu.VMEM((local_keys, LANES), BF16),
        pltpu.VMEM((local_keys, LANES), BF16),
        pltpu.VMEM((LANES, local_keys), BF16),
        pltpu.VMEM((2, 3, _KEYS, _STACK), F32),
    ]


def _attn_kernel(*refs, seq):
    _run(_attn_body(*refs, seq=seq, half=pl.program_id(1) % _SEQ_HALVES))


def _attention(aq, akv, slope_cols, sink_cols):
    group_batch, _, s, _ = aq.shape
    out_spec, out_shape = _attn_out(group_batch, s)
    return pl.pallas_call(
        functools.partial(_attn_kernel, seq=s),
        grid=(group_batch, ATTN_KV_HEADS * _SEQ_HALVES),
        in_specs=_attn_in_specs(s),
        out_specs=out_spec,
        out_shape=out_shape,
        scratch_shapes=_attn_scratch(s),
        compiler_params=pltpu.CompilerParams(
            dimension_semantics=("arbitrary", "arbitrary"), vmem_limit_bytes=VMEM_LIMIT_BYTES),
        name="attention",
    )(aq, akv, slope_cols, sink_cols)


_N_RET_IN, _N_ATTN_IN = 7, 4


def _fused_kernel(x_ref, wb_ref, *refs, seq):
    ret_in, refs = refs[:_N_RET_IN], refs[_N_RET_IN:]
    attn_in, refs = refs[:_N_ATTN_IN], refs[_N_ATTN_IN:]
    (ret_ref, aq_ref, akv_ref, yr_ref, ya_ref), refs = refs[:5], refs[5:]
    ret_scr, attn_scr = refs[:4], refs[4:]
    _interleave([
        (_in_proj_body(x_ref, wb_ref, ret_ref, aq_ref, akv_ref), _IN_PROJ_CHUNKS),
        (_ret_body(*ret_in, yr_ref, *ret_scr, seq=seq), _ret_yields(seq)),
        (_attn_body(*attn_in, ya_ref, *attn_scr, seq=seq,
                    half=pl.program_id(1) % _SEQ_HALVES), _attn_yields(seq)),
    ])


def _fused(x, wb, next_batch0, ret, aq, akv, decf, decb, gain, slope_cols, sink_cols):
    group_batch, _, s, _ = ret.shape
    d = x.shape[2]
    assert s // SEQ_TILE == RET_PAIRS == ATTN_KV_HEADS * _SEQ_HALVES
    u_specs, u_shapes = _u_specs(group_batch, s)
    yr_spec, yr_shape = _ret_out(group_batch, s)
    ya_spec, ya_shape = _attn_out(group_batch, s)
    outs = pl.pallas_call(
        functools.partial(_fused_kernel, seq=s),
        grid=(group_batch, RET_PAIRS),
        in_specs=[_x_spec(next_batch0, d),
                  pl.BlockSpec((d, IN_PROJ_WIDTH), lambda i, j: (0, 0), pipeline_mode=pl.Buffered(1))]
        + _ret_in_specs(s) + _attn_in_specs(s),
        out_specs=u_specs + [yr_spec, ya_spec],
        out_shape=u_shapes + [yr_shape, ya_shape],
        scratch_shapes=_ret_scratch(s) + _attn_scratch(s),
        compiler_params=pltpu.CompilerParams(
            dimension_semantics=("arbitrary", "arbitrary"), vmem_limit_bytes=VMEM_LIMIT_BYTES),
        name="in_proj_mixers",
    )(x, wb, ret, ret, ret, ret, decf, decb, gain, aq, akv, slope_cols, sink_cols)
    return outs[:3], outs[3], outs[4]


_FFN_SUB_TILES = 2


def _layer_norm(z, gain, bias):
    mu = jnp.mean(z, axis=-1, keepdims=True)
    dlt = z - mu
    var = jnp.mean(dlt * dlt, axis=-1, keepdims=True)
    return dlt * lax.rsqrt(var + LN_EPS) * gain + bias


def _ffn_kernel(x_ref, yr_ref, ya_ref, p_ref, wo_ref, g1_ref, b1_ref, wg_ref, wu_ref, wd_ref,
                wpe_ref, wpg_ref, g2_ref, b2_ref, *rest, alpha):
    o_ref = rest[-1]
    tm = x_ref.shape[0]
    rows = [pl.ds(k * (tm // _FFN_SUB_TILES), tm // _FFN_SUB_TILES) for k in range(_FFN_SUB_TILES)]
    mix = [_dot(yr_ref[r, :], wo_ref[0:RET_WIDTH, :]) + _dot(ya_ref[r, :], wo_ref[RET_WIDTH:, :])
           for r in rows]
    h = [_layer_norm(alpha * x_ref[r, :] + m, g1_ref[...], b1_ref[...]) for r, m in zip(rows, mix)]
    z = []
    for r, hk in zip(rows, h):
        hb = hk.astype(BF16)
        g = _dot(hb, wg_ref[...])
        up = _dot(hb, wu_ref[...])
        act = (g * jax.nn.sigmoid(g) * up).astype(BF16)
        ffn = _dot(act, wd_ref[...])
        ple = _dot(p_ref[r, :].astype(BF16), wpe_ref[...])
        ple_gate = jax.nn.sigmoid(_dot(hb, wpg_ref[...]))
        z.append(alpha * hk + ffn + ple_gate * ple)
    for r, zk in zip(rows, z):
        o_ref[r, :] = _layer_norm(zk, g2_ref[...], b2_ref[...])


def _ffn(x, yr, ya, p, weights, out_prev, row0, *, alpha):
    t, d = x.shape
    tm = SEQ_TILE
    blk0 = row0 // tm

    def rows(width, offset):
        return pl.BlockSpec((tm, width), lambda i: (offset + i, 0))

    def resident(arr):
        return pl.BlockSpec(arr.shape, lambda i: (0, 0), pipeline_mode=pl.Buffered(1))

    in_specs = [rows(d, blk0), rows(RET_WIDTH, 0), rows(ATTN_WIDTH, 0), rows(PLE_DIM, blk0)] \
        + [resident(w) for w in weights]
    args = [x, yr, ya, p, *weights]
    aliases = {}
    if out_prev is not None:
        in_specs.append(pl.BlockSpec(memory_space=pl.ANY))
        args.append(out_prev)
        aliases = {len(args) - 1: 0}
    return pl.pallas_call(
        functools.partial(_ffn_kernel, alpha=alpha),
        grid=(yr.shape[0] // tm,),
        in_specs=in_specs,
        out_specs=rows(d, blk0),
        out_shape=jax.ShapeDtypeStruct((t, d), F32),
        input_output_aliases=aliases,
        compiler_params=pltpu.CompilerParams(
            dimension_semantics=("arbitrary",), vmem_limit_bytes=VMEM_LIMIT_BYTES),
        name="ffn",
    )(*args)


def _stacked_head_cols(per_head):
    v = per_head.astype(F32).reshape(ATTN_KV_HEADS, 2, 2)
    v = jnp.transpose(v, (0, 2, 1))
    v = jnp.repeat(v, BLOCK, axis=2)
    return jnp.broadcast_to(v[:, :, None, :], v.shape[:2] + (_SUBLANES, _STACK))


def kernel(x, p, w_in, ret_decay_fwd, ret_decay_bwd, ret_gn_gain, attn_sink, w_out, ln1_gain, ln1_bias,
           w_ffn_gate, w_ffn_up, w_ffn_down, w_ple_proj, w_ple_gate, ln2_gain, ln2_bias):
    depth = w_in.shape[0]
    b, s, d = x.shape
    assert b % BATCH_GROUPS == 0
    gb = b // BATCH_GROUPS
    alpha = (2.0 * depth) ** 0.25
    slopes = 2.0 ** (-8.0 * (jnp.arange(ATTN_HEADS, dtype=F32) + 1.0) / ATTN_HEADS)
    slope_cols = _stacked_head_cols(slopes)

    def lane_rep(v):
        v = v.astype(F32).reshape(RET_PAIRS, HEADS_PER_LANE_BLOCK, 1)
        return jnp.broadcast_to(v, (RET_PAIRS, HEADS_PER_LANE_BLOCK, LANES))

    row = lambda v: v.astype(F32).reshape(1, -1)
    h = x.astype(F32)
    for i in range(depth):
        decf, decb = lane_rep(ret_decay_fwd[i]), lane_rep(ret_decay_bwd[i])
        gain = ret_gn_gain[i].astype(F32).reshape(RET_PAIRS, 1, LANES)
        sink_cols = _stacked_head_cols(attn_sink[i])

        u, wb, (wo, wg, wu, wd, wpe, wpg) = _in_proj_first(
            h, w_in, i, (w_out, w_ffn_gate, w_ffn_up, w_ffn_down, w_ple_proj, w_ple_gate), gb)
        mixed = []
        for g in range(BATCH_GROUPS):
            if g + 1 < BATCH_GROUPS:
                u_next, y_ret, y_att = _fused(h, wb, (g + 1) * gb, *u, decf, decb, gain,
                                              slope_cols, sink_cols)
            else:
                u_next = None
                y_ret = _retention(u[0], decf, decb, gain)
                y_att = _attention(u[1], u[2], slope_cols, sink_cols)
            mixed.append((y_ret, y_att))
            u = u_next

        weights = (wo, row(ln1_gain[i]), row(ln1_bias[i]), wg, wu, wd, wpe, wpg,
                   row(ln2_gain[i]), row(ln2_bias[i]))
        out = None
        for g, (y_ret, y_att) in enumerate(mixed):
            out = _ffn(h.reshape(b * s, d), y_ret.reshape(gb * s, RET_WIDTH),
                       y_att.reshape(gb * s, ATTN_WIDTH), p[i].reshape(b * s, PLE_DIM),
                       weights, out, g * gb * s, alpha=alpha)
        h = out.reshape(b, s, d)
    return h.astype(x.dtype)
```

```python
import functools

import jax
import jax.numpy as jnp
from jax import lax
from jax.experimental import pallas as pl
from jax.experimental.pallas import tpu as pltpu

D_MODEL = 1024
HEAD_DIM = 64
RET_HEADS = 8
ATTN_HEADS = 8
ATTN_KV_HEADS = 2
ATTN_GROUP = ATTN_HEADS // ATTN_KV_HEADS
RET_WIDTH = RET_HEADS * HEAD_DIM
ATTN_WIDTH = ATTN_HEADS * HEAD_DIM
KV_WIDTH = ATTN_KV_HEADS * HEAD_DIM
IN_PROJ_WIDTH = 4 * RET_WIDTH + ATTN_WIDTH + 2 * KV_WIDTH
CHUNK = 128
WINDOW = 128
BLOCK = 128
PLE_DIM = 256
FFN_HIDDEN = ((8 * D_MODEL + 767) // 768) * 256
LN_EPS = 1e-5
GN_EPS = 1e-5
NEG_INF = -1e30
QK_SCALE = HEAD_DIM ** -0.5

LANES = 128
HEADS_PER_LANE_BLOCK = LANES // HEAD_DIM
RET_PAIRS = RET_HEADS // HEADS_PER_LANE_BLOCK
VMEM_LIMIT_BYTES = 56 * 1024 * 1024
IN_PROJ_TILE = 1024
FFN_TILE = 512

F32 = jnp.float32
BF16 = jnp.bfloat16


def _dot(a, b):
    return jnp.dot(a, b, preferred_element_type=F32)


def _dot_nt(a, b):
    return lax.dot_general(a, b, (((1,), (1,)), ((), ())), preferred_element_type=F32)


def _dot_tn(a, b):
    return lax.dot_general(a, b, (((0,), (0,)), ((), ())), preferred_element_type=F32)


_IN_PROJ_COLS = 256
_RET_BLOCKS = 4 * RET_PAIRS
_KV_COL0 = 4 * RET_WIDTH + ATTN_WIDTH
_BF16_ROWS = 16


def _cast_rows_per_step(n_rows, n_steps):
    for rows in range(_BF16_ROWS, n_rows + 1, _BF16_ROWS):
        if n_rows % rows == 0 and n_rows // rows <= n_steps:
            return rows
    raise ValueError(f"no bf16-tile row split of {n_rows} rows over {n_steps} steps")


def _in_proj_kernel(x_ref, w_ref, *refs, n_cast):
    cast_in, (ret_ref, aq_ref, akv_ref) = refs[:n_cast], refs[n_cast:n_cast + 3]
    cast_out, wb_ref = refs[n_cast + 3:2 * n_cast + 3], refs[2 * n_cast + 3]

    @pl.when((pl.program_id(0) == 0) & (pl.program_id(1) == 0))
    def _():
        wb_ref[:, 0:_KV_COL0] = w_ref[:, 0:_KV_COL0].astype(BF16)
        low = lax.broadcasted_iota(jnp.int32, (1, LANES), 1) < HEAD_DIM
        ak = w_ref[:, _KV_COL0:_KV_COL0 + KV_WIDTH]
        av = w_ref[:, _KV_COL0 + KV_WIDTH:_KV_COL0 + 2 * KV_WIDTH]
        wb_ref[:, _KV_COL0:_KV_COL0 + LANES] = jnp.where(
            low, ak, pltpu.roll(av, HEAD_DIM, 1)).astype(BF16)
        wb_ref[:, _KV_COL0 + LANES:_KV_COL0 + 2 * LANES] = jnp.where(
            low, pltpu.roll(ak, HEAD_DIM, 1), av).astype(BF16)

    for src, dst in zip(cast_in, cast_out):
        dst[...] = src[...].astype(BF16)

    xb = x_ref[0].astype(BF16)
    for c in range(IN_PROJ_WIDTH // _IN_PROJ_COLS):
        u = _dot(xb, wb_ref[:, c * _IN_PROJ_COLS:(c + 1) * _IN_PROJ_COLS]).astype(BF16)
        if 2 * c < _RET_BLOCKS:
            ret_ref[0, 2 * c] = u[:, :LANES]
            ret_ref[0, 2 * c + 1] = u[:, LANES:]
        elif c < _RET_BLOCKS // 2 + ATTN_KV_HEADS:
            aq_ref[0, c - _RET_BLOCKS // 2] = u
        else:
            akv_ref[0, 0] = u[:, :LANES]
            akv_ref[0, 1] = u[:, LANES:]


def _in_proj(x, w_in, layer, cast_weights, tm):
    b, s, d = x.shape
    n_j = s // tm
    n_steps = b * n_j

    def cast_spec(w):
        rows = _cast_rows_per_step(w.shape[1], n_steps)
        last = w.shape[1] // rows - 1
        return rows, lambda i, j: (layer, jnp.minimum(i * n_j + j, last), 0)

    cast_in_specs, cast_out_specs, cast_out_shapes = [], [], []
    for w in cast_weights:
        rows, imap = cast_spec(w)
        cast_in_specs.append(pl.BlockSpec((None, rows, w.shape[2]), imap))
        cast_out_specs.append(pl.BlockSpec((rows, w.shape[2]), lambda i, j, imap=imap: imap(i, j)[1:]))
        cast_out_shapes.append(jax.ShapeDtypeStruct(w.shape[1:], BF16))

    outs = pl.pallas_call(
        functools.partial(_in_proj_kernel, n_cast=len(cast_weights)),
        grid=(b, n_j),
        in_specs=[
            pl.BlockSpec((1, tm, d), lambda i, j: (i, j, 0)),
            pl.BlockSpec((None, d, IN_PROJ_WIDTH), lambda i, j: (layer, 0, 0),
                         pipeline_mode=pl.Buffered(1)),
        ] + cast_in_specs,
        out_specs=[
            pl.BlockSpec((1, _RET_BLOCKS, tm, LANES), lambda i, j: (i, 0, j, 0)),
            pl.BlockSpec((1, ATTN_KV_HEADS, tm, ATTN_GROUP * HEAD_DIM), lambda i, j: (i, 0, j, 0)),
            pl.BlockSpec((1, ATTN_KV_HEADS, tm, LANES), lambda i, j: (i, 0, j, 0)),
        ] + cast_out_specs,
        out_shape=[
            jax.ShapeDtypeStruct((b, _RET_BLOCKS, s, LANES), BF16),
            jax.ShapeDtypeStruct((b, ATTN_KV_HEADS, s, ATTN_GROUP * HEAD_DIM), BF16),
            jax.ShapeDtypeStruct((b, ATTN_KV_HEADS, s, LANES), BF16),
        ] + cast_out_shapes,
        scratch_shapes=[pltpu.VMEM((d, IN_PROJ_WIDTH), BF16)],
        compiler_params=pltpu.CompilerParams(
            dimension_semantics=("arbitrary", "arbitrary"), vmem_limit_bytes=VMEM_LIMIT_BYTES),
        name="in_proj",
    )(x, w_in, *cast_weights)
    return outs[:3], outs[3:]


def _ret_kernel(q_ref, k_ref, v_ref, g_ref, decf_ref, decb_ref, gain_ref, o_ref, kv_ref, st_ref,
                p_ref, y_ref, *, seq):
    n_chunks = seq // CHUNK
    lane = lax.broadcasted_iota(jnp.int32, (1, LANES), 1)
    head0 = lane < HEAD_DIM

    def log_gamma(dec):
        return jnp.log1p(-jnp.exp2(dec))

    lgf0, lgf1 = log_gamma(decf_ref[0, 0:1, :]), log_gamma(decf_ref[0, 1:2, :])
    lgb0, lgb1 = log_gamma(decb_ref[0, 0:1, :]), log_gamma(decb_ref[0, 1:2, :])
    lgf = jnp.where(head0, lgf0, lgf1)
    lgb = jnp.where(head0, lgb0, lgb1)

    idx = lax.broadcasted_iota(jnp.int32, (CHUNK, LANES), 0).astype(F32)
    qdf = jnp.exp(lgf * (idx + 1.0))
    kdf = jnp.exp(lgf * (CHUNK - 1.0 - idx)) * QK_SCALE
    qdb = jnp.exp(lgb * (CHUNK - idx))
    kdb = jnp.exp(lgb * idx) * QK_SCALE
    cdf = jnp.exp(lgf * CHUNK)
    cdb = jnp.exp(lgb * CHUNK)

    ri = lax.broadcasted_iota(jnp.int32, (CHUNK, CHUNK), 0)
    ci = lax.broadcasted_iota(jnp.int32, (CHUNK, CHUNK), 1)
    diff = (ri - ci).astype(F32)

    def intra_mask(lf, lb):
        fwd = jnp.exp(lf * jnp.maximum(diff, 0.0))
        bwd = jnp.exp(lb * jnp.maximum(-diff, 0.0))
        return jnp.where(diff > 0, fwd, jnp.where(diff < 0, bwd, fwd + bwd))

    mask2 = jnp.concatenate([intra_mask(lgf0, lgb0), intra_mask(lgf1, lgb1)], axis=0) * QK_SCALE
    same_head = (ri < HEAD_DIM) == (ci < HEAD_DIM)
    gn_avg = jnp.where(same_head, 1.0 / HEAD_DIM, 0.0).astype(BF16)

    for n in range(n_chunks):
        sl = pl.ds(n * CHUNK, CHUNK)
        kc = k_ref[0, 0, sl, :].astype(F32)
        kd = jnp.concatenate([(kc * kdf).astype(BF16), (kc * kdb).astype(BF16)], axis=1)
        kv = _dot_tn(kd, v_ref[0, 0, sl, :])
        kv_ref[n] = kv

    zero_state = jnp.zeros((LANES, LANES), BF16)
    state = jnp.zeros((LANES, LANES), F32)
    for n in range(n_chunks):
        st_ref[n, 0:LANES, :] = jnp.where(same_head, state.astype(BF16), zero_state)
        state = state * cdf + kv_ref[n, 0:LANES, :]
    state = jnp.zeros((LANES, LANES), F32)
    for n in reversed(range(n_chunks)):
        st_ref[n, LANES:2 * LANES, :] = jnp.where(same_head, state.astype(BF16), zero_state)
        state = state * cdb + kv_ref[n, LANES:2 * LANES, :]

    zero = jnp.zeros((CHUNK, LANES), BF16)
    for n in range(n_chunks):
        sl = pl.ds(n * CHUNK, CHUNK)
        qb = q_ref[0, 0, sl, :]
        kb = k_ref[0, 0, sl, :]
        qs = jnp.concatenate([jnp.where(head0, qb, zero), jnp.where(head0, zero, qb)], axis=0)
        p = (_dot_nt(qs, kb) * mask2).astype(BF16)
        p_ref[n, :, 0:CHUNK] = p[0:CHUNK]
        p_ref[n, :, CHUNK:2 * CHUNK] = p[CHUNK:2 * CHUNK]

    for n in range(n_chunks):
        sl = pl.ds(n * CHUNK, CHUNK)
        vb = v_ref[0, 0, sl, :]
        qc = q_ref[0, 0, sl, :].astype(F32)
        lhs = jnp.concatenate([p_ref[n], (qc * qdf).astype(BF16), (qc * qdb).astype(BF16)], axis=1)
        rhs = jnp.concatenate([jnp.where(head0, vb, zero), jnp.where(head0, zero, vb), st_ref[n]],
                              axis=0)
        y_ref[sl, :] = _dot(lhs, rhs)

    def group_mean(z):
        return _dot(z.astype(BF16), gn_avg)

    for n in range(n_chunks):
        sl = pl.ds(n * CHUNK, CHUNK)
        y = y_ref[sl, :]
        y_ref[sl, :] = y - group_mean(y)

    gain = gain_ref[0]
    for n in range(n_chunks):
        sl = pl.ds(n * CHUNK, CHUNK)
        dlt = y_ref[sl, :]
        var = group_mean(dlt * dlt)
        gate = g_ref[0, 0, sl, :].astype(F32)
        o_ref[0, sl, :] = (dlt * lax.rsqrt(var + GN_EPS) * gain
                           * (gate * jax.nn.sigmoid(gate))).astype(BF16)


def _retention(ret, decf, decb, gain):
    b, _, s, _ = ret.shape
    n_chunks = s // CHUNK

    def part(kind):
        return pl.BlockSpec((1, 1, s, LANES), lambda i, j, kind=kind: (i, kind * RET_PAIRS + j, 0, 0))

    return pl.pallas_call(
        functools.partial(_ret_kernel, seq=s),
        grid=(b, RET_PAIRS),
        in_specs=[
            part(0), part(1), part(2), part(3),
            pl.BlockSpec((1, HEADS_PER_LANE_BLOCK, LANES), lambda i, j: (j, 0, 0)),
            pl.BlockSpec((1, HEADS_PER_LANE_BLOCK, LANES), lambda i, j: (j, 0, 0)),
            pl.BlockSpec((1, 1, LANES), lambda i, j: (j, 0, 0)),
        ],
        out_specs=pl.BlockSpec((1, s, LANES), lambda i, j: (i, 0, j)),
        out_shape=jax.ShapeDtypeStruct((b, s, RET_WIDTH), BF16),
        scratch_shapes=[
            pltpu.VMEM((n_chunks, 2 * LANES, LANES), F32),
            pltpu.VMEM((n_chunks, 2 * LANES, LANES), BF16),
            pltpu.VMEM((n_chunks, CHUNK, 2 * CHUNK), BF16),
            pltpu.VMEM((s, LANES), F32),
        ],
        compiler_params=pltpu.CompilerParams(
            dimension_semantics=("arbitrary", "arbitrary"), vmem_limit_bytes=VMEM_LIMIT_BYTES),
        name="retention",
    )(ret, ret, ret, ret, decf, decb, gain)


_STACK = 2 * BLOCK
_KEYS = 3 * BLOCK
_SUBLANES = 8
_LOG2E = 1.4426950408889634
_ATTN_STAGE_LAG = (0, 2, 4, 5)


def _attn_kernel(q_ref, kv_ref, slope_ref, sink_ref, o_ref, ks_ref, kss_ref, vt_ref, bias_ref,
                 *, seq):
    n_blocks = seq // BLOCK
    lane = lax.broadcasted_iota(jnp.int32, (1, LANES), 1)
    low = lane < HEAD_DIM

    kv = kv_ref[0, 0].astype(F32)
    ks = kv * (QK_SCALE * _LOG2E)
    ks_ref[...] = ks.astype(BF16)
    kss_ref[...] = pltpu.roll(ks, HEAD_DIM, 1).astype(BF16)
    kv_t = jnp.transpose(kv)
    vt_ref[...] = jnp.concatenate(
        [kv_t[HEAD_DIM:], jnp.ones((_BF16_ROWS, seq), F32)], axis=0).astype(BF16)

    kj = lax.broadcasted_iota(jnp.int32, (_KEYS, _STACK), 0)
    c = lax.broadcasted_iota(jnp.int32, (_KEYS, _STACK), 1)
    qi = jnp.where(c < BLOCK, c, c - BLOCK)
    dist = jnp.abs(kj - BLOCK - qi)
    for par in range(2):
        slope = slope_ref[0, par, 0:1, :]
        bias_ref[par] = jnp.where(dist <= WINDOW, -slope * dist.astype(F32), NEG_INF) * _LOG2E

    zero = jnp.zeros((BLOCK, LANES), BF16)

    def key_range(n):
        lo_blk, hi_blk = max(n - 1, 0), min(n + 2, n_blocks)
        return pl.ds(lo_blk * BLOCK, (hi_blk - lo_blk) * BLOCK), (lo_blk - (n - 1)) * BLOCK, \
            (hi_blk - lo_blk) * BLOCK

    units = [(n, par) for n in range(n_blocks) for par in range(2)]
    live = [dict() for _ in units]

    def stage_scores(u):
        n, par = units[u]
        keys, c0, width = key_range(n)
        qblk = q_ref[0, 0, pl.ds(n * BLOCK, BLOCK), :]
        qa, qb = qblk[:, :LANES], qblk[:, LANES:]
        if par == 0:
            qs = jnp.concatenate([jnp.where(low, qa, zero), jnp.where(low, qb, zero)], axis=0)
            kmat = ks_ref[keys, :]
        else:
            qs = jnp.concatenate([jnp.where(low, zero, qa), jnp.where(low, zero, qb)], axis=0)
            kmat = kss_ref[keys, :]
        live[u]["s"] = _dot_nt(kmat, qs) + bias_ref[par, c0:c0 + width, :]

    def stage_softmax(u):
        _, par = units[u]
        st = live[u]
        s = st.pop("s")
        sink = sink_ref[0, par, 0:1, :] * _LOG2E
        m = jnp.maximum(jnp.max(s, axis=0, keepdims=True), sink)
        st["e"] = jnp.exp2(s - m).astype(BF16)
        st["esink"] = jnp.exp2(sink - m)

    def stage_values(u):
        n, _ = units[u]
        keys, _, _ = key_range(n)
        live[u]["pv"] = _dot(vt_ref[:, keys], live[u].pop("e"))

    def stage_out(u):
        n, par = units[u]
        st = live[u]
        pv = st.pop("pv")
        st["out"] = pv[:HEAD_DIM] / (pv[HEAD_DIM:HEAD_DIM + 1] + st.pop("esink"))
        if par == 1:
            even, odd = live[u - 1].pop("out"), st.pop("out")
            y_t = jnp.concatenate(
                [even[:, :BLOCK], odd[:, :BLOCK], even[:, BLOCK:], odd[:, BLOCK:]], axis=0)
            o_ref[0, pl.ds(n * BLOCK, BLOCK), :] = jnp.transpose(y_t).astype(BF16)

    stages = (stage_scores, stage_softmax, stage_values, stage_out)
    for i in range(len(units) + _ATTN_STAGE_LAG[-1]):
        for lag, stage in zip(_ATTN_STAGE_LAG, stages):
            if 0 <= i - lag < len(units):
                stage(i - lag)


def _attention(aq, akv, slope_cols, sink_cols):
    b, _, s, qw = aq.shape
    return pl.pallas_call(
        functools.partial(_attn_kernel, seq=s),
        grid=(b, ATTN_KV_HEADS),
        in_specs=[
            pl.BlockSpec((1, 1, s, qw), lambda i, j: (i, j, 0, 0)),
            pl.BlockSpec((1, 1, s, LANES), lambda i, j: (i, j, 0, 0)),
            pl.BlockSpec((1, 2, _SUBLANES, _STACK), lambda i, j: (j, 0, 0, 0)),
            pl.BlockSpec((1, 2, _SUBLANES, _STACK), lambda i, j: (j, 0, 0, 0)),
        ],
        out_specs=pl.BlockSpec((1, s, qw), lambda i, j: (i, 0, j)),
        out_shape=jax.ShapeDtypeStruct((b, s, ATTN_WIDTH), BF16),
        scratch_shapes=[
            pltpu.VMEM((s, LANES), BF16),
            pltpu.VMEM((s, LANES), BF16),
            pltpu.VMEM((HEAD_DIM + _BF16_ROWS, s), BF16),
            pltpu.VMEM((2, _KEYS, _STACK), F32),
        ],
        compiler_params=pltpu.CompilerParams(
            dimension_semantics=("arbitrary", "arbitrary"), vmem_limit_bytes=VMEM_LIMIT_BYTES),
        name="attention",
    )(aq, akv, slope_cols, sink_cols)


_FFN_SUB_TILES = 2


def _layer_norm(z, gain, bias):
    mu = jnp.mean(z, axis=-1, keepdims=True)
    dlt = z - mu
    var = jnp.mean(dlt * dlt, axis=-1, keepdims=True)
    return dlt * lax.rsqrt(var + LN_EPS) * gain + bias


def _ffn_kernel(x_ref, yr_ref, ya_ref, p_ref, wo_ref, g1_ref, b1_ref, wg_ref, wu_ref, wd_ref,
                wpe_ref, wpg_ref, g2_ref, b2_ref, o_ref, *, alpha):
    tm = x_ref.shape[0]
    rows = [pl.ds(k * (tm // _FFN_SUB_TILES), tm // _FFN_SUB_TILES) for k in range(_FFN_SUB_TILES)]
    mix = [_dot(yr_ref[r, :], wo_ref[0:RET_WIDTH, :]) + _dot(ya_ref[r, :], wo_ref[RET_WIDTH:, :])
           for r in rows]
    h = [_layer_norm(alpha * x_ref[r, :] + m, g1_ref[...], b1_ref[...]) for r, m in zip(rows, mix)]
    z = []
    for r, hk in zip(rows, h):
        hb = hk.astype(BF16)
        g = _dot(hb, wg_ref[...])
        up = _dot(hb, wu_ref[...])
        act = (g * jax.nn.sigmoid(g) * up).astype(BF16)
        ffn = _dot(act, wd_ref[...])
        ple = _dot(p_ref[r, :].astype(BF16), wpe_ref[...])
        ple_gate = jax.nn.sigmoid(_dot(hb, wpg_ref[...]))
        z.append(alpha * hk + ffn + ple_gate * ple)
    for r, zk in zip(rows, z):
        o_ref[r, :] = _layer_norm(zk, g2_ref[...], b2_ref[...])


def _ffn(x, yr, ya, p, wo, g1, b1, wg, wu, wd, wpe, wpg, g2, b2, *, alpha, tm):
    t, d = x.shape

    def rows(width):
        return pl.BlockSpec((tm, width), lambda i: (i, 0))

    def resident(arr):
        return pl.BlockSpec(arr.shape, lambda i: (0, 0), pipeline_mode=pl.Buffered(1))

    return pl.pallas_call(
        functools.partial(_ffn_kernel, alpha=alpha),
        grid=(t // tm,),
        in_specs=[rows(d), rows(RET_WIDTH), rows(ATTN_WIDTH), rows(PLE_DIM),
                  resident(wo), resident(g1), resident(b1), resident(wg), resident(wu),
                  resident(wd), resident(wpe), resident(wpg), resident(g2), resident(b2)],
        out_specs=rows(d),
        out_shape=jax.ShapeDtypeStruct((t, d), F32),
        compiler_params=pltpu.CompilerParams(
            dimension_semantics=("arbitrary",), vmem_limit_bytes=VMEM_LIMIT_BYTES),
        name="ffn",
    )(x, yr, ya, p, wo, g1, b1, wg, wu, wd, wpe, wpg, g2, b2)


def _stacked_head_cols(per_head):
    v = per_head.astype(F32).reshape(ATTN_KV_HEADS, 2, 2)
    v = jnp.transpose(v, (0, 2, 1))
    v = jnp.repeat(v, BLOCK, axis=2)
    return jnp.broadcast_to(v[:, :, None, :], v.shape[:2] + (_SUBLANES, _STACK))


def kernel(x, p, w_in, ret_decay_fwd, ret_decay_bwd, ret_gn_gain, attn_sink, w_out, ln1_gain, ln1_bias,
           w_ffn_gate, w_ffn_up, w_ffn_down, w_ple_proj, w_ple_gate, ln2_gain, ln2_bias):
    depth = w_in.shape[0]
    b, s, d = x.shape
    alpha = (2.0 * depth) ** 0.25
    slopes = 2.0 ** (-8.0 * (jnp.arange(ATTN_HEADS, dtype=F32) + 1.0) / ATTN_HEADS)
    slope_cols = _stacked_head_cols(slopes)

    def lane_rep(v):
        v = v.astype(F32).reshape(RET_PAIRS, HEADS_PER_LANE_BLOCK, 1)
        return jnp.broadcast_to(v, (RET_PAIRS, HEADS_PER_LANE_BLOCK, LANES))

    row = lambda v: v.astype(F32).reshape(1, -1)
    h = x.astype(F32)
    for i in range(depth):
        (ret, aq, akv), (wo, wg, wu, wd, wpe, wpg) = _in_proj(
            h, w_in, i, (w_out, w_ffn_gate, w_ffn_up, w_ffn_down, w_ple_proj, w_ple_gate), tm=IN_PROJ_TILE)
        y_ret = _retention(ret, lane_rep(ret_decay_fwd[i]), lane_rep(ret_decay_bwd[i]),
                           ret_gn_gain[i].astype(F32).reshape(RET_PAIRS, 1, LANES))
        y_att = _attention(aq, akv, slope_cols, _stacked_head_cols(attn_sink[i]))
        h = _ffn(h.reshape(b * s, d), y_ret.reshape(b * s, RET_WIDTH), y_att.reshape(b * s, ATTN_WIDTH),
                 p[i].reshape(b * s, PLE_DIM),
                 wo, row(ln1_gain[i]), row(ln1_bias[i]), wg, wu, wd, wpe, wpg,
                 row(ln2_gain[i]), row(ln2_bias[i]), alpha=alpha, tm=FFN_TILE).reshape(b, s, d)
    return h.astype(x.dtype)
```

```python
import functools

import jax
import jax.numpy as jnp
from jax import lax
from jax.experimental import pallas as pl
from jax.experimental.pallas import tpu as pltpu

D_MODEL = 1024
HEAD_DIM = 64
RET_HEADS = 8
ATTN_HEADS = 8
ATTN_KV_HEADS = 2
ATTN_GROUP = ATTN_HEADS // ATTN_KV_HEADS
RET_WIDTH = RET_HEADS * HEAD_DIM
ATTN_WIDTH = ATTN_HEADS * HEAD_DIM
KV_WIDTH = ATTN_KV_HEADS * HEAD_DIM
IN_PROJ_WIDTH = 4 * RET_WIDTH + ATTN_WIDTH + 2 * KV_WIDTH
CHUNK = 128
WINDOW = 128
BLOCK = 128
PLE_DIM = 256
FFN_HIDDEN = ((8 * D_MODEL + 767) // 768) * 256
LN_EPS = 1e-5
GN_EPS = 1e-5
NEG_INF = -1e30
QK_SCALE = HEAD_DIM ** -0.5

LANES = 128
HEADS_PER_LANE_BLOCK = LANES // HEAD_DIM
RET_PAIRS = RET_HEADS // HEADS_PER_LANE_BLOCK
VMEM_LIMIT_BYTES = 56 * 1024 * 1024
IN_PROJ_TILE = 1024
FFN_TILE = 512

F32 = jnp.float32
BF16 = jnp.bfloat16


def _dot(a, b):
    return jnp.dot(a, b, preferred_element_type=F32)


def _dot_nt(a, b):
    return lax.dot_general(a, b, (((1,), (1,)), ((), ())), preferred_element_type=F32)


def _dot_tn(a, b):
    return lax.dot_general(a, b, (((0,), (0,)), ((), ())), preferred_element_type=F32)


_IN_PROJ_COLS = 256
_RET_BLOCKS = 4 * RET_PAIRS
_KV_COL0 = 4 * RET_WIDTH + ATTN_WIDTH
_BF16_ROWS = 16


def _cast_rows_per_step(n_rows, n_steps):
    for rows in range(_BF16_ROWS, n_rows + 1, _BF16_ROWS):
        if n_rows % rows == 0 and n_rows // rows <= n_steps:
            return rows
    raise ValueError(f"no bf16-tile row split of {n_rows} rows over {n_steps} steps")


def _in_proj_kernel(x_ref, w_ref, *refs, n_cast):
    cast_in, (ret_ref, aq_ref, akv_ref) = refs[:n_cast], refs[n_cast:n_cast + 3]
    cast_out, wb_ref = refs[n_cast + 3:2 * n_cast + 3], refs[2 * n_cast + 3]

    @pl.when((pl.program_id(0) == 0) & (pl.program_id(1) == 0))
    def _():
        wb_ref[:, 0:_KV_COL0] = w_ref[:, 0:_KV_COL0].astype(BF16)
        low = lax.broadcasted_iota(jnp.int32, (1, LANES), 1) < HEAD_DIM
        ak = w_ref[:, _KV_COL0:_KV_COL0 + KV_WIDTH]
        av = w_ref[:, _KV_COL0 + KV_WIDTH:_KV_COL0 + 2 * KV_WIDTH]
        wb_ref[:, _KV_COL0:_KV_COL0 + LANES] = jnp.where(
            low, ak, pltpu.roll(av, HEAD_DIM, 1)).astype(BF16)
        wb_ref[:, _KV_COL0 + LANES:_KV_COL0 + 2 * LANES] = jnp.where(
            low, pltpu.roll(ak, HEAD_DIM, 1), av).astype(BF16)

    for src, dst in zip(cast_in, cast_out):
        dst[...] = src[...].astype(BF16)

    xb = x_ref[0].astype(BF16)
    for c in range(IN_PROJ_WIDTH // _IN_PROJ_COLS):
        u = _dot(xb, wb_ref[:, c * _IN_PROJ_COLS:(c + 1) * _IN_PROJ_COLS]).astype(BF16)
        if 2 * c < _RET_BLOCKS:
            ret_ref[0, 2 * c] = u[:, :LANES]
            ret_ref[0, 2 * c + 1] = u[:, LANES:]
        elif c < _RET_BLOCKS // 2 + ATTN_KV_HEADS:
            aq_ref[0, c - _RET_BLOCKS // 2] = u
        else:
            akv_ref[0, 0] = u[:, :LANES]
            akv_ref[0, 1] = u[:, LANES:]


def _in_proj(x, w_in, layer, cast_weights, tm):
    b, s, d = x.shape
    n_j = s // tm
    n_steps = b * n_j

    def cast_spec(w):
        rows = _cast_rows_per_step(w.shape[1], n_steps)
        last = w.shape[1] // rows - 1
        return rows, lambda i, j: (layer, jnp.minimum(i * n_j + j, last), 0)

    cast_in_specs, cast_out_specs, cast_out_shapes = [], [], []
    for w in cast_weights:
        rows, imap = cast_spec(w)
        cast_in_specs.append(pl.BlockSpec((None, rows, w.shape[2]), imap))
        cast_out_specs.append(pl.BlockSpec((rows, w.shape[2]), lambda i, j, imap=imap: imap(i, j)[1:]))
        cast_out_shapes.append(jax.ShapeDtypeStruct(w.shape[1:], BF16))

    outs = pl.pallas_call(
        functools.partial(_in_proj_kernel, n_cast=len(cast_weights)),
        grid=(b, n_j),
        in_specs=[
            pl.BlockSpec((1, tm, d), lambda i, j: (i, j, 0)),
            pl.BlockSpec((None, d, IN_PROJ_WIDTH), lambda i, j: (layer, 0, 0),
                         pipeline_mode=pl.Buffered(1)),
        ] + cast_in_specs,
        out_specs=[
            pl.BlockSpec((1, _RET_BLOCKS, tm, LANES), lambda i, j: (i, 0, j, 0)),
            pl.BlockSpec((1, ATTN_KV_HEADS, tm, ATTN_GROUP * HEAD_DIM), lambda i, j: (i, 0, j, 0)),
            pl.BlockSpec((1, ATTN_KV_HEADS, tm, LANES), lambda i, j: (i, 0, j, 0)),
        ] + cast_out_specs,
        out_shape=[
            jax.ShapeDtypeStruct((b, _RET_BLOCKS, s, LANES), BF16),
            jax.ShapeDtypeStruct((b, ATTN_KV_HEADS, s, ATTN_GROUP * HEAD_DIM), BF16),
            jax.ShapeDtypeStruct((b, ATTN_KV_HEADS, s, LANES), BF16),
        ] + cast_out_shapes,
        scratch_shapes=[pltpu.VMEM((d, IN_PROJ_WIDTH), BF16)],
        compiler_params=pltpu.CompilerParams(
            dimension_semantics=("arbitrary", "arbitrary"), vmem_limit_bytes=VMEM_LIMIT_BYTES),
        name="in_proj",
    )(x, w_in, *cast_weights)
    return outs[:3], outs[3:]


def _ret_body(q_ref, k_ref, v_ref, g_ref, decf_ref, decb_ref, gain_ref, o_ref, kv_ref, st_ref,
              p_ref, y_ref, *, seq, pp):
    n_chunks = seq // CHUNK
    kv_ref, st_ref, p_ref, y_ref = kv_ref.at[pp], st_ref.at[pp], p_ref.at[pp], y_ref.at[pp]
    out_lanes = pl.ds(pp * LANES, LANES)
    lane = lax.broadcasted_iota(jnp.int32, (1, LANES), 1)
    head0 = lane < HEAD_DIM

    def log_gamma(dec):
        return jnp.log1p(-jnp.exp2(dec))

    lgf0, lgf1 = log_gamma(decf_ref[pp, 0:1, :]), log_gamma(decf_ref[pp, 1:2, :])
    lgb0, lgb1 = log_gamma(decb_ref[pp, 0:1, :]), log_gamma(decb_ref[pp, 1:2, :])
    lgf = jnp.where(head0, lgf0, lgf1)
    lgb = jnp.where(head0, lgb0, lgb1)

    idx = lax.broadcasted_iota(jnp.int32, (CHUNK, LANES), 0).astype(F32)
    qdf = jnp.exp(lgf * (idx + 1.0))
    kdf = jnp.exp(lgf * (CHUNK - 1.0 - idx)) * QK_SCALE
    qdb = jnp.exp(lgb * (CHUNK - idx))
    kdb = jnp.exp(lgb * idx) * QK_SCALE
    cdf = jnp.exp(lgf * CHUNK)
    cdb = jnp.exp(lgb * CHUNK)

    ri = lax.broadcasted_iota(jnp.int32, (CHUNK, CHUNK), 0)
    ci = lax.broadcasted_iota(jnp.int32, (CHUNK, CHUNK), 1)
    diff = (ri - ci).astype(F32)

    def intra_mask(lf, lb):
        fwd = jnp.exp(lf * jnp.maximum(diff, 0.0))
        bwd = jnp.exp(lb * jnp.maximum(-diff, 0.0))
        return jnp.where(diff > 0, fwd, jnp.where(diff < 0, bwd, fwd + bwd))

    mask2 = jnp.concatenate([intra_mask(lgf0, lgb0), intra_mask(lgf1, lgb1)], axis=0) * QK_SCALE
    same_head = (ri < HEAD_DIM) == (ci < HEAD_DIM)
    gn_avg = jnp.where(same_head, 1.0 / HEAD_DIM, 0.0).astype(BF16)

    for n in range(n_chunks):
        sl = pl.ds(n * CHUNK, CHUNK)
        kc = k_ref[0, pp, sl, :].astype(F32)
        kd = jnp.concatenate([(kc * kdf).astype(BF16), (kc * kdb).astype(BF16)], axis=1)
        kv = _dot_tn(kd, v_ref[0, pp, sl, :])
        kv_ref[n] = kv
        yield

    zero_state = jnp.zeros((LANES, LANES), BF16)
    state = jnp.zeros((LANES, LANES), F32)
    for n in range(n_chunks):
        st_ref[n, 0:LANES, :] = jnp.where(same_head, state.astype(BF16), zero_state)
        state = state * cdf + kv_ref[n, 0:LANES, :]
    state = jnp.zeros((LANES, LANES), F32)
    for n in reversed(range(n_chunks)):
        st_ref[n, LANES:2 * LANES, :] = jnp.where(same_head, state.astype(BF16), zero_state)
        state = state * cdb + kv_ref[n, LANES:2 * LANES, :]

    zero = jnp.zeros((CHUNK, LANES), BF16)
    for n in range(n_chunks):
        sl = pl.ds(n * CHUNK, CHUNK)
        qb = q_ref[0, pp, sl, :]
        kb = k_ref[0, pp, sl, :]
        qs = jnp.concatenate([jnp.where(head0, qb, zero), jnp.where(head0, zero, qb)], axis=0)
        p = (_dot_nt(qs, kb) * mask2).astype(BF16)
        p_ref[n, :, 0:CHUNK] = p[0:CHUNK]
        p_ref[n, :, CHUNK:2 * CHUNK] = p[CHUNK:2 * CHUNK]
        yield

    for n in range(n_chunks):
        sl = pl.ds(n * CHUNK, CHUNK)
        vb = v_ref[0, pp, sl, :]
        qc = q_ref[0, pp, sl, :].astype(F32)
        lhs = jnp.concatenate([p_ref[n], (qc * qdf).astype(BF16), (qc * qdb).astype(BF16)], axis=1)
        rhs = jnp.concatenate([jnp.where(head0, vb, zero), jnp.where(head0, zero, vb), st_ref[n]],
                              axis=0)
        y_ref[sl, :] = _dot(lhs, rhs)
        yield

    def group_mean(z):
        return _dot(z.astype(BF16), gn_avg)

    for n in range(n_chunks):
        sl = pl.ds(n * CHUNK, CHUNK)
        y = y_ref[sl, :]
        y_ref[sl, :] = y - group_mean(y)
        yield

    gain = gain_ref[pp]
    for n in range(n_chunks):
        sl = pl.ds(n * CHUNK, CHUNK)
        dlt = y_ref[sl, :]
        var = group_mean(dlt * dlt)
        gate = g_ref[0, pp, sl, :].astype(F32)
        o_ref[0, sl, out_lanes] = (dlt * lax.rsqrt(var + GN_EPS) * gain
                                   * (gate * jax.nn.sigmoid(gate))).astype(BF16)
        yield


_RET_PAIRS_PER_STEP = 2


def _ret_kernel(*refs, seq):
    bodies = [_ret_body(*refs, seq=seq, pp=pp) for pp in range(_RET_PAIRS_PER_STEP)]
    alive = list(bodies)
    while alive:
        for gen in list(alive):
            try:
                next(gen)
            except StopIteration:
                alive.remove(gen)


def _retention(ret, decf, decb, gain):
    b, _, s, _ = ret.shape
    n_chunks = s // CHUNK
    pps = _RET_PAIRS_PER_STEP
    steps = RET_PAIRS // pps

    def part(kind):
        return pl.BlockSpec((1, pps, s, LANES), lambda i, j, kind=kind: (i, kind * steps + j, 0, 0))

    return pl.pallas_call(
        functools.partial(_ret_kernel, seq=s),
        grid=(b, steps),
        in_specs=[
            part(0), part(1), part(2), part(3),
            pl.BlockSpec((pps, HEADS_PER_LANE_BLOCK, LANES), lambda i, j: (j, 0, 0)),
            pl.BlockSpec((pps, HEADS_PER_LANE_BLOCK, LANES), lambda i, j: (j, 0, 0)),
            pl.BlockSpec((pps, 1, LANES), lambda i, j: (j, 0, 0)),
        ],
        out_specs=pl.BlockSpec((1, s, pps * LANES), lambda i, j: (i, 0, j)),
        out_shape=jax.ShapeDtypeStruct((b, s, RET_WIDTH), BF16),
        scratch_shapes=[
            pltpu.VMEM((pps, n_chunks, 2 * LANES, LANES), F32),
            pltpu.VMEM((pps, n_chunks, 2 * LANES, LANES), BF16),
            pltpu.VMEM((pps, n_chunks, CHUNK, 2 * CHUNK), BF16),
            pltpu.VMEM((pps, s, LANES), F32),
        ],
        compiler_params=pltpu.CompilerParams(
            dimension_semantics=("arbitrary", "arbitrary"), vmem_limit_bytes=VMEM_LIMIT_BYTES),
        name="retention",
    )(ret, ret, ret, ret, decf, decb, gain)


_STACK = 2 * BLOCK
_KEYS = 3 * BLOCK
_SUBLANES = 8
_LOG2E = 1.4426950408889634
_ATTN_STAGE_LAG = (0, 2, 4, 5)


def _attn_kernel(q_ref, kv_ref, slope_ref, sink_ref, o_ref, ks_ref, kss_ref, vt_ref, bias_ref,
                 *, seq):
    n_blocks = seq // BLOCK
    lane = lax.broadcasted_iota(jnp.int32, (1, LANES), 1)
    low = lane < HEAD_DIM

    for hk in range(ATTN_KV_HEADS):
        kv = kv_ref[0, hk].astype(F32)
        ks = kv * (QK_SCALE * _LOG2E)
        ks_ref[hk] = ks.astype(BF16)
        kss_ref[hk] = pltpu.roll(ks, HEAD_DIM, 1).astype(BF16)
        kv_t = jnp.transpose(kv)
        vt_ref[hk] = jnp.concatenate(
            [kv_t[HEAD_DIM:], jnp.ones((_BF16_ROWS, seq), F32)], axis=0).astype(BF16)

    kj = lax.broadcasted_iota(jnp.int32, (_KEYS, _STACK), 0)
    c = lax.broadcasted_iota(jnp.int32, (_KEYS, _STACK), 1)
    qi = jnp.where(c < BLOCK, c, c - BLOCK)
    dist = jnp.abs(kj - BLOCK - qi)
    for hk in range(ATTN_KV_HEADS):
        for par in range(2):
            slope = slope_ref[hk, par, 0:1, :]
            bias_ref[hk, par] = jnp.where(dist <= WINDOW, -slope * dist.astype(F32), NEG_INF) * _LOG2E

    zero = jnp.zeros((BLOCK, LANES), BF16)

    def key_range(n):
        lo_blk, hi_blk = max(n - 1, 0), min(n + 2, n_blocks)
        return pl.ds(lo_blk * BLOCK, (hi_blk - lo_blk) * BLOCK), (lo_blk - (n - 1)) * BLOCK, \
            (hi_blk - lo_blk) * BLOCK

    units = [(hk, n, par) for hk in range(ATTN_KV_HEADS) for n in range(n_blocks) for par in range(2)]
    live = [dict() for _ in units]

    def stage_scores(u):
        hk, n, par = units[u]
        keys, c0, width = key_range(n)
        qblk = q_ref[0, hk, pl.ds(n * BLOCK, BLOCK), :]
        qa, qb = qblk[:, :LANES], qblk[:, LANES:]
        if par == 0:
            qs = jnp.concatenate([jnp.where(low, qa, zero), jnp.where(low, qb, zero)], axis=0)
            kmat = ks_ref[hk, keys, :]
        else:
            qs = jnp.concatenate([jnp.where(low, zero, qa), jnp.where(low, zero, qb)], axis=0)
            kmat = kss_ref[hk, keys, :]
        live[u]["s"] = _dot_nt(kmat, qs) + bias_ref[hk, par, c0:c0 + width, :]

    def stage_softmax(u):
        hk, _, par = units[u]
        st = live[u]
        s = st.pop("s")
        sink = sink_ref[hk, par, 0:1, :] * _LOG2E
        m = jnp.maximum(jnp.max(s, axis=0, keepdims=True), sink)
        st["e"] = jnp.exp2(s - m).astype(BF16)
        st["esink"] = jnp.exp2(sink - m)

    def stage_values(u):
        hk, n, _ = units[u]
        keys, _, _ = key_range(n)
        live[u]["pv"] = _dot(vt_ref[hk, :, keys], live[u].pop("e"))

    def stage_out(u):
        hk, n, par = units[u]
        st = live[u]
        pv = st.pop("pv")
        st["out"] = pv[:HEAD_DIM] / (pv[HEAD_DIM:HEAD_DIM + 1] + st.pop("esink"))
        if par == 1:
            even, odd = live[u - 1].pop("out"), st.pop("out")
            y_t = jnp.concatenate(
                [even[:, :BLOCK], odd[:, :BLOCK], even[:, BLOCK:], odd[:, BLOCK:]], axis=0)
            o_ref[0, pl.ds(n * BLOCK, BLOCK), pl.ds(hk * _STACK, _STACK)] = \
                jnp.transpose(y_t).astype(BF16)

    stages = (stage_scores, stage_softmax, stage_values, stage_out)
    for i in range(len(units) + _ATTN_STAGE_LAG[-1]):
        for lag, stage in zip(_ATTN_STAGE_LAG, stages):
            if 0 <= i - lag < len(units):
                stage(i - lag)


def _attention(aq, akv, slope_cols, sink_cols):
    b, _, s, qw = aq.shape
    return pl.pallas_call(
        functools.partial(_attn_kernel, seq=s),
        grid=(b,),
        in_specs=[
            pl.BlockSpec((1, ATTN_KV_HEADS, s, qw), lambda i: (i, 0, 0, 0)),
            pl.BlockSpec((1, ATTN_KV_HEADS, s, LANES), lambda i: (i, 0, 0, 0)),
            pl.BlockSpec((ATTN_KV_HEADS, 2, _SUBLANES, _STACK), lambda i: (0, 0, 0, 0)),
            pl.BlockSpec((ATTN_KV_HEADS, 2, _SUBLANES, _STACK), lambda i: (0, 0, 0, 0)),
        ],
        out_specs=pl.BlockSpec((1, s, ATTN_WIDTH), lambda i: (i, 0, 0)),
        out_shape=jax.ShapeDtypeStruct((b, s, ATTN_WIDTH), BF16),
        scratch_shapes=[
            pltpu.VMEM((ATTN_KV_HEADS, s, LANES), BF16),
            pltpu.VMEM((ATTN_KV_HEADS, s, LANES), BF16),
            pltpu.VMEM((ATTN_KV_HEADS, HEAD_DIM + _BF16_ROWS, s), BF16),
            pltpu.VMEM((ATTN_KV_HEADS, 2, _KEYS, _STACK), F32),
        ],
        compiler_params=pltpu.CompilerParams(
            dimension_semantics=("arbitrary",), vmem_limit_bytes=VMEM_LIMIT_BYTES),
        name="attention",
    )(aq, akv, slope_cols, sink_cols)


_FFN_SUB_TILES = 2


def _layer_norm(z, gain, bias):
    mu = jnp.mean(z, axis=-1, keepdims=True)
    dlt = z - mu
    var = jnp.mean(dlt * dlt, axis=-1, keepdims=True)
    return dlt * lax.rsqrt(var + LN_EPS) * gain + bias


def _ffn_kernel(x_ref, yr_ref, ya_ref, p_ref, wo_ref, g1_ref, b1_ref, wg_ref, wu_ref, wd_ref,
                wpe_ref, wpg_ref, g2_ref, b2_ref, o_ref, *, alpha):
    tm = x_ref.shape[0]
    rows = [pl.ds(k * (tm // _FFN_SUB_TILES), tm // _FFN_SUB_TILES) for k in range(_FFN_SUB_TILES)]
    mix = [_dot(yr_ref[r, :], wo_ref[0:RET_WIDTH, :]) + _dot(ya_ref[r, :], wo_ref[RET_WIDTH:, :])
           for r in rows]
    h = [_layer_norm(alpha * x_ref[r, :] + m, g1_ref[...], b1_ref[...]) for r, m in zip(rows, mix)]
    z = []
    for r, hk in zip(rows, h):
        hb = hk.astype(BF16)
        g = _dot(hb, wg_ref[...])
        up = _dot(hb, wu_ref[...])
        act = (g * jax.nn.sigmoid(g) * up).astype(BF16)
        ffn = _dot(act, wd_ref[...])
        ple = _dot(p_ref[r, :].astype(BF16), wpe_ref[...])
        ple_gate = jax.nn.sigmoid(_dot(hb, wpg_ref[...]))
        z.append(alpha * hk + ffn + ple_gate * ple)
    for r, zk in zip(rows, z):
        o_ref[r, :] = _layer_norm(zk, g2_ref[...], b2_ref[...])


def _ffn(x, yr, ya, p, wo, g1, b1, wg, wu, wd, wpe, wpg, g2, b2, *, alpha, tm):
    t, d = x.shape

    def rows(width):
        return pl.BlockSpec((tm, width), lambda i: (i, 0))

    def resident(arr):
        return pl.BlockSpec(arr.shape, lambda i: (0, 0), pipeline_mode=pl.Buffered(1))

    return pl.pallas_call(
        functools.partial(_ffn_kernel, alpha=alpha),
        grid=(t // tm,),
        in_specs=[rows(d), rows(RET_WIDTH), rows(ATTN_WIDTH), rows(PLE_DIM),
                  resident(wo), resident(g1), resident(b1), resident(wg), resident(wu),
                  resident(wd), resident(wpe), resident(wpg), resident(g2), resident(b2)],
        out_specs=rows(d),
        out_shape=jax.ShapeDtypeStruct((t, d), F32),
        compiler_params=pltpu.CompilerParams(
            dimension_semantics=("arbitrary",), vmem_limit_bytes=VMEM_LIMIT_BYTES),
        name="ffn",
    )(x, yr, ya, p, wo, g1, b1, wg, wu, wd, wpe, wpg, g2, b2)


def _stacked_head_cols(per_head):
    v = per_head.astype(F32).reshape(ATTN_KV_HEADS, 2, 2)
    v = jnp.transpose(v, (0, 2, 1))
    v = jnp.repeat(v, BLOCK, axis=2)
    return jnp.broadcast_to(v[:, :, None, :], v.shape[:2] + (_SUBLANES, _STACK))


def kernel(x, p, w_in, ret_decay_fwd, ret_decay_bwd, ret_gn_gain, attn_sink, w_out, ln1_gain, ln1_bias,
           w_ffn_gate, w_ffn_up, w_ffn_down, w_ple_proj, w_ple_gate, ln2_gain, ln2_bias):
    depth = w_in.shape[0]
    b, s, d = x.shape
    alpha = (2.0 * depth) ** 0.25
    slopes = 2.0 ** (-8.0 * (jnp.arange(ATTN_HEADS, dtype=F32) + 1.0) / ATTN_HEADS)
    slope_cols = _stacked_head_cols(slopes)

    def lane_rep(v):
        v = v.astype(F32).reshape(RET_PAIRS, HEADS_PER_LANE_BLOCK, 1)
        return jnp.broadcast_to(v, (RET_PAIRS, HEADS_PER_LANE_BLOCK, LANES))

    row = lambda v: v.astype(F32).reshape(1, -1)
    h = x.astype(F32)
    for i in range(depth):
        (ret, aq, akv), (wo, wg, wu, wd, wpe, wpg) = _in_proj(
            h, w_in, i, (w_out, w_ffn_gate, w_ffn_up, w_ffn_down, w_ple_proj, w_ple_gate), tm=IN_PROJ_TILE)
        y_ret = _retention(ret, lane_rep(ret_decay_fwd[i]), lane_rep(ret_decay_bwd[i]),
                           ret_gn_gain[i].astype(F32).reshape(RET_PAIRS, 1, LANES))
        y_att = _attention(aq, akv, slope_cols, _stacked_head_cols(attn_sink[i]))
        h = _ffn(h.reshape(b * s, d), y_ret.reshape(b * s, RET_WIDTH), y_att.reshape(b * s, ATTN_WIDTH),
                 p[i].reshape(b * s, PLE_DIM),
                 wo, row(ln1_gain[i]), row(ln1_bias[i]), wg, wu, wd, wpe, wpg,
                 row(ln2_gain[i]), row(ln2_bias[i]), alpha=alpha, tm=FFN_TILE).reshape(b, s, d)
    return h.astype(x.dtype)
```

```python
import functools

import jax
import jax.numpy as jnp
from jax import lax
from jax.experimental import pallas as pl
from jax.experimental.pallas import tpu as pltpu

D_MODEL = 1024
HEAD_DIM = 64
RET_HEADS = 8
ATTN_HEADS = 8
ATTN_KV_HEADS = 2
ATTN_GROUP = ATTN_HEADS // ATTN_KV_HEADS
RET_WIDTH = RET_HEADS * HEAD_DIM
ATTN_WIDTH = ATTN_HEADS * HEAD_DIM
KV_WIDTH = ATTN_KV_HEADS * HEAD_DIM
IN_PROJ_WIDTH = 4 * RET_WIDTH + ATTN_WIDTH + 2 * KV_WIDTH
CHUNK = 128
WINDOW = 128
BLOCK = 128
PLE_DIM = 256
FFN_HIDDEN = ((8 * D_MODEL + 767) // 768) * 256
LN_EPS = 1e-5
GN_EPS = 1e-5
NEG_INF = -1e30
QK_SCALE = HEAD_DIM ** -0.5

LANES = 128
HEADS_PER_LANE_BLOCK = LANES // HEAD_DIM
RET_PAIRS = RET_HEADS // HEADS_PER_LANE_BLOCK
VMEM_LIMIT_BYTES = 60 * 1024 * 1024
IN_PROJ_TILE = 1024
FFN_TILE = 1024

F32 = jnp.float32
BF16 = jnp.bfloat16


def _dot(a, b):
    return jnp.dot(a, b, preferred_element_type=F32)


def _dot_nt(a, b):
    return lax.dot_general(a, b, (((1,), (1,)), ((), ())), preferred_element_type=F32)


def _dot_tn(a, b):
    return lax.dot_general(a, b, (((0,), (0,)), ((), ())), preferred_element_type=F32)


_IN_PROJ_COLS = 256
_RET_BLOCKS = 4 * RET_PAIRS
_KV_COL0 = 4 * RET_WIDTH + ATTN_WIDTH
_BF16_ROWS = 16


def _cast_rows_per_step(n_rows, n_steps):
    for rows in range(_BF16_ROWS, n_rows + 1, _BF16_ROWS):
        if n_rows % rows == 0 and n_rows // rows <= n_steps:
            return rows
    raise ValueError(f"no bf16-tile row split of {n_rows} rows over {n_steps} steps")


def _in_proj_kernel(x_ref, w_ref, *refs, n_cast):
    cast_in, (ret_ref, aq_ref, akv_ref) = refs[:n_cast], refs[n_cast:n_cast + 3]
    cast_out, wb_ref = refs[n_cast + 3:2 * n_cast + 3], refs[2 * n_cast + 3]

    @pl.when((pl.program_id(0) == 0) & (pl.program_id(1) == 0))
    def _():
        wb_ref[:, 0:_KV_COL0] = w_ref[:, 0:_KV_COL0].astype(BF16)
        low = lax.broadcasted_iota(jnp.int32, (1, LANES), 1) < HEAD_DIM
        ak = w_ref[:, _KV_COL0:_KV_COL0 + KV_WIDTH]
        av = w_ref[:, _KV_COL0 + KV_WIDTH:_KV_COL0 + 2 * KV_WIDTH]
        wb_ref[:, _KV_COL0:_KV_COL0 + LANES] = jnp.where(
            low, ak, pltpu.roll(av, HEAD_DIM, 1)).astype(BF16)
        wb_ref[:, _KV_COL0 + LANES:_KV_COL0 + 2 * LANES] = jnp.where(
            low, pltpu.roll(ak, HEAD_DIM, 1), av).astype(BF16)

    for src, dst in zip(cast_in, cast_out):
        dst[...] = src[...].astype(BF16)

    xb = x_ref[0].astype(BF16)
    for c in range(IN_PROJ_WIDTH // _IN_PROJ_COLS):
        u = _dot(xb, wb_ref[:, c * _IN_PROJ_COLS:(c + 1) * _IN_PROJ_COLS]).astype(BF16)
        if 2 * c < _RET_BLOCKS:
            ret_ref[0, 2 * c] = u[:, :LANES]
            ret_ref[0, 2 * c + 1] = u[:, LANES:]
        elif c < _RET_BLOCKS // 2 + ATTN_KV_HEADS:
            aq_ref[0, c - _RET_BLOCKS // 2] = u
        else:
            akv_ref[0, 0] = u[:, :LANES]
            akv_ref[0, 1] = u[:, LANES:]


def _in_proj(x, w_in, layer, cast_weights, tm):
    b, s, d = x.shape
    n_j = s // tm
    n_steps = b * n_j

    def cast_spec(w):
        rows = _cast_rows_per_step(w.shape[1], n_steps)
        last = w.shape[1] // rows - 1
        return rows, lambda i, j: (layer, jnp.minimum(i * n_j + j, last), 0)

    cast_in_specs, cast_out_specs, cast_out_shapes = [], [], []
    for w in cast_weights:
        rows, imap = cast_spec(w)
        cast_in_specs.append(pl.BlockSpec((None, rows, w.shape[2]), imap))
        cast_out_specs.append(pl.BlockSpec((rows, w.shape[2]), lambda i, j, imap=imap: imap(i, j)[1:]))
        cast_out_shapes.append(jax.ShapeDtypeStruct(w.shape[1:], BF16))

    outs = pl.pallas_call(
        functools.partial(_in_proj_kernel, n_cast=len(cast_weights)),
        grid=(b, n_j),
        in_specs=[
            pl.BlockSpec((1, tm, d), lambda i, j: (i, j, 0)),
            pl.BlockSpec((None, d, IN_PROJ_WIDTH), lambda i, j: (layer, 0, 0),
                         pipeline_mode=pl.Buffered(1)),
        ] + cast_in_specs,
        out_specs=[
            pl.BlockSpec((1, _RET_BLOCKS, tm, LANES), lambda i, j: (i, 0, j, 0)),
            pl.BlockSpec((1, ATTN_KV_HEADS, tm, ATTN_GROUP * HEAD_DIM), lambda i, j: (i, 0, j, 0)),
            pl.BlockSpec((1, ATTN_KV_HEADS, tm, LANES), lambda i, j: (i, 0, j, 0)),
        ] + cast_out_specs,
        out_shape=[
            jax.ShapeDtypeStruct((b, _RET_BLOCKS, s, LANES), BF16),
            jax.ShapeDtypeStruct((b, ATTN_KV_HEADS, s, ATTN_GROUP * HEAD_DIM), BF16),
            jax.ShapeDtypeStruct((b, ATTN_KV_HEADS, s, LANES), BF16),
        ] + cast_out_shapes,
        scratch_shapes=[pltpu.VMEM((d, IN_PROJ_WIDTH), BF16)],
        compiler_params=pltpu.CompilerParams(
            dimension_semantics=("arbitrary", "arbitrary"), vmem_limit_bytes=VMEM_LIMIT_BYTES),
        name="in_proj",
    )(x, w_in, *cast_weights)
    return outs[:3], outs[3:]


def _ret_body(q_ref, k_ref, v_ref, g_ref, decf_ref, decb_ref, gain_ref, o_ref, kv_ref, st_ref,
              p_ref, y_ref, *, seq, pp):
    n_chunks = seq // CHUNK
    kv_ref, st_ref, p_ref, y_ref = kv_ref.at[pp], st_ref.at[pp], p_ref.at[pp], y_ref.at[pp]
    out_lanes = pl.ds(pp * LANES, LANES)
    lane = lax.broadcasted_iota(jnp.int32, (1, LANES), 1)
    head0 = lane < HEAD_DIM

    def log_gamma(dec):
        return jnp.log1p(-jnp.exp2(dec))

    lgf0, lgf1 = log_gamma(decf_ref[pp, 0:1, :]), log_gamma(decf_ref[pp, 1:2, :])
    lgb0, lgb1 = log_gamma(decb_ref[pp, 0:1, :]), log_gamma(decb_ref[pp, 1:2, :])
    lgf = jnp.where(head0, lgf0, lgf1)
    lgb = jnp.where(head0, lgb0, lgb1)

    idx = lax.broadcasted_iota(jnp.int32, (CHUNK, LANES), 0).astype(F32)
    qdf = jnp.exp(lgf * (idx + 1.0))
    kdf = jnp.exp(lgf * (CHUNK - 1.0 - idx)) * QK_SCALE
    qdb = jnp.exp(lgb * (CHUNK - idx))
    kdb = jnp.exp(lgb * idx) * QK_SCALE
    cdf = jnp.exp(lgf * CHUNK)
    cdb = jnp.exp(lgb * CHUNK)

    ri = lax.broadcasted_iota(jnp.int32, (CHUNK, CHUNK), 0)
    ci = lax.broadcasted_iota(jnp.int32, (CHUNK, CHUNK), 1)
    diff = (ri - ci).astype(F32)

    def intra_mask(lf, lb):
        fwd = jnp.exp(lf * jnp.maximum(diff, 0.0))
        bwd = jnp.exp(lb * jnp.maximum(-diff, 0.0))
        return jnp.where(diff > 0, fwd, jnp.where(diff < 0, bwd, fwd + bwd))

    mask2 = jnp.concatenate([intra_mask(lgf0, lgb0), intra_mask(lgf1, lgb1)], axis=0) * QK_SCALE
    same_head = (ri < HEAD_DIM) == (ci < HEAD_DIM)
    gn_avg = jnp.where(same_head, 1.0 / HEAD_DIM, 0.0).astype(BF16)

    for n in range(n_chunks):
        sl = pl.ds(n * CHUNK, CHUNK)
        kc = k_ref[0, pp, sl, :].astype(F32)
        kd = jnp.concatenate([(kc * kdf).astype(BF16), (kc * kdb).astype(BF16)], axis=1)
        kv = _dot_tn(kd, v_ref[0, pp, sl, :])
        kv_ref[n] = kv
        yield

    zero_state = jnp.zeros((LANES, LANES), BF16)
    state = jnp.zeros((LANES, LANES), F32)
    for n in range(n_chunks):
        st_ref[n, 0:LANES, :] = jnp.where(same_head, state.astype(BF16), zero_state)
        state = state * cdf + kv_ref[n, 0:LANES, :]
    state = jnp.zeros((LANES, LANES), F32)
    for n in reversed(range(n_chunks)):
        st_ref[n, LANES:2 * LANES, :] = jnp.where(same_head, state.astype(BF16), zero_state)
        state = state * cdb + kv_ref[n, LANES:2 * LANES, :]

    zero = jnp.zeros((CHUNK, LANES), BF16)
    for n in range(n_chunks):
        sl = pl.ds(n * CHUNK, CHUNK)
        qb = q_ref[0, pp, sl, :]
        kb = k_ref[0, pp, sl, :]
        qs = jnp.concatenate([jnp.where(head0, qb, zero), jnp.where(head0, zero, qb)], axis=0)
        p = (_dot_nt(qs, kb) * mask2).astype(BF16)
        p_ref[n, :, 0:CHUNK] = p[0:CHUNK]
        p_ref[n, :, CHUNK:2 * CHUNK] = p[CHUNK:2 * CHUNK]
        yield

    for n in range(n_chunks):
        sl = pl.ds(n * CHUNK, CHUNK)
        vb = v_ref[0, pp, sl, :]
        qc = q_ref[0, pp, sl, :].astype(F32)
        lhs = jnp.concatenate([p_ref[n], (qc * qdf).astype(BF16), (qc * qdb).astype(BF16)], axis=1)
        rhs = jnp.concatenate([jnp.where(head0, vb, zero), jnp.where(head0, zero, vb), st_ref[n]],
                              axis=0)
        y_ref[sl, :] = _dot(lhs, rhs)
        yield

    def group_mean(z):
        return _dot(z.astype(BF16), gn_avg)

    for n in range(n_chunks):
        sl = pl.ds(n * CHUNK, CHUNK)
        y = y_ref[sl, :]
        y_ref[sl, :] = y - group_mean(y)
        yield

    gain = gain_ref[pp]
    for n in range(n_chunks):
        sl = pl.ds(n * CHUNK, CHUNK)
        dlt = y_ref[sl, :]
        var = group_mean(dlt * dlt)
        gate = g_ref[0, pp, sl, :].astype(F32)
        o_ref[0, sl, out_lanes] = (dlt * lax.rsqrt(var + GN_EPS) * gain
                                   * (gate * jax.nn.sigmoid(gate))).astype(BF16)
        yield


_RET_PAIRS_PER_STEP = 2


def _ret_kernel(*refs, seq):
    bodies = [_ret_body(*refs, seq=seq, pp=pp) for pp in range(_RET_PAIRS_PER_STEP)]
    alive = list(bodies)
    while alive:
        for gen in list(alive):
            try:
                next(gen)
            except StopIteration:
                alive.remove(gen)


def _retention(ret, decf, decb, gain):
    b, _, s, _ = ret.shape
    n_chunks = s // CHUNK
    pps = _RET_PAIRS_PER_STEP
    steps = RET_PAIRS // pps

    def part(kind):
        return pl.BlockSpec((1, pps, s, LANES), lambda i, j, kind=kind: (i, kind * steps + j, 0, 0))

    return pl.pallas_call(
        functools.partial(_ret_kernel, seq=s),
        grid=(b, steps),
        in_specs=[
            part(0), part(1), part(2), part(3),
            pl.BlockSpec((pps, HEADS_PER_LANE_BLOCK, LANES), lambda i, j: (j, 0, 0)),
            pl.BlockSpec((pps, HEADS_PER_LANE_BLOCK, LANES), lambda i, j: (j, 0, 0)),
            pl.BlockSpec((pps, 1, LANES), lambda i, j: (j, 0, 0)),
        ],
        out_specs=pl.BlockSpec((1, s, pps * LANES), lambda i, j: (i, 0, j)),
        out_shape=jax.ShapeDtypeStruct((b, s, RET_WIDTH), BF16),
        scratch_shapes=[
            pltpu.VMEM((pps, n_chunks, 2 * LANES, LANES), F32),
            pltpu.VMEM((pps, n_chunks, 2 * LANES, LANES), BF16),
            pltpu.VMEM((pps, n_chunks, CHUNK, 2 * CHUNK), BF16),
            pltpu.VMEM((pps, s, LANES), F32),
        ],
        compiler_params=pltpu.CompilerParams(
            dimension_semantics=("arbitrary", "arbitrary"), vmem_limit_bytes=VMEM_LIMIT_BYTES),
        name="retention",
    )(ret, ret, ret, ret, decf, decb, gain)


_STACK = 2 * BLOCK
_KEYS = 3 * BLOCK
_SUBLANES = 8
_LOG2E = 1.4426950408889634
_ATTN_STAGE_LAG = (0, 2, 4, 5)


def _attn_kernel(q_ref, kv_ref, slope_ref, sink_ref, o_ref, ks_ref, kss_ref, vt_ref, bias_ref,
                 *, seq):
    n_blocks = seq // BLOCK
    lane = lax.broadcasted_iota(jnp.int32, (1, LANES), 1)
    low = lane < HEAD_DIM

    for hk in range(ATTN_KV_HEADS):
        kv = kv_ref[0, hk].astype(F32)
        ks = kv * (QK_SCALE * _LOG2E)
        ks_ref[hk] = ks.astype(BF16)
        kss_ref[hk] = pltpu.roll(ks, HEAD_DIM, 1).astype(BF16)
        kv_t = jnp.transpose(kv)
        vt_ref[hk] = jnp.concatenate(
            [kv_t[HEAD_DIM:], jnp.ones((_BF16_ROWS, seq), F32)], axis=0).astype(BF16)

    kj = lax.broadcasted_iota(jnp.int32, (_KEYS, _STACK), 0)
    c = lax.broadcasted_iota(jnp.int32, (_KEYS, _STACK), 1)
    qi = jnp.where(c < BLOCK, c, c - BLOCK)
    dist = jnp.abs(kj - BLOCK - qi)
    for hk in range(ATTN_KV_HEADS):
        for par in range(2):
            slope = slope_ref[hk, par, 0:1, :]
            bias_ref[hk, par] = jnp.where(dist <= WINDOW, -slope * dist.astype(F32), NEG_INF) * _LOG2E

    zero = jnp.zeros((BLOCK, LANES), BF16)

    def key_range(n):
        lo_blk, hi_blk = max(n - 1, 0), min(n + 2, n_blocks)
        return pl.ds(lo_blk * BLOCK, (hi_blk - lo_blk) * BLOCK), (lo_blk - (n - 1)) * BLOCK, \
            (hi_blk - lo_blk) * BLOCK

    units = [(hk, n, par) for hk in range(ATTN_KV_HEADS) for n in range(n_blocks) for par in range(2)]
    live = [dict() for _ in units]

    def stage_scores(u):
        hk, n, par = units[u]
        keys, c0, width = key_range(n)
        qblk = q_ref[0, hk, pl.ds(n * BLOCK, BLOCK), :]
        qa, qb = qblk[:, :LANES], qblk[:, LANES:]
        if par == 0:
            qs = jnp.concatenate([jnp.where(low, qa, zero), jnp.where(low, qb, zero)], axis=0)
            kmat = ks_ref[hk, keys, :]
        else:
            qs = jnp.concatenate([jnp.where(low, zero, qa), jnp.where(low, zero, qb)], axis=0)
            kmat = kss_ref[hk, keys, :]
        live[u]["s"] = _dot_nt(kmat, qs) + bias_ref[hk, par, c0:c0 + width, :]

    def stage_softmax(u):
        hk, _, par = units[u]
        st = live[u]
        s = st.pop("s")
        sink = sink_ref[hk, par, 0:1, :] * _LOG2E
        m = jnp.maximum(jnp.max(s, axis=0, keepdims=True), sink)
        st["e"] = jnp.exp2(s - m).astype(BF16)
        st["esink"] = jnp.exp2(sink - m)

    def stage_values(u):
        hk, n, _ = units[u]
        keys, _, _ = key_range(n)
        live[u]["pv"] = _dot(vt_ref[hk, :, keys], live[u].pop("e"))

    def stage_out(u):
        hk, n, par = units[u]
        st = live[u]
        pv = st.pop("pv")
        st["out"] = pv[:HEAD_DIM] / (pv[HEAD_DIM:HEAD_DIM + 1] + st.pop("esink"))
        if par == 1:
            even, odd = live[u - 1].pop("out"), st.pop("out")
            y_t = jnp.concatenate(
                [even[:, :BLOCK], odd[:, :BLOCK], even[:, BLOCK:], odd[:, BLOCK:]], axis=0)
            o_ref[0, pl.ds(n * BLOCK, BLOCK), pl.ds(hk * _STACK, _STACK)] = \
                jnp.transpose(y_t).astype(BF16)

    stages = (stage_scores, stage_softmax, stage_values, stage_out)
    for i in range(len(units) + _ATTN_STAGE_LAG[-1]):
        for lag, stage in zip(_ATTN_STAGE_LAG, stages):
            if 0 <= i - lag < len(units):
                stage(i - lag)


def _attention(aq, akv, slope_cols, sink_cols):
    b, _, s, qw = aq.shape
    return pl.pallas_call(
        functools.partial(_attn_kernel, seq=s),
        grid=(b,),
        in_specs=[
            pl.BlockSpec((1, ATTN_KV_HEADS, s, qw), lambda i: (i, 0, 0, 0)),
            pl.BlockSpec((1, ATTN_KV_HEADS, s, LANES), lambda i: (i, 0, 0, 0)),
            pl.BlockSpec((ATTN_KV_HEADS, 2, _SUBLANES, _STACK), lambda i: (0, 0, 0, 0)),
            pl.BlockSpec((ATTN_KV_HEADS, 2, _SUBLANES, _STACK), lambda i: (0, 0, 0, 0)),
        ],
        out_specs=pl.BlockSpec((1, s, ATTN_WIDTH), lambda i: (i, 0, 0)),
        out_shape=jax.ShapeDtypeStruct((b, s, ATTN_WIDTH), BF16),
        scratch_shapes=[
            pltpu.VMEM((ATTN_KV_HEADS, s, LANES), BF16),
            pltpu.VMEM((ATTN_KV_HEADS, s, LANES), BF16),
            pltpu.VMEM((ATTN_KV_HEADS, HEAD_DIM + _BF16_ROWS, s), BF16),
            pltpu.VMEM((ATTN_KV_HEADS, 2, _KEYS, _STACK), F32),
        ],
        compiler_params=pltpu.CompilerParams(
            dimension_semantics=("arbitrary",), vmem_limit_bytes=VMEM_LIMIT_BYTES),
        name="attention",
    )(aq, akv, slope_cols, sink_cols)


_FFN_SUB_TILES = 4


def _layer_norm(z, gain, bias):
    mu = jnp.mean(z, axis=-1, keepdims=True)
    dlt = z - mu
    var = jnp.mean(dlt * dlt, axis=-1, keepdims=True)
    return dlt * lax.rsqrt(var + LN_EPS) * gain + bias


def _ffn_kernel(x_ref, yr_ref, ya_ref, p_ref, wo_ref, g1_ref, b1_ref, wg_ref, wu_ref, wd_ref,
                wpe_ref, wpg_ref, g2_ref, b2_ref, o_ref, *, alpha):
    tm = x_ref.shape[0]
    rows = [pl.ds(k * (tm // _FFN_SUB_TILES), tm // _FFN_SUB_TILES) for k in range(_FFN_SUB_TILES)]
    mix = [_dot(yr_ref[r, :], wo_ref[0:RET_WIDTH, :]) + _dot(ya_ref[r, :], wo_ref[RET_WIDTH:, :])
           for r in rows]
    h = [_layer_norm(alpha * x_ref[r, :] + m, g1_ref[...], b1_ref[...]) for r, m in zip(rows, mix)]
    z = []
    for r, hk in zip(rows, h):
        hb = hk.astype(BF16)
        g = _dot(hb, wg_ref[...])
        up = _dot(hb, wu_ref[...])
        act = (g * jax.nn.sigmoid(g) * up).astype(BF16)
        ffn = _dot(act, wd_ref[...])
        ple = _dot(p_ref[r, :].astype(BF16), wpe_ref[...])
        ple_gate = jax.nn.sigmoid(_dot(hb, wpg_ref[...]))
        z.append(alpha * hk + ffn + ple_gate * ple)
    for r, zk in zip(rows, z):
        o_ref[r, :] = _layer_norm(zk, g2_ref[...], b2_ref[...])


def _ffn(x, yr, ya, p, wo, g1, b1, wg, wu, wd, wpe, wpg, g2, b2, *, alpha, tm):
    t, d = x.shape

    def rows(width):
        return pl.BlockSpec((tm, width), lambda i: (i, 0))

    def resident(arr):
        return pl.BlockSpec(arr.shape, lambda i: (0, 0), pipeline_mode=pl.Buffered(1))

    return pl.pallas_call(
        functools.partial(_ffn_kernel, alpha=alpha),
        grid=(t // tm,),
        in_specs=[rows(d), rows(RET_WIDTH), rows(ATTN_WIDTH), rows(PLE_DIM),
                  resident(wo), resident(g1), resident(b1), resident(wg), resident(wu),
                  resident(wd), resident(wpe), resident(wpg), resident(g2), resident(b2)],
        out_specs=rows(d),
        out_shape=jax.ShapeDtypeStruct((t, d), F32),
        compiler_params=pltpu.CompilerParams(
            dimension_semantics=("arbitrary",), vmem_limit_bytes=VMEM_LIMIT_BYTES),
        name="ffn",
    )(x, yr, ya, p, wo, g1, b1, wg, wu, wd, wpe, wpg, g2, b2)


def _stacked_head_cols(per_head):
    v = per_head.astype(F32).reshape(ATTN_KV_HEADS, 2, 2)
    v = jnp.transpose(v, (0, 2, 1))
    v = jnp.repeat(v, BLOCK, axis=2)
    return jnp.broadcast_to(v[:, :, None, :], v.shape[:2] + (_SUBLANES, _STACK))


def kernel(x, p, w_in, ret_decay_fwd, ret_decay_bwd, ret_gn_gain, attn_sink, w_out, ln1_gain, ln1_bias,
           w_ffn_gate, w_ffn_up, w_ffn_down, w_ple_proj, w_ple_gate, ln2_gain, ln2_bias):
    depth = w_in.shape[0]
    b, s, d = x.shape
    alpha = (2.0 * depth) ** 0.25
    slopes = 2.0 ** (-8.0 * (jnp.arange(ATTN_HEADS, dtype=F32) + 1.0) / ATTN_HEADS)
    slope_cols = _stacked_head_cols(slopes)

    def lane_rep(v):
        v = v.astype(F32).reshape(RET_PAIRS, HEADS_PER_LANE_BLOCK, 1)
        return jnp.broadcast_to(v, (RET_PAIRS, HEADS_PER_LANE_BLOCK, LANES))

    row = lambda v: v.astype(F32).reshape(1, -1)
    h = x.astype(F32)
    for i in range(depth):
        (ret, aq, akv), (wo, wg, wu, wd, wpe, wpg) = _in_proj(
            h, w_in, i, (w_out, w_ffn_gate, w_ffn_up, w_ffn_down, w_ple_proj, w_ple_gate), tm=IN_PROJ_TILE)
        y_ret = _retention(ret, lane_rep(ret_decay_fwd[i]), lane_rep(ret_decay_bwd[i]),
                           ret_gn_gain[i].astype(F32).reshape(RET_PAIRS, 1, LANES))
        y_att = _attention(aq, akv, slope_cols, _stacked_head_cols(attn_sink[i]))
        h = _ffn(h.reshape(b * s, d), y_ret.reshape(b * s, RET_WIDTH), y_att.reshape(b * s, ATTN_WIDTH),
                 p[i].reshape(b * s, PLE_DIM),
                 wo, row(ln1_gain[i]), row(ln1_bias[i]), wg, wu, wd, wpe, wpg,
                 row(ln2_gain[i]), row(ln2_bias[i]), alpha=alpha, tm=FFN_TILE).reshape(b, s, d)
    return h.astype(x.dtype)
```

```python
import functools

import jax
import jax.numpy as jnp
from jax import lax
from jax.experimental import pallas as pl
from jax.experimental.pallas import tpu as pltpu

D_MODEL = 1024
HEAD_DIM = 64
RET_HEADS = 8
ATTN_HEADS = 8
ATTN_KV_HEADS = 2
ATTN_GROUP = ATTN_HEADS // ATTN_KV_HEADS
RET_WIDTH = RET_HEADS * HEAD_DIM
ATTN_WIDTH = ATTN_HEADS * HEAD_DIM
KV_WIDTH = ATTN_KV_HEADS * HEAD_DIM
IN_PROJ_WIDTH = 4 * RET_WIDTH + ATTN_WIDTH + 2 * KV_WIDTH
CHUNK = 128
WINDOW = 128
BLOCK = 128
PLE_DIM = 256
FFN_HIDDEN = ((8 * D_MODEL + 767) // 768) * 256
LN_EPS = 1e-5
GN_EPS = 1e-5
NEG_INF = -1e30
QK_SCALE = HEAD_DIM ** -0.5

LANES = 128
HEADS_PER_LANE_BLOCK = LANES // HEAD_DIM
RET_PAIRS = RET_HEADS // HEADS_PER_LANE_BLOCK
VMEM_LIMIT_BYTES = 56 * 1024 * 1024
IN_PROJ_TILE = 1024
FFN_TILE = 512

F32 = jnp.float32
BF16 = jnp.bfloat16


def _dot(a, b):
    return jnp.dot(a, b, preferred_element_type=F32)


def _dot_nt(a, b):
    return lax.dot_general(a, b, (((1,), (1,)), ((), ())), preferred_element_type=F32)


def _dot_tn(a, b):
    return lax.dot_general(a, b, (((0,), (0,)), ((), ())), preferred_element_type=F32)


_IN_PROJ_COLS = 256
_RET_BLOCKS = 4 * RET_PAIRS
_KV_COL0 = 4 * RET_WIDTH + ATTN_WIDTH
_BF16_ROWS = 16


def _cast_rows_per_step(n_rows, n_steps):
    for rows in range(_BF16_ROWS, n_rows + 1, _BF16_ROWS):
        if n_rows % rows == 0 and n_rows // rows <= n_steps:
            return rows
    raise ValueError(f"no bf16-tile row split of {n_rows} rows over {n_steps} steps")


def _in_proj_kernel(x_ref, w_ref, *refs, n_cast):
    cast_in, (ret_ref, aq_ref, akv_ref) = refs[:n_cast], refs[n_cast:n_cast + 3]
    cast_out, wb_ref = refs[n_cast + 3:2 * n_cast + 3], refs[2 * n_cast + 3]

    @pl.when((pl.program_id(0) == 0) & (pl.program_id(1) == 0))
    def _():
        wb_ref[:, 0:_KV_COL0] = w_ref[:, 0:_KV_COL0].astype(BF16)
        low = lax.broadcasted_iota(jnp.int32, (1, LANES), 1) < HEAD_DIM
        ak = w_ref[:, _KV_COL0:_KV_COL0 + KV_WIDTH]
        av = w_ref[:, _KV_COL0 + KV_WIDTH:_KV_COL0 + 2 * KV_WIDTH]
        wb_ref[:, _KV_COL0:_KV_COL0 + LANES] = jnp.where(
            low, ak, pltpu.roll(av, HEAD_DIM, 1)).astype(BF16)
        wb_ref[:, _KV_COL0 + LANES:_KV_COL0 + 2 * LANES] = jnp.where(
            low, pltpu.roll(ak, HEAD_DIM, 1), av).astype(BF16)

    for src, dst in zip(cast_in, cast_out):
        dst[...] = src[...].astype(BF16)

    xb = x_ref[0].astype(BF16)
    for c in range(IN_PROJ_WIDTH // _IN_PROJ_COLS):
        u = _dot(xb, wb_ref[:, c * _IN_PROJ_COLS:(c + 1) * _IN_PROJ_COLS]).astype(BF16)
        if 2 * c < _RET_BLOCKS:
            ret_ref[0, 2 * c] = u[:, :LANES]
            ret_ref[0, 2 * c + 1] = u[:, LANES:]
        elif c < _RET_BLOCKS // 2 + ATTN_KV_HEADS:
            aq_ref[0, c - _RET_BLOCKS // 2] = u
        else:
            akv_ref[0, 0] = u[:, :LANES]
            akv_ref[0, 1] = u[:, LANES:]


def _cast_specs(cast_weights, layer, n_steps, step_of):
    in_specs, out_specs, out_shapes = [], [], []
    for w in cast_weights:
        rows = _cast_rows_per_step(w.shape[1], n_steps)
        last = w.shape[1] // rows - 1
        in_specs.append(pl.BlockSpec(
            (None, rows, w.shape[2]),
            lambda *idx, last=last: (layer, jnp.minimum(step_of(*idx), last), 0)))
        out_specs.append(pl.BlockSpec(
            (rows, w.shape[2]), lambda *idx, last=last: (jnp.minimum(step_of(*idx), last), 0)))
        out_shapes.append(jax.ShapeDtypeStruct(w.shape[1:], BF16))
    return in_specs, out_specs, out_shapes


def _in_proj(x, w_in, layer, cast_weights, tm):
    b, s, d = x.shape
    n_j = s // tm
    cast_in_specs, cast_out_specs, cast_out_shapes = _cast_specs(
        cast_weights, layer, b * n_j, lambda i, j: i * n_j + j)

    outs = pl.pallas_call(
        functools.partial(_in_proj_kernel, n_cast=len(cast_weights)),
        grid=(b, n_j),
        in_specs=[
            pl.BlockSpec((1, tm, d), lambda i, j: (i, j, 0)),
            pl.BlockSpec((None, d, IN_PROJ_WIDTH), lambda i, j: (layer, 0, 0),
                         pipeline_mode=pl.Buffered(1)),
        ] + cast_in_specs,
        out_specs=[
            pl.BlockSpec((1, _RET_BLOCKS, tm, LANES), lambda i, j: (i, 0, j, 0)),
            pl.BlockSpec((1, ATTN_KV_HEADS, tm, ATTN_GROUP * HEAD_DIM), lambda i, j: (i, 0, j, 0)),
            pl.BlockSpec((1, ATTN_KV_HEADS, tm, LANES), lambda i, j: (i, 0, j, 0)),
        ] + cast_out_specs,
        out_shape=[
            jax.ShapeDtypeStruct((b, _RET_BLOCKS, s, LANES), BF16),
            jax.ShapeDtypeStruct((b, ATTN_KV_HEADS, s, ATTN_GROUP * HEAD_DIM), BF16),
            jax.ShapeDtypeStruct((b, ATTN_KV_HEADS, s, LANES), BF16),
        ] + cast_out_shapes,
        scratch_shapes=[pltpu.VMEM((d, IN_PROJ_WIDTH), BF16)],
        compiler_params=pltpu.CompilerParams(
            dimension_semantics=("arbitrary", "arbitrary"), vmem_limit_bytes=VMEM_LIMIT_BYTES),
        name="in_proj",
    )(x, w_in, *cast_weights)
    return outs[:3], outs[3:]


def _ret_body(q_ref, k_ref, v_ref, g_ref, decf_ref, decb_ref, gain_ref, o_ref, kv_ref, st_ref,
              p_ref, y_ref, *, seq, pp):
    n_chunks = seq // CHUNK
    kv_ref, st_ref, p_ref, y_ref = kv_ref.at[pp], st_ref.at[pp], p_ref.at[pp], y_ref.at[pp]
    out_lanes = pl.ds(pp * LANES, LANES)
    lane = lax.broadcasted_iota(jnp.int32, (1, LANES), 1)
    head0 = lane < HEAD_DIM

    def log_gamma(dec):
        return jnp.log1p(-jnp.exp2(dec))

    lgf0, lgf1 = log_gamma(decf_ref[pp, 0:1, :]), log_gamma(decf_ref[pp, 1:2, :])
    lgb0, lgb1 = log_gamma(decb_ref[pp, 0:1, :]), log_gamma(decb_ref[pp, 1:2, :])
    lgf = jnp.where(head0, lgf0, lgf1)
    lgb = jnp.where(head0, lgb0, lgb1)

    idx = lax.broadcasted_iota(jnp.int32, (CHUNK, LANES), 0).astype(F32)
    qdf = jnp.exp(lgf * (idx + 1.0))
    kdf = jnp.exp(lgf * (CHUNK - 1.0 - idx)) * QK_SCALE
    qdb = jnp.exp(lgb * (CHUNK - idx))
    kdb = jnp.exp(lgb * idx) * QK_SCALE
    cdf = jnp.exp(lgf * CHUNK)
    cdb = jnp.exp(lgb * CHUNK)

    ri = lax.broadcasted_iota(jnp.int32, (CHUNK, CHUNK), 0)
    ci = lax.broadcasted_iota(jnp.int32, (CHUNK, CHUNK), 1)
    diff = (ri - ci).astype(F32)

    def intra_mask(lf, lb):
        fwd = jnp.exp(lf * jnp.maximum(diff, 0.0))
        bwd = jnp.exp(lb * jnp.maximum(-diff, 0.0))
        return jnp.where(diff > 0, fwd, jnp.where(diff < 0, bwd, fwd + bwd))

    mask2 = jnp.concatenate([intra_mask(lgf0, lgb0), intra_mask(lgf1, lgb1)], axis=0) * QK_SCALE
    same_head = (ri < HEAD_DIM) == (ci < HEAD_DIM)
    gn_avg = jnp.where(same_head, 1.0 / HEAD_DIM, 0.0).astype(BF16)

    for n in range(n_chunks):
        sl = pl.ds(n * CHUNK, CHUNK)
        kc = k_ref[0, pp, sl, :].astype(F32)
        kd = jnp.concatenate([(kc * kdf).astype(BF16), (kc * kdb).astype(BF16)], axis=1)
        kv = _dot_tn(kd, v_ref[0, pp, sl, :])
        kv_ref[n] = kv
        yield

    zero_state = jnp.zeros((LANES, LANES), BF16)
    state = jnp.zeros((LANES, LANES), F32)
    for n in range(n_chunks):
        st_ref[n, 0:LANES, :] = jnp.where(same_head, state.astype(BF16), zero_state)
        state = state * cdf + kv_ref[n, 0:LANES, :]
    state = jnp.zeros((LANES, LANES), F32)
    for n in reversed(range(n_chunks)):
        st_ref[n, LANES:2 * LANES, :] = jnp.where(same_head, state.astype(BF16), zero_state)
        state = state * cdb + kv_ref[n, LANES:2 * LANES, :]

    zero = jnp.zeros((CHUNK, LANES), BF16)
    for n in range(n_chunks):
        sl = pl.ds(n * CHUNK, CHUNK)
        qb = q_ref[0, pp, sl, :]
        kb = k_ref[0, pp, sl, :]
        qs = jnp.concatenate([jnp.where(head0, qb, zero), jnp.where(head0, zero, qb)], axis=0)
        p = (_dot_nt(qs, kb) * mask2).astype(BF16)
        p_ref[n, :, 0:CHUNK] = p[0:CHUNK]
        p_ref[n, :, CHUNK:2 * CHUNK] = p[CHUNK:2 * CHUNK]
        yield

    for n in range(n_chunks):
        sl = pl.ds(n * CHUNK, CHUNK)
        vb = v_ref[0, pp, sl, :]
        qc = q_ref[0, pp, sl, :].astype(F32)
        lhs = jnp.concatenate([p_ref[n], (qc * qdf).astype(BF16), (qc * qdb).astype(BF16)], axis=1)
        rhs = jnp.concatenate([jnp.where(head0, vb, zero), jnp.where(head0, zero, vb), st_ref[n]],
                              axis=0)
        y_ref[sl, :] = _dot(lhs, rhs)
        yield

    def group_mean(z):
        return _dot(z.astype(BF16), gn_avg)

    for n in range(n_chunks):
        sl = pl.ds(n * CHUNK, CHUNK)
        y = y_ref[sl, :]
        y_ref[sl, :] = y - group_mean(y)
        yield

    gain = gain_ref[pp]
    for n in range(n_chunks):
        sl = pl.ds(n * CHUNK, CHUNK)
        dlt = y_ref[sl, :]
        var = group_mean(dlt * dlt)
        gate = g_ref[0, pp, sl, :].astype(F32)
        o_ref[0, sl, out_lanes] = (dlt * lax.rsqrt(var + GN_EPS) * gain
                                   * (gate * jax.nn.sigmoid(gate))).astype(BF16)
        yield


_RET_PAIRS_PER_STEP = 2


def _ret_kernel(*refs, seq):
    bodies = [_ret_body(*refs, seq=seq, pp=pp) for pp in range(_RET_PAIRS_PER_STEP)]
    alive = list(bodies)
    while alive:
        for gen in list(alive):
            try:
                next(gen)
            except StopIteration:
                alive.remove(gen)


def _retention(ret, decf, decb, gain):
    b, _, s, _ = ret.shape
    n_chunks = s // CHUNK
    pps = _RET_PAIRS_PER_STEP
    steps = RET_PAIRS // pps

    def part(kind):
        return pl.BlockSpec((1, pps, s, LANES), lambda i, j, kind=kind: (i, kind * steps + j, 0, 0))

    return pl.pallas_call(
        functools.partial(_ret_kernel, seq=s),
        grid=(b, steps),
        in_specs=[
            part(0), part(1), part(2), part(3),
            pl.BlockSpec((pps, HEADS_PER_LANE_BLOCK, LANES), lambda i, j: (j, 0, 0)),
            pl.BlockSpec((pps, HEADS_PER_LANE_BLOCK, LANES), lambda i, j: (j, 0, 0)),
            pl.BlockSpec((pps, 1, LANES), lambda i, j: (j, 0, 0)),
        ],
        out_specs=pl.BlockSpec((1, s, pps * LANES), lambda i, j: (i, 0, j)),
        out_shape=jax.ShapeDtypeStruct((b, s, RET_WIDTH), BF16),
        scratch_shapes=[
            pltpu.VMEM((pps, n_chunks, 2 * LANES, LANES), F32),
            pltpu.VMEM((pps, n_chunks, 2 * LANES, LANES), BF16),
            pltpu.VMEM((pps, n_chunks, CHUNK, 2 * CHUNK), BF16),
            pltpu.VMEM((pps, s, LANES), F32),
        ],
        compiler_params=pltpu.CompilerParams(
            dimension_semantics=("arbitrary", "arbitrary"), vmem_limit_bytes=VMEM_LIMIT_BYTES),
        name="retention",
    )(ret, ret, ret, ret, decf, decb, gain)


_STACK = 2 * BLOCK
_KEYS = 3 * BLOCK
_SUBLANES = 8
_LOG2E = 1.4426950408889634
_ATTN_STAGE_LAG = (0, 2, 4, 5)


def _attn_kernel(q_ref, kv_ref, slope_ref, sink_ref, *refs, seq, n_cast):
    cast_in, o_ref, cast_out = refs[:n_cast], refs[n_cast], refs[n_cast + 1:2 * n_cast + 1]
    ks_ref, kss_ref, vt_ref, bias_ref = refs[2 * n_cast + 1:]
    for src, dst in zip(cast_in, cast_out):
        dst[...] = src[...].astype(BF16)

    n_blocks = seq // BLOCK
    lane = lax.broadcasted_iota(jnp.int32, (1, LANES), 1)
    low = lane < HEAD_DIM

    for hk in range(ATTN_KV_HEADS):
        kv = kv_ref[0, hk].astype(F32)
        ks = kv * (QK_SCALE * _LOG2E)
        ks_ref[hk] = ks.astype(BF16)
        kss_ref[hk] = pltpu.roll(ks, HEAD_DIM, 1).astype(BF16)
        kv_t = jnp.transpose(kv)
        vt_ref[hk] = jnp.concatenate(
            [kv_t[HEAD_DIM:], jnp.ones((_BF16_ROWS, seq), F32)], axis=0).astype(BF16)

    kj = lax.broadcasted_iota(jnp.int32, (_KEYS, _STACK), 0)
    c = lax.broadcasted_iota(jnp.int32, (_KEYS, _STACK), 1)
    qi = jnp.where(c < BLOCK, c, c - BLOCK)
    dist = jnp.abs(kj - BLOCK - qi)
    for hk in range(ATTN_KV_HEADS):
        for par in range(2):
            slope = slope_ref[hk, par, 0:1, :]
            bias_ref[hk, par] = jnp.where(dist <= WINDOW, -slope * dist.astype(F32), NEG_INF) * _LOG2E

    zero = jnp.zeros((BLOCK, LANES), BF16)

    def key_range(n):
        lo_blk, hi_blk = max(n - 1, 0), min(n + 2, n_blocks)
        return pl.ds(lo_blk * BLOCK, (hi_blk - lo_blk) * BLOCK), (lo_blk - (n - 1)) * BLOCK, \
            (hi_blk - lo_blk) * BLOCK

    units = [(hk, n, par) for hk in range(ATTN_KV_HEADS) for n in range(n_blocks) for par in range(2)]
    live = [dict() for _ in units]

    def stage_scores(u):
        hk, n, par = units[u]
        keys, c0, width = key_range(n)
        qblk = q_ref[0, hk, pl.ds(n * BLOCK, BLOCK), :]
        qa, qb = qblk[:, :LANES], qblk[:, LANES:]
        if par == 0:
            qs = jnp.concatenate([jnp.where(low, qa, zero), jnp.where(low, qb, zero)], axis=0)
            kmat = ks_ref[hk, keys, :]
        else:
            qs = jnp.concatenate([jnp.where(low, zero, qa), jnp.where(low, zero, qb)], axis=0)
            kmat = kss_ref[hk, keys, :]
        live[u]["s"] = _dot_nt(kmat, qs) + bias_ref[hk, par, c0:c0 + width, :]

    def stage_softmax(u):
        hk, _, par = units[u]
        st = live[u]
        s = st.pop("s")
        sink = sink_ref[hk, par, 0:1, :] * _LOG2E
        m = jnp.maximum(jnp.max(s, axis=0, keepdims=True), sink)
        st["e"] = jnp.exp2(s - m).astype(BF16)
        st["esink"] = jnp.exp2(sink - m)

    def stage_values(u):
        hk, n, _ = units[u]
        keys, _, _ = key_range(n)
        live[u]["pv"] = _dot(vt_ref[hk, :, keys], live[u].pop("e"))

    def stage_out(u):
        hk, n, par = units[u]
        st = live[u]
        pv = st.pop("pv")
        st["out"] = pv[:HEAD_DIM] / (pv[HEAD_DIM:HEAD_DIM + 1] + st.pop("esink"))
        if par == 1:
            even, odd = live[u - 1].pop("out"), st.pop("out")
            y_t = jnp.concatenate(
                [even[:, :BLOCK], odd[:, :BLOCK], even[:, BLOCK:], odd[:, BLOCK:]], axis=0)
            o_ref[0, pl.ds(n * BLOCK, BLOCK), pl.ds(hk * _STACK, _STACK)] = \
                jnp.transpose(y_t).astype(BF16)

    stages = (stage_scores, stage_softmax, stage_values, stage_out)
    for i in range(len(units) + _ATTN_STAGE_LAG[-1]):
        for lag, stage in zip(_ATTN_STAGE_LAG, stages):
            if 0 <= i - lag < len(units):
                stage(i - lag)


def _attention(aq, akv, slope_cols, sink_cols, layer, cast_weights):
    b, _, s, qw = aq.shape
    cast_in_specs, cast_out_specs, cast_out_shapes = _cast_specs(cast_weights, layer, b, lambda i: i)
    outs = pl.pallas_call(
        functools.partial(_attn_kernel, seq=s, n_cast=len(cast_weights)),
        grid=(b,),
        in_specs=[
            pl.BlockSpec((1, ATTN_KV_HEADS, s, qw), lambda i: (i, 0, 0, 0)),
            pl.BlockSpec((1, ATTN_KV_HEADS, s, LANES), lambda i: (i, 0, 0, 0)),
            pl.BlockSpec((ATTN_KV_HEADS, 2, _SUBLANES, _STACK), lambda i: (0, 0, 0, 0)),
            pl.BlockSpec((ATTN_KV_HEADS, 2, _SUBLANES, _STACK), lambda i: (0, 0, 0, 0)),
        ] + cast_in_specs,
        out_specs=[pl.BlockSpec((1, s, ATTN_WIDTH), lambda i: (i, 0, 0))] + cast_out_specs,
        out_shape=[jax.ShapeDtypeStruct((b, s, ATTN_WIDTH), BF16)] + cast_out_shapes,
        scratch_shapes=[
            pltpu.VMEM((ATTN_KV_HEADS, s, LANES), BF16),
            pltpu.VMEM((ATTN_KV_HEADS, s, LANES), BF16),
            pltpu.VMEM((ATTN_KV_HEADS, HEAD_DIM + _BF16_ROWS, s), BF16),
            pltpu.VMEM((ATTN_KV_HEADS, 2, _KEYS, _STACK), F32),
        ],
        compiler_params=pltpu.CompilerParams(
            dimension_semantics=("arbitrary",), vmem_limit_bytes=VMEM_LIMIT_BYTES),
        name="attention",
    )(aq, akv, slope_cols, sink_cols, *cast_weights)
    return outs[0], outs[1:]


_FFN_SUB_TILES = 2


def _layer_norm(z, gain, bias):
    mu = jnp.mean(z, axis=-1, keepdims=True)
    dlt = z - mu
    var = jnp.mean(dlt * dlt, axis=-1, keepdims=True)
    return dlt * lax.rsqrt(var + LN_EPS) * gain + bias


def _ffn_kernel(x_ref, yr_ref, ya_ref, p_ref, wo_ref, g1_ref, b1_ref, wg_ref, wu_ref, wd_ref,
                wpe_ref, wpg_ref, g2_ref, b2_ref, o_ref, *, alpha):
    tm = x_ref.shape[0]
    rows = [pl.ds(k * (tm // _FFN_SUB_TILES), tm // _FFN_SUB_TILES) for k in range(_FFN_SUB_TILES)]
    mix = [_dot(yr_ref[r, :], wo_ref[0:RET_WIDTH, :]) + _dot(ya_ref[r, :], wo_ref[RET_WIDTH:, :])
           for r in rows]
    h = [_layer_norm(alpha * x_ref[r, :] + m, g1_ref[...], b1_ref[...]) for r, m in zip(rows, mix)]
    z = []
    for r, hk in zip(rows, h):
        hb = hk.astype(BF16)
        g = _dot(hb, wg_ref[...])
        up = _dot(hb, wu_ref[...])
        act = (g * jax.nn.sigmoid(g) * up).astype(BF16)
        ffn = _dot(act, wd_ref[...])
        ple = _dot(p_ref[r, :].astype(BF16), wpe_ref[...])
        ple_gate = jax.nn.sigmoid(_dot(hb, wpg_ref[...]))
        z.append(alpha * hk + ffn + ple_gate * ple)
    for r, zk in zip(rows, z):
        o_ref[r, :] = _layer_norm(zk, g2_ref[...], b2_ref[...])


def _ffn(x, yr, ya, p, wo, g1, b1, wg, wu, wd, wpe, wpg, g2, b2, *, alpha, tm):
    t, d = x.shape

    def rows(width):
        return pl.BlockSpec((tm, width), lambda i: (i, 0))

    def resident(arr):
        return pl.BlockSpec(arr.shape, lambda i: (0, 0), pipeline_mode=pl.Buffered(1))

    return pl.pallas_call(
        functools.partial(_ffn_kernel, alpha=alpha),
        grid=(t // tm,),
        in_specs=[rows(d), rows(RET_WIDTH), rows(ATTN_WIDTH), rows(PLE_DIM),
                  resident(wo), resident(g1), resident(b1), resident(wg), resident(wu),
                  resident(wd), resident(wpe), resident(wpg), resident(g2), resident(b2)],
        out_specs=rows(d),
        out_shape=jax.ShapeDtypeStruct((t, d), F32),
        compiler_params=pltpu.CompilerParams(
            dimension_semantics=("arbitrary",), vmem_limit_bytes=VMEM_LIMIT_BYTES),
        name="ffn",
    )(x, yr, ya, p, wo, g1, b1, wg, wu, wd, wpe, wpg, g2, b2)


def _stacked_head_cols(per_head):
    v = per_head.astype(F32).reshape(ATTN_KV_HEADS, 2, 2)
    v = jnp.transpose(v, (0, 2, 1))
    v = jnp.repeat(v, BLOCK, axis=2)
    return jnp.broadcast_to(v[:, :, None, :], v.shape[:2] + (_SUBLANES, _STACK))


def kernel(x, p, w_in, ret_decay_fwd, ret_decay_bwd, ret_gn_gain, attn_sink, w_out, ln1_gain, ln1_bias,
           w_ffn_gate, w_ffn_up, w_ffn_down, w_ple_proj, w_ple_gate, ln2_gain, ln2_bias):
    depth = w_in.shape[0]
    b, s, d = x.shape
    alpha = (2.0 * depth) ** 0.25
    slopes = 2.0 ** (-8.0 * (jnp.arange(ATTN_HEADS, dtype=F32) + 1.0) / ATTN_HEADS)
    slope_cols = _stacked_head_cols(slopes)

    def lane_rep(v):
        v = v.astype(F32).reshape(RET_PAIRS, HEADS_PER_LANE_BLOCK, 1)
        return jnp.broadcast_to(v, (RET_PAIRS, HEADS_PER_LANE_BLOCK, LANES))

    row = lambda v: v.astype(F32).reshape(1, -1)
    h = x.astype(F32)
    for i in range(depth):
        (ret, aq, akv), _ = _in_proj(h, w_in, i, (), tm=IN_PROJ_TILE)
        y_ret = _retention(ret, lane_rep(ret_decay_fwd[i]), lane_rep(ret_decay_bwd[i]),
                           ret_gn_gain[i].astype(F32).reshape(RET_PAIRS, 1, LANES))
        y_att, (wo, wg, wu, wd, wpe, wpg) = _attention(
            aq, akv, slope_cols, _stacked_head_cols(attn_sink[i]), i,
            (w_out, w_ffn_gate, w_ffn_up, w_ffn_down, w_ple_proj, w_ple_gate))
        h = _ffn(h.reshape(b * s, d), y_ret.reshape(b * s, RET_WIDTH), y_att.reshape(b * s, ATTN_WIDTH),
                 p[i].reshape(b * s, PLE_DIM),
                 wo, row(ln1_gain[i]), row(ln1_bias[i]), wg, wu, wd, wpe, wpg,
                 row(ln2_gain[i]), row(ln2_bias[i]), alpha=alpha, tm=FFN_TILE).reshape(b, s, d)
    return h.astype(x.dtype)
```

```python
import functools

import jax
import jax.numpy as jnp
from jax import lax
from jax.experimental import pallas as pl
from jax.experimental.pallas import tpu as pltpu

D_MODEL = 1024
HEAD_DIM = 64
RET_HEADS = 8
ATTN_HEADS = 8
ATTN_KV_HEADS = 2
ATTN_GROUP = ATTN_HEADS // ATTN_KV_HEADS
RET_WIDTH = RET_HEADS * HEAD_DIM
ATTN_WIDTH = ATTN_HEADS * HEAD_DIM
KV_WIDTH = ATTN_KV_HEADS * HEAD_DIM
IN_PROJ_WIDTH = 4 * RET_WIDTH + ATTN_WIDTH + 2 * KV_WIDTH
CHUNK = 128
WINDOW = 128
BLOCK = 128
PLE_DIM = 256
FFN_HIDDEN = ((8 * D_MODEL + 767) // 768) * 256
LN_EPS = 1e-5
GN_EPS = 1e-5
NEG_INF = -1e30
QK_SCALE = HEAD_DIM ** -0.5

LANES = 128
HEADS_PER_LANE_BLOCK = LANES // HEAD_DIM
RET_PAIRS = RET_HEADS // HEADS_PER_LANE_BLOCK
VMEM_LIMIT_BYTES = 56 * 1024 * 1024
IN_PROJ_TILE = 1024
FFN_TILE = 512

F32 = jnp.float32
BF16 = jnp.bfloat16


def _dot(a, b):
    return jnp.dot(a, b, preferred_element_type=F32)


def _dot_nt(a, b):
    return lax.dot_general(a, b, (((1,), (1,)), ((), ())), preferred_element_type=F32)


def _dot_tn(a, b):
    return lax.dot_general(a, b, (((0,), (0,)), ((), ())), preferred_element_type=F32)


_IN_PROJ_COLS = 256
_RET_BLOCKS = 4 * RET_PAIRS
_KV_COL0 = 4 * RET_WIDTH + ATTN_WIDTH
_BF16_ROWS = 16


def _cast_rows_per_step(n_rows, n_steps):
    for rows in range(_BF16_ROWS, n_rows + 1, _BF16_ROWS):
        if n_rows % rows == 0 and n_rows // rows <= n_steps:
            return rows
    raise ValueError(f"no bf16-tile row split of {n_rows} rows over {n_steps} steps")


def _in_proj_kernel(x_ref, w_ref, *refs, n_cast):
    cast_in, (ret_ref, aq_ref, akv_ref) = refs[:n_cast], refs[n_cast:n_cast + 3]
    cast_out, wb_ref = refs[n_cast + 3:2 * n_cast + 3], refs[2 * n_cast + 3]

    @pl.when((pl.program_id(0) == 0) & (pl.program_id(1) == 0))
    def _():
        wb_ref[:, 0:_KV_COL0] = w_ref[:, 0:_KV_COL0].astype(BF16)
        low = lax.broadcasted_iota(jnp.int32, (1, LANES), 1) < HEAD_DIM
        ak = w_ref[:, _KV_COL0:_KV_COL0 + KV_WIDTH]
        av = w_ref[:, _KV_COL0 + KV_WIDTH:_KV_COL0 + 2 * KV_WIDTH]
        wb_ref[:, _KV_COL0:_KV_COL0 + LANES] = jnp.where(
            low, ak, pltpu.roll(av, HEAD_DIM, 1)).astype(BF16)
        wb_ref[:, _KV_COL0 + LANES:_KV_COL0 + 2 * LANES] = jnp.where(
            low, pltpu.roll(ak, HEAD_DIM, 1), av).astype(BF16)

    for src, dst in zip(cast_in, cast_out):
        dst[...] = src[...].astype(BF16)

    xb = x_ref[0].astype(BF16)
    for c in range(IN_PROJ_WIDTH // _IN_PROJ_COLS):
        u = _dot(xb, wb_ref[:, c * _IN_PROJ_COLS:(c + 1) * _IN_PROJ_COLS]).astype(BF16)
        if 2 * c < _RET_BLOCKS:
            ret_ref[0, 2 * c] = u[:, :LANES]
            ret_ref[0, 2 * c + 1] = u[:, LANES:]
        elif c < _RET_BLOCKS // 2 + ATTN_KV_HEADS:
            aq_ref[0, c - _RET_BLOCKS // 2] = u
        else:
            akv_ref[0, 0] = u[:, :LANES]
            akv_ref[0, 1] = u[:, LANES:]


def _cast_specs(cast_weights, layer, n_steps, step_of):
    in_specs, out_specs, out_shapes = [], [], []
    for w in cast_weights:
        rows = _cast_rows_per_step(w.shape[1], n_steps)
        last = w.shape[1] // rows - 1
        in_specs.append(pl.BlockSpec(
            (None, rows, w.shape[2]),
            lambda *idx, last=last: (layer, jnp.minimum(step_of(*idx), last), 0)))
        out_specs.append(pl.BlockSpec(
            (rows, w.shape[2]), lambda *idx, last=last: (jnp.minimum(step_of(*idx), last), 0)))
        out_shapes.append(jax.ShapeDtypeStruct(w.shape[1:], BF16))
    return in_specs, out_specs, out_shapes


def _in_proj(x, w_in, layer, cast_weights, tm):
    b, s, d = x.shape
    n_j = s // tm
    cast_in_specs, cast_out_specs, cast_out_shapes = _cast_specs(
        cast_weights, layer, b * n_j, lambda i, j: i * n_j + j)

    outs = pl.pallas_call(
        functools.partial(_in_proj_kernel, n_cast=len(cast_weights)),
        grid=(b, n_j),
        in_specs=[
            pl.BlockSpec((1, tm, d), lambda i, j: (i, j, 0)),
            pl.BlockSpec((None, d, IN_PROJ_WIDTH), lambda i, j: (layer, 0, 0),
                         pipeline_mode=pl.Buffered(1)),
        ] + cast_in_specs,
        out_specs=[
            pl.BlockSpec((1, _RET_BLOCKS, tm, LANES), lambda i, j: (i, 0, j, 0)),
            pl.BlockSpec((1, ATTN_KV_HEADS, tm, ATTN_GROUP * HEAD_DIM), lambda i, j: (i, 0, j, 0)),
            pl.BlockSpec((1, ATTN_KV_HEADS, tm, LANES), lambda i, j: (i, 0, j, 0)),
        ] + cast_out_specs,
        out_shape=[
            jax.ShapeDtypeStruct((b, _RET_BLOCKS, s, LANES), BF16),
            jax.ShapeDtypeStruct((b, ATTN_KV_HEADS, s, ATTN_GROUP * HEAD_DIM), BF16),
            jax.ShapeDtypeStruct((b, ATTN_KV_HEADS, s, LANES), BF16),
        ] + cast_out_shapes,
        scratch_shapes=[pltpu.VMEM((d, IN_PROJ_WIDTH), BF16)],
        compiler_params=pltpu.CompilerParams(
            dimension_semantics=("arbitrary", "arbitrary"), vmem_limit_bytes=VMEM_LIMIT_BYTES),
        name="in_proj",
    )(x, w_in, *cast_weights)
    return outs[:3], outs[3:]


def _ret_body(q_ref, k_ref, v_ref, g_ref, decf_ref, decb_ref, gain_ref, o_ref, kv_ref, st_ref,
              p_ref, y_ref, *, seq, pp):
    n_chunks = seq // CHUNK
    kv_ref, st_ref, p_ref, y_ref = kv_ref.at[pp], st_ref.at[pp], p_ref.at[pp], y_ref.at[pp]
    out_lanes = pl.ds(pp * LANES, LANES)
    lane = lax.broadcasted_iota(jnp.int32, (1, LANES), 1)
    head0 = lane < HEAD_DIM

    def log_gamma(dec):
        return jnp.log1p(-jnp.exp2(dec))

    lgf0, lgf1 = log_gamma(decf_ref[pp, 0:1, :]), log_gamma(decf_ref[pp, 1:2, :])
    lgb0, lgb1 = log_gamma(decb_ref[pp, 0:1, :]), log_gamma(decb_ref[pp, 1:2, :])
    lgf = jnp.where(head0, lgf0, lgf1)
    lgb = jnp.where(head0, lgb0, lgb1)

    idx = lax.broadcasted_iota(jnp.int32, (CHUNK, LANES), 0).astype(F32)
    qdf = jnp.exp(lgf * (idx + 1.0))
    kdf = jnp.exp(lgf * (CHUNK - 1.0 - idx)) * QK_SCALE
    qdb = jnp.exp(lgb * (CHUNK - idx))
    kdb = jnp.exp(lgb * idx) * QK_SCALE
    cdf = jnp.exp(lgf * CHUNK)
    cdb = jnp.exp(lgb * CHUNK)

    ri = lax.broadcasted_iota(jnp.int32, (CHUNK, CHUNK), 0)
    ci = lax.broadcasted_iota(jnp.int32, (CHUNK, CHUNK), 1)
    diff = (ri - ci).astype(F32)

    def intra_mask(lf, lb):
        fwd = jnp.exp(lf * jnp.maximum(diff, 0.0))
        bwd = jnp.exp(lb * jnp.maximum(-diff, 0.0))
        return jnp.where(diff > 0, fwd, jnp.where(diff < 0, bwd, fwd + bwd))

    mask2 = jnp.concatenate([intra_mask(lgf0, lgb0), intra_mask(lgf1, lgb1)], axis=0) * QK_SCALE
    same_head = (ri < HEAD_DIM) == (ci < HEAD_DIM)
    gn_avg = jnp.where(same_head, 1.0 / HEAD_DIM, 0.0).astype(BF16)

    for n in range(n_chunks):
        sl = pl.ds(n * CHUNK, CHUNK)
        kc = k_ref[0, pp, sl, :].astype(F32)
        kd = jnp.concatenate([(kc * kdf).astype(BF16), (kc * kdb).astype(BF16)], axis=1)
        kv = _dot_tn(kd, v_ref[0, pp, sl, :])
        kv_ref[n] = kv
        yield

    zero_state = jnp.zeros((LANES, LANES), BF16)
    state = jnp.zeros((LANES, LANES), F32)
    for n in range(n_chunks):
        st_ref[n, 0:LANES, :] = jnp.where(same_head, state.astype(BF16), zero_state)
        state = state * cdf + kv_ref[n, 0:LANES, :]
    state = jnp.zeros((LANES, LANES), F32)
    for n in reversed(range(n_chunks)):
        st_ref[n, LANES:2 * LANES, :] = jnp.where(same_head, state.astype(BF16), zero_state)
        state = state * cdb + kv_ref[n, LANES:2 * LANES, :]

    zero = jnp.zeros((CHUNK, LANES), BF16)
    for n in range(n_chunks):
        sl = pl.ds(n * CHUNK, CHUNK)
        qb = q_ref[0, pp, sl, :]
        kb = k_ref[0, pp, sl, :]
        qs = jnp.concatenate([jnp.where(head0, qb, zero), jnp.where(head0, zero, qb)], axis=0)
        p = (_dot_nt(qs, kb) * mask2).astype(BF16)
        p_ref[n, :, 0:CHUNK] = p[0:CHUNK]
        p_ref[n, :, CHUNK:2 * CHUNK] = p[CHUNK:2 * CHUNK]
        yield

    for n in range(n_chunks):
        sl = pl.ds(n * CHUNK, CHUNK)
        vb = v_ref[0, pp, sl, :]
        qc = q_ref[0, pp, sl, :].astype(F32)
        lhs = jnp.concatenate([p_ref[n], (qc * qdf).astype(BF16), (qc * qdb).astype(BF16)], axis=1)
        rhs = jnp.concatenate([jnp.where(head0, vb, zero), jnp.where(head0, zero, vb), st_ref[n]],
                              axis=0)
        y_ref[sl, :] = _dot(lhs, rhs)
        yield

    def group_mean(z):
        return _dot(z.astype(BF16), gn_avg)

    for n in range(n_chunks):
        sl = pl.ds(n * CHUNK, CHUNK)
        y = y_ref[sl, :]
        y_ref[sl, :] = y - group_mean(y)
        yield

    gain = gain_ref[pp]
    for n in range(n_chunks):
        sl = pl.ds(n * CHUNK, CHUNK)
        dlt = y_ref[sl, :]
        var = group_mean(dlt * dlt)
        gate = g_ref[0, pp, sl, :].astype(F32)
        o_ref[0, sl, out_lanes] = (dlt * lax.rsqrt(var + GN_EPS) * gain
                                   * (gate * jax.nn.sigmoid(gate))).astype(BF16)
        yield


_RET_PAIRS_PER_STEP = 2


def _ret_kernel(*refs, seq):
    bodies = [_ret_body(*refs, seq=seq, pp=pp) for pp in range(_RET_PAIRS_PER_STEP)]
    alive = list(bodies)
    while alive:
        for gen in list(alive):
            try:
                next(gen)
            except StopIteration:
                alive.remove(gen)


def _retention(ret, decf, decb, gain):
    b, _, s, _ = ret.shape
    n_chunks = s // CHUNK
    pps = _RET_PAIRS_PER_STEP
    steps = RET_PAIRS // pps

    def part(kind):
        return pl.BlockSpec((1, pps, s, LANES), lambda i, j, kind=kind: (i, kind * steps + j, 0, 0))

    return pl.pallas_call(
        functools.partial(_ret_kernel, seq=s),
        grid=(b, steps),
        in_specs=[
            part(0), part(1), part(2), part(3),
            pl.BlockSpec((pps, HEADS_PER_LANE_BLOCK, LANES), lambda i, j: (j, 0, 0)),
            pl.BlockSpec((pps, HEADS_PER_LANE_BLOCK, LANES), lambda i, j: (j, 0, 0)),
            pl.BlockSpec((pps, 1, LANES), lambda i, j: (j, 0, 0)),
        ],
        out_specs=pl.BlockSpec((1, s, pps * LANES), lambda i, j: (i, 0, j)),
        out_shape=jax.ShapeDtypeStruct((b, s, RET_WIDTH), BF16),
        scratch_shapes=[
            pltpu.VMEM((pps, n_chunks, 2 * LANES, LANES), F32),
            pltpu.VMEM((pps, n_chunks, 2 * LANES, LANES), BF16),
            pltpu.VMEM((pps, n_chunks, CHUNK, 2 * CHUNK), BF16),
            pltpu.VMEM((pps, s, LANES), F32),
        ],
        compiler_params=pltpu.CompilerParams(
            dimension_semantics=("arbitrary", "arbitrary"), vmem_limit_bytes=VMEM_LIMIT_BYTES),
        name="retention",
    )(ret, ret, ret, ret, decf, decb, gain)


_STACK = 2 * BLOCK
_KEYS = 3 * BLOCK
_SUBLANES = 8
_LOG2E = 1.4426950408889634
_ATTN_STAGE_LAG = (0, 2, 4, 5)


def _attn_kernel(q_ref, kv_ref, slope_ref, sink_ref, o_ref, ks_ref, kss_ref, vt_ref, bias_ref,
                 *, seq):
    n_blocks = seq // BLOCK
    lane = lax.broadcasted_iota(jnp.int32, (1, LANES), 1)
    low = lane < HEAD_DIM

    for hk in range(ATTN_KV_HEADS):
        kv = kv_ref[0, hk].astype(F32)
        ks = kv * (QK_SCALE * _LOG2E)
        ks_ref[hk] = ks.astype(BF16)
        kss_ref[hk] = pltpu.roll(ks, HEAD_DIM, 1).astype(BF16)
        kv_t = jnp.transpose(kv)
        vt_ref[hk] = jnp.concatenate(
            [kv_t[HEAD_DIM:], jnp.ones((_BF16_ROWS, seq), F32)], axis=0).astype(BF16)

    kj = lax.broadcasted_iota(jnp.int32, (_KEYS, _STACK), 0)
    c = lax.broadcasted_iota(jnp.int32, (_KEYS, _STACK), 1)
    qi = jnp.where(c < BLOCK, c, c - BLOCK)
    dist = jnp.abs(kj - BLOCK - qi)
    for hk in range(ATTN_KV_HEADS):
        for par in range(2):
            slope = slope_ref[hk, par, 0:1, :]
            bias_ref[hk, par] = jnp.where(dist <= WINDOW, -slope * dist.astype(F32), NEG_INF) * _LOG2E

    zero = jnp.zeros((BLOCK, LANES), BF16)

    def key_range(n):
        lo_blk, hi_blk = max(n - 1, 0), min(n + 2, n_blocks)
        return pl.ds(lo_blk * BLOCK, (hi_blk - lo_blk) * BLOCK), (lo_blk - (n - 1)) * BLOCK, \
            (hi_blk - lo_blk) * BLOCK

    units = [(hk, n, par) for hk in range(ATTN_KV_HEADS) for n in range(n_blocks) for par in range(2)]
    live = [dict() for _ in units]

    def stage_scores(u):
        hk, n, par = units[u]
        keys, c0, width = key_range(n)
        qblk = q_ref[0, hk, pl.ds(n * BLOCK, BLOCK), :]
        qa, qb = qblk[:, :LANES], qblk[:, LANES:]
        if par == 0:
            qs = jnp.concatenate([jnp.where(low, qa, zero), jnp.where(low, qb, zero)], axis=0)
            kmat = ks_ref[hk, keys, :]
        else:
            qs = jnp.concatenate([jnp.where(low, zero, qa), jnp.where(low, zero, qb)], axis=0)
            kmat = kss_ref[hk, keys, :]
        live[u]["s"] = _dot_nt(kmat, qs) + bias_ref[hk, par, c0:c0 + width, :]

    def stage_softmax(u):
        hk, _, par = units[u]
        st = live[u]
        s = st.pop("s")
        sink = sink_ref[hk, par, 0:1, :] * _LOG2E
        m = jnp.maximum(jnp.max(s, axis=0, keepdims=True), sink)
        st["e"] = jnp.exp2(s - m).astype(BF16)
        st["esink"] = jnp.exp2(sink - m)

    def stage_values(u):
        hk, n, _ = units[u]
        keys, _, _ = key_range(n)
        live[u]["pv"] = _dot(vt_ref[hk, :, keys], live[u].pop("e"))

    def stage_out(u):
        hk, n, par = units[u]
        st = live[u]
        pv = st.pop("pv")
        st["out"] = pv[:HEAD_DIM] / (pv[HEAD_DIM:HEAD_DIM + 1] + st.pop("esink"))
        if par == 1:
            even, odd = live[u - 1].pop("out"), st.pop("out")
            y_t = jnp.concatenate(
                [even[:, :BLOCK], odd[:, :BLOCK], even[:, BLOCK:], odd[:, BLOCK:]], axis=0)
            o_ref[0, pl.ds(n * BLOCK, BLOCK), pl.ds(hk * _STACK, _STACK)] = \
                jnp.transpose(y_t).astype(BF16)

    stages = (stage_scores, stage_softmax, stage_values, stage_out)
    for i in range(len(units) + _ATTN_STAGE_LAG[-1]):
        for lag, stage in zip(_ATTN_STAGE_LAG, stages):
            if 0 <= i - lag < len(units):
                stage(i - lag)


def _attention(aq, akv, slope_cols, sink_cols):
    b, _, s, qw = aq.shape
    return pl.pallas_call(
        functools.partial(_attn_kernel, seq=s),
        grid=(b,),
        in_specs=[
            pl.BlockSpec((1, ATTN_KV_HEADS, s, qw), lambda i: (i, 0, 0, 0)),
            pl.BlockSpec((1, ATTN_KV_HEADS, s, LANES), lambda i: (i, 0, 0, 0)),
            pl.BlockSpec((ATTN_KV_HEADS, 2, _SUBLANES, _STACK), lambda i: (0, 0, 0, 0)),
            pl.BlockSpec((ATTN_KV_HEADS, 2, _SUBLANES, _STACK), lambda i: (0, 0, 0, 0)),
        ],
        out_specs=pl.BlockSpec((1, s, ATTN_WIDTH), lambda i: (i, 0, 0)),
        out_shape=jax.ShapeDtypeStruct((b, s, ATTN_WIDTH), BF16),
        scratch_shapes=[
            pltpu.VMEM((ATTN_KV_HEADS, s, LANES), BF16),
            pltpu.VMEM((ATTN_KV_HEADS, s, LANES), BF16),
            pltpu.VMEM((ATTN_KV_HEADS, HEAD_DIM + _BF16_ROWS, s), BF16),
            pltpu.VMEM((ATTN_KV_HEADS, 2, _KEYS, _STACK), F32),
        ],
        compiler_params=pltpu.CompilerParams(
            dimension_semantics=("arbitrary",), vmem_limit_bytes=VMEM_LIMIT_BYTES),
        name="attention",
    )(aq, akv, slope_cols, sink_cols)


_FFN_SUB_TILES = 2


def _layer_norm(z, gain, bias):
    mu = jnp.mean(z, axis=-1, keepdims=True)
    dlt = z - mu
    var = jnp.mean(dlt * dlt, axis=-1, keepdims=True)
    return dlt * lax.rsqrt(var + LN_EPS) * gain + bias


def _ffn_kernel(x_ref, yr_ref, ya_ref, p_ref, wo_ref, g1_ref, b1_ref, wg_ref, wu_ref, wd_ref,
                wpe_ref, wpg_ref, g2_ref, b2_ref, o_ref, *, alpha):
    tm = x_ref.shape[0]
    rows = [pl.ds(k * (tm // _FFN_SUB_TILES), tm // _FFN_SUB_TILES) for k in range(_FFN_SUB_TILES)]
    mix = [_dot(yr_ref[r, :], wo_ref[0:RET_WIDTH, :]) + _dot(ya_ref[r, :], wo_ref[RET_WIDTH:, :])
           for r in rows]
    h = [_layer_norm(alpha * x_ref[r, :] + m, g1_ref[...], b1_ref[...]) for r, m in zip(rows, mix)]
    z = []
    for r, hk in zip(rows, h):
        hb = hk.astype(BF16)
        ple = _dot(p_ref[r, :].astype(BF16), wpe_ref[...])
        ple_gate = jax.nn.sigmoid(_dot(hb, wpg_ref[...]))
        rest = alpha * hk + ple_gate * ple
        g = _dot(hb, wg_ref[...])
        up = _dot(hb, wu_ref[...])
        act = (g * jax.nn.sigmoid(g) * up).astype(BF16)
        z.append(rest + _dot(act, wd_ref[...]))
    for r, zk in zip(rows, z):
        o_ref[r, :] = _layer_norm(zk, g2_ref[...], b2_ref[...])


def _ffn(x, yr, ya, p, wo, g1, b1, wg, wu, wd, wpe, wpg, g2, b2, *, alpha, tm):
    t, d = x.shape

    def rows(width):
        return pl.BlockSpec((tm, width), lambda i: (i, 0))

    def resident(arr):
        return pl.BlockSpec(arr.shape, lambda i: (0, 0), pipeline_mode=pl.Buffered(1))

    return pl.pallas_call(
        functools.partial(_ffn_kernel, alpha=alpha),
        grid=(t // tm,),
        in_specs=[rows(d), rows(RET_WIDTH), rows(ATTN_WIDTH), rows(PLE_DIM),
                  resident(wo), resident(g1), resident(b1), resident(wg), resident(wu),
                  resident(wd), resident(wpe), resident(wpg), resident(g2), resident(b2)],
        out_specs=rows(d),
        out_shape=jax.ShapeDtypeStruct((t, d), F32),
        compiler_params=pltpu.CompilerParams(
            dimension_semantics=("arbitrary",), vmem_limit_bytes=VMEM_LIMIT_BYTES),
        name="ffn",
    )(x, yr, ya, p, wo, g1, b1, wg, wu, wd, wpe, wpg, g2, b2)


def _stacked_head_cols(per_head):
    v = per_head.astype(F32).reshape(ATTN_KV_HEADS, 2, 2)
    v = jnp.transpose(v, (0, 2, 1))
    v = jnp.repeat(v, BLOCK, axis=2)
    return jnp.broadcast_to(v[:, :, None, :], v.shape[:2] + (_SUBLANES, _STACK))


def kernel(x, p, w_in, ret_decay_fwd, ret_decay_bwd, ret_gn_gain, attn_sink, w_out, ln1_gain, ln1_bias,
           w_ffn_gate, w_ffn_up, w_ffn_down, w_ple_proj, w_ple_gate, ln2_gain, ln2_bias):
    depth = w_in.shape[0]
    b, s, d = x.shape
    alpha = (2.0 * depth) ** 0.25
    slopes = 2.0 ** (-8.0 * (jnp.arange(ATTN_HEADS, dtype=F32) + 1.0) / ATTN_HEADS)
    slope_cols = _stacked_head_cols(slopes)

    def lane_rep(v):
        v = v.astype(F32).reshape(RET_PAIRS, HEADS_PER_LANE_BLOCK, 1)
        return jnp.broadcast_to(v, (RET_PAIRS, HEADS_PER_LANE_BLOCK, LANES))

    row = lambda v: v.astype(F32).reshape(1, -1)
    h = x.astype(F32)
    for i in range(depth):
        (ret, aq, akv), (wo, wg, wu, wd, wpe, wpg) = _in_proj(
            h, w_in, i, (w_out, w_ffn_gate, w_ffn_up, w_ffn_down, w_ple_proj, w_ple_gate), tm=IN_PROJ_TILE)
        y_ret = _retention(ret, lane_rep(ret_decay_fwd[i]), lane_rep(ret_decay_bwd[i]),
                           ret_gn_gain[i].astype(F32).reshape(RET_PAIRS, 1, LANES))
        y_att = _attention(aq, akv, slope_cols, _stacked_head_cols(attn_sink[i]))
        h = _ffn(h.reshape(b * s, d), y_ret.reshape(b * s, RET_WIDTH), y_att.reshape(b * s, ATTN_WIDTH),
                 p[i].reshape(b * s, PLE_DIM),
                 wo, row(ln1_gain[i]), row(ln1_bias[i]), wg, wu, wd, wpe, wpg,
                 row(ln2_gain[i]), row(ln2_bias[i]), alpha=alpha, tm=FFN_TILE).reshape(b, s, d)
    return h.astype(x.dtype)
```

```python
import functools

import jax
import jax.numpy as jnp
from jax import lax
from jax.experimental import pallas as pl
from jax.experimental.pallas import tpu as pltpu

D_MODEL = 1024
HEAD_DIM = 64
RET_HEADS = 8
ATTN_HEADS = 8
ATTN_KV_HEADS = 2
ATTN_GROUP = ATTN_HEADS // ATTN_KV_HEADS
RET_WIDTH = RET_HEADS * HEAD_DIM
ATTN_WIDTH = ATTN_HEADS * HEAD_DIM
KV_WIDTH = ATTN_KV_HEADS * HEAD_DIM
IN_PROJ_WIDTH = 4 * RET_WIDTH + ATTN_WIDTH + 2 * KV_WIDTH
CHUNK = 128
WINDOW = 128
BLOCK = 128
PLE_DIM = 256
FFN_HIDDEN = ((8 * D_MODEL + 767) // 768) * 256
LN_EPS = 1e-5
GN_EPS = 1e-5
NEG_INF = -1e30
QK_SCALE = HEAD_DIM ** -0.5

LANES = 128
HEADS_PER_LANE_BLOCK = LANES // HEAD_DIM
RET_PAIRS = RET_HEADS // HEADS_PER_LANE_BLOCK
VMEM_LIMIT_BYTES = 56 * 1024 * 1024
IN_PROJ_TILE = 1024
FFN_TILE = 512

F32 = jnp.float32
BF16 = jnp.bfloat16


def _dot(a, b):
    return jnp.dot(a, b, preferred_element_type=F32)


def _dot_nt(a, b):
    return lax.dot_general(a, b, (((1,), (1,)), ((), ())), preferred_element_type=F32)


def _dot_tn(a, b):
    return lax.dot_general(a, b, (((0,), (0,)), ((), ())), preferred_element_type=F32)


_IN_PROJ_COLS = 256
_RET_BLOCKS = 4 * RET_PAIRS
_KV_COL0 = 4 * RET_WIDTH + ATTN_WIDTH
_BF16_ROWS = 16


def _cast_rows_per_step(n_rows, n_steps):
    for rows in range(_BF16_ROWS, n_rows + 1, _BF16_ROWS):
        if n_rows % rows == 0 and n_rows // rows <= n_steps:
            return rows
    raise ValueError(f"no bf16-tile row split of {n_rows} rows over {n_steps} steps")


def _in_proj_kernel(x_ref, w_ref, *refs, n_cast):
    cast_in, (ret_ref, aq_ref, akv_ref) = refs[:n_cast], refs[n_cast:n_cast + 3]
    cast_out, wb_ref = refs[n_cast + 3:2 * n_cast + 3], refs[2 * n_cast + 3]

    @pl.when((pl.program_id(0) == 0) & (pl.program_id(1) == 0))
    def _():
        wb_ref[:, 0:_KV_COL0] = w_ref[:, 0:_KV_COL0].astype(BF16)
        low = lax.broadcasted_iota(jnp.int32, (1, LANES), 1) < HEAD_DIM
        ak = w_ref[:, _KV_COL0:_KV_COL0 + KV_WIDTH]
        av = w_ref[:, _KV_COL0 + KV_WIDTH:_KV_COL0 + 2 * KV_WIDTH]
        wb_ref[:, _KV_COL0:_KV_COL0 + LANES] = jnp.where(
            low, ak, pltpu.roll(av, HEAD_DIM, 1)).astype(BF16)
        wb_ref[:, _KV_COL0 + LANES:_KV_COL0 + 2 * LANES] = jnp.where(
            low, pltpu.roll(ak, HEAD_DIM, 1), av).astype(BF16)

    for src, dst in zip(cast_in, cast_out):
        dst[...] = src[...].astype(BF16)

    xb = x_ref[0].astype(BF16)
    for c in range(IN_PROJ_WIDTH // _IN_PROJ_COLS):
        u = _dot(xb, wb_ref[:, c * _IN_PROJ_COLS:(c + 1) * _IN_PROJ_COLS]).astype(BF16)
        if 2 * c < _RET_BLOCKS:
            ret_ref[0, 2 * c] = u[:, :LANES]
            ret_ref[0, 2 * c + 1] = u[:, LANES:]
        elif c < _RET_BLOCKS // 2 + ATTN_KV_HEADS:
            aq_ref[0, c - _RET_BLOCKS // 2] = u
        else:
            akv_ref[0, 0] = u[:, :LANES]
            akv_ref[0, 1] = u[:, LANES:]


def _cast_specs(cast_weights, layer, n_steps, step_of):
    in_specs, out_specs, out_shapes = [], [], []
    for w in cast_weights:
        rows = _cast_rows_per_step(w.shape[1], n_steps)
        last = w.shape[1] // rows - 1
        in_specs.append(pl.BlockSpec(
            (None, rows, w.shape[2]),
            lambda *idx, last=last: (layer, jnp.minimum(step_of(*idx), last), 0)))
        out_specs.append(pl.BlockSpec(
            (rows, w.shape[2]), lambda *idx, last=last: (jnp.minimum(step_of(*idx), last), 0)))
        out_shapes.append(jax.ShapeDtypeStruct(w.shape[1:], BF16))
    return in_specs, out_specs, out_shapes


def _in_proj(x, w_in, layer, cast_weights, tm):
    b, s, d = x.shape
    n_j = s // tm
    cast_in_specs, cast_out_specs, cast_out_shapes = _cast_specs(
        cast_weights, layer, b * n_j, lambda i, j: i * n_j + j)

    outs = pl.pallas_call(
        functools.partial(_in_proj_kernel, n_cast=len(cast_weights)),
        grid=(b, n_j),
        in_specs=[
            pl.BlockSpec((1, tm, d), lambda i, j: (i, j, 0)),
            pl.BlockSpec((None, d, IN_PROJ_WIDTH), lambda i, j: (layer, 0, 0),
                         pipeline_mode=pl.Buffered(1)),
        ] + cast_in_specs,
        out_specs=[
            pl.BlockSpec((1, _RET_BLOCKS, tm, LANES), lambda i, j: (i, 0, j, 0)),
            pl.BlockSpec((1, ATTN_KV_HEADS, tm, ATTN_GROUP * HEAD_DIM), lambda i, j: (i, 0, j, 0)),
            pl.BlockSpec((1, ATTN_KV_HEADS, tm, LANES), lambda i, j: (i, 0, j, 0)),
        ] + cast_out_specs,
        out_shape=[
            jax.ShapeDtypeStruct((b, _RET_BLOCKS, s, LANES), BF16),
            jax.ShapeDtypeStruct((b, ATTN_KV_HEADS, s, ATTN_GROUP * HEAD_DIM), BF16),
            jax.ShapeDtypeStruct((b, ATTN_KV_HEADS, s, LANES), BF16),
        ] + cast_out_shapes,
        scratch_shapes=[pltpu.VMEM((d, IN_PROJ_WIDTH), BF16)],
        compiler_params=pltpu.CompilerParams(
            dimension_semantics=("arbitrary", "arbitrary"), vmem_limit_bytes=VMEM_LIMIT_BYTES),
        name="in_proj",
    )(x, w_in, *cast_weights)
    return outs[:3], outs[3:]


def _ret_body(q_ref, k_ref, v_ref, g_ref, decf_ref, decb_ref, gain_ref, o_ref, kv_ref, st_ref,
              p_ref, y_ref, *, seq, pp):
    n_chunks = seq // CHUNK
    kv_ref, st_ref, p_ref, y_ref = kv_ref.at[pp], st_ref.at[pp], p_ref.at[pp], y_ref.at[pp]
    out_lanes = pl.ds(pp * LANES, LANES)
    lane = lax.broadcasted_iota(jnp.int32, (1, LANES), 1)
    head0 = lane < HEAD_DIM

    def log_gamma(dec):
        return jnp.log1p(-jnp.exp2(dec))

    lgf0, lgf1 = log_gamma(decf_ref[pp, 0:1, :]), log_gamma(decf_ref[pp, 1:2, :])
    lgb0, lgb1 = log_gamma(decb_ref[pp, 0:1, :]), log_gamma(decb_ref[pp, 1:2, :])
    lgf = jnp.where(head0, lgf0, lgf1)
    lgb = jnp.where(head0, lgb0, lgb1)

    idx = lax.broadcasted_iota(jnp.int32, (CHUNK, LANES), 0).astype(F32)
    qdf = jnp.exp(lgf * (idx + 1.0))
    kdf = jnp.exp(lgf * (CHUNK - 1.0 - idx)) * QK_SCALE
    qdb = jnp.exp(lgb * (CHUNK - idx))
    kdb = jnp.exp(lgb * idx) * QK_SCALE
    cdf = jnp.exp(lgf * CHUNK)
    cdb = jnp.exp(lgb * CHUNK)

    ri = lax.broadcasted_iota(jnp.int32, (CHUNK, CHUNK), 0)
    ci = lax.broadcasted_iota(jnp.int32, (CHUNK, CHUNK), 1)
    diff = (ri - ci).astype(F32)

    def intra_mask(lf, lb):
        fwd = jnp.exp(lf * jnp.maximum(diff, 0.0))
        bwd = jnp.exp(lb * jnp.maximum(-diff, 0.0))
        return jnp.where(diff > 0, fwd, jnp.where(diff < 0, bwd, fwd + bwd))

    mask2 = jnp.concatenate([intra_mask(lgf0, lgb0), intra_mask(lgf1, lgb1)], axis=0) * QK_SCALE
    same_head = (ri < HEAD_DIM) == (ci < HEAD_DIM)
    gn_avg = jnp.where(same_head, 1.0 / HEAD_DIM, 0.0).astype(BF16)

    for n in range(n_chunks):
        sl = pl.ds(n * CHUNK, CHUNK)
        kc = k_ref[0, pp, sl, :].astype(F32)
        kd = jnp.concatenate([(kc * kdf).astype(BF16), (kc * kdb).astype(BF16)], axis=1)
        kv = _dot_tn(kd, v_ref[0, pp, sl, :])
        kv_ref[n] = kv
        yield

    zero_state = jnp.zeros((LANES, LANES), BF16)
    state = jnp.zeros((LANES, LANES), F32)
    for n in range(n_chunks):
        st_ref[n, 0:LANES, :] = jnp.where(same_head, state.astype(BF16), zero_state)
        state = state * cdf + kv_ref[n, 0:LANES, :]
    state = jnp.zeros((LANES, LANES), F32)
    for n in reversed(range(n_chunks)):
        st_ref[n, LANES:2 * LANES, :] = jnp.where(same_head, state.astype(BF16), zero_state)
        state = state * cdb + kv_ref[n, LANES:2 * LANES, :]

    zero = jnp.zeros((CHUNK, LANES), BF16)
    for n in range(n_chunks):
        sl = pl.ds(n * CHUNK, CHUNK)
        qb = q_ref[0, pp, sl, :]
        kb = k_ref[0, pp, sl, :]
        qs = jnp.concatenate([jnp.where(head0, qb, zero), jnp.where(head0, zero, qb)], axis=0)
        p = (_dot_nt(qs, kb) * mask2).astype(BF16)
        p_ref[n, :, 0:CHUNK] = p[0:CHUNK]
        p_ref[n, :, CHUNK:2 * CHUNK] = p[CHUNK:2 * CHUNK]
        yield

    for n in range(n_chunks):
        sl = pl.ds(n * CHUNK, CHUNK)
        vb = v_ref[0, pp, sl, :]
        qc = q_ref[0, pp, sl, :].astype(F32)
        lhs = jnp.concatenate([p_ref[n], (qc * qdf).astype(BF16), (qc * qdb).astype(BF16)], axis=1)
        rhs = jnp.concatenate([jnp.where(head0, vb, zero), jnp.where(head0, zero, vb), st_ref[n]],
                              axis=0)
        y_ref[sl, :] = _dot(lhs, rhs)
        yield

    def group_mean(z):
        return _dot(z.astype(BF16), gn_avg)

    for n in range(n_chunks):
        sl = pl.ds(n * CHUNK, CHUNK)
        y = y_ref[sl, :]
        y_ref[sl, :] = y - group_mean(y)
        yield

    gain = gain_ref[pp]
    for n in range(n_chunks):
        sl = pl.ds(n * CHUNK, CHUNK)
        dlt = y_ref[sl, :]
        var = group_mean(dlt * dlt)
        gate = g_ref[0, pp, sl, :].astype(F32)
        o_ref[0, sl, out_lanes] = (dlt * lax.rsqrt(var + GN_EPS) * gain
                                   * (gate * jax.nn.sigmoid(gate))).astype(BF16)
        yield


_RET_PAIRS_PER_STEP = 4


def _ret_kernel(*refs, seq):
    bodies = [_ret_body(*refs, seq=seq, pp=pp) for pp in range(_RET_PAIRS_PER_STEP)]
    alive = list(bodies)
    while alive:
        for gen in list(alive):
            try:
                next(gen)
            except StopIteration:
                alive.remove(gen)


def _retention(ret, decf, decb, gain):
    b, _, s, _ = ret.shape
    n_chunks = s // CHUNK
    pps = _RET_PAIRS_PER_STEP
    steps = RET_PAIRS // pps

    def part(kind):
        return pl.BlockSpec((1, pps, s, LANES), lambda i, j, kind=kind: (i, kind * steps + j, 0, 0))

    return pl.pallas_call(
        functools.partial(_ret_kernel, seq=s),
        grid=(b, steps),
        in_specs=[
            part(0), part(1), part(2), part(3),
            pl.BlockSpec((pps, HEADS_PER_LANE_BLOCK, LANES), lambda i, j: (j, 0, 0)),
            pl.BlockSpec((pps, HEADS_PER_LANE_BLOCK, LANES), lambda i, j: (j, 0, 0)),
            pl.BlockSpec((pps, 1, LANES), lambda i, j: (j, 0, 0)),
        ],
        out_specs=pl.BlockSpec((1, s, pps * LANES), lambda i, j: (i, 0, j)),
        out_shape=jax.ShapeDtypeStruct((b, s, RET_WIDTH), BF16),
        scratch_shapes=[
            pltpu.VMEM((pps, n_chunks, 2 * LANES, LANES), F32),
            pltpu.VMEM((pps, n_chunks, 2 * LANES, LANES), BF16),
            pltpu.VMEM((pps, n_chunks, CHUNK, 2 * CHUNK), BF16),
            pltpu.VMEM((pps, s, LANES), F32),
        ],
        compiler_params=pltpu.CompilerParams(
            dimension_semantics=("arbitrary", "arbitrary"), vmem_limit_bytes=VMEM_LIMIT_BYTES),
        name="retention",
    )(ret, ret, ret, ret, decf, decb, gain)


_STACK = 2 * BLOCK
_KEYS = 3 * BLOCK
_SUBLANES = 8
_LOG2E = 1.4426950408889634
_ATTN_STAGE_LAG = (0, 2, 4, 5)


def _attn_kernel(q_ref, kv_ref, slope_ref, sink_ref, o_ref, ks_ref, kss_ref, vt_ref, bias_ref,
                 *, seq):
    n_blocks = seq // BLOCK
    lane = lax.broadcasted_iota(jnp.int32, (1, LANES), 1)
    low = lane < HEAD_DIM

    for hk in range(ATTN_KV_HEADS):
        kv = kv_ref[0, hk].astype(F32)
        ks = kv * (QK_SCALE * _LOG2E)
        ks_ref[hk] = ks.astype(BF16)
        kss_ref[hk] = pltpu.roll(ks, HEAD_DIM, 1).astype(BF16)
        kv_t = jnp.transpose(kv)
        vt_ref[hk] = jnp.concatenate(
            [kv_t[HEAD_DIM:], jnp.ones((_BF16_ROWS, seq), F32)], axis=0).astype(BF16)

    kj = lax.broadcasted_iota(jnp.int32, (_KEYS, _STACK), 0)
    c = lax.broadcasted_iota(jnp.int32, (_KEYS, _STACK), 1)
    qi = jnp.where(c < BLOCK, c, c - BLOCK)
    dist = jnp.abs(kj - BLOCK - qi)
    for hk in range(ATTN_KV_HEADS):
        for par in range(2):
            slope = slope_ref[hk, par, 0:1, :]
            bias_ref[hk, par] = jnp.where(dist <= WINDOW, -slope * dist.astype(F32), NEG_INF) * _LOG2E

    zero = jnp.zeros((BLOCK, LANES), BF16)

    def key_range(n):
        lo_blk, hi_blk = max(n - 1, 0), min(n + 2, n_blocks)
        return pl.ds(lo_blk * BLOCK, (hi_blk - lo_blk) * BLOCK), (lo_blk - (n - 1)) * BLOCK, \
            (hi_blk - lo_blk) * BLOCK

    units = [(hk, n, par) for hk in range(ATTN_KV_HEADS) for n in range(n_blocks) for par in range(2)]
    live = [dict() for _ in units]

    def stage_scores(u):
        hk, n, par = units[u]
        keys, c0, width = key_range(n)
        qblk = q_ref[0, hk, pl.ds(n * BLOCK, BLOCK), :]
        qa, qb = qblk[:, :LANES], qblk[:, LANES:]
        if par == 0:
            qs = jnp.concatenate([jnp.where(low, qa, zero), jnp.where(low, qb, zero)], axis=0)
            kmat = ks_ref[hk, keys, :]
        else:
            qs = jnp.concatenate([jnp.where(low, zero, qa), jnp.where(low, zero, qb)], axis=0)
            kmat = kss_ref[hk, keys, :]
        live[u]["s"] = _dot_nt(kmat, qs) + bias_ref[hk, par, c0:c0 + width, :]

    def stage_softmax(u):
        hk, _, par = units[u]
        st = live[u]
        s = st.pop("s")
        sink = sink_ref[hk, par, 0:1, :] * _LOG2E
        m = jnp.maximum(jnp.max(s, axis=0, keepdims=True), sink)
        st["e"] = jnp.exp2(s - m).astype(BF16)
        st["esink"] = jnp.exp2(sink - m)

    def stage_values(u):
        hk, n, _ = units[u]
        keys, _, _ = key_range(n)
        live[u]["pv"] = _dot(vt_ref[hk, :, keys], live[u].pop("e"))

    def stage_out(u):
        hk, n, par = units[u]
        st = live[u]
        pv = st.pop("pv")
        st["out"] = pv[:HEAD_DIM] / (pv[HEAD_DIM:HEAD_DIM + 1] + st.pop("esink"))
        if par == 1:
            even, odd = live[u - 1].pop("out"), st.pop("out")
            y_t = jnp.concatenate(
                [even[:, :BLOCK], odd[:, :BLOCK], even[:, BLOCK:], odd[:, BLOCK:]], axis=0)
            o_ref[0, pl.ds(n * BLOCK, BLOCK), pl.ds(hk * _STACK, _STACK)] = \
                jnp.transpose(y_t).astype(BF16)

    stages = (stage_scores, stage_softmax, stage_values, stage_out)
    for i in range(len(units) + _ATTN_STAGE_LAG[-1]):
        for lag, stage in zip(_ATTN_STAGE_LAG, stages):
            if 0 <= i - lag < len(units):
                stage(i - lag)


def _attention(aq, akv, slope_cols, sink_cols):
    b, _, s, qw = aq.shape
    return pl.pallas_call(
        functools.partial(_attn_kernel, seq=s),
        grid=(b,),
        in_specs=[
            pl.BlockSpec((1, ATTN_KV_HEADS, s, qw), lambda i: (i, 0, 0, 0)),
            pl.BlockSpec((1, ATTN_KV_HEADS, s, LANES), lambda i: (i, 0, 0, 0)),
            pl.BlockSpec((ATTN_KV_HEADS, 2, _SUBLANES, _STACK), lambda i: (0, 0, 0, 0)),
            pl.BlockSpec((ATTN_KV_HEADS, 2, _SUBLANES, _STACK), lambda i: (0, 0, 0, 0)),
        ],
        out_specs=pl.BlockSpec((1, s, ATTN_WIDTH), lambda i: (i, 0, 0)),
        out_shape=jax.ShapeDtypeStruct((b, s, ATTN_WIDTH), BF16),
        scratch_shapes=[
            pltpu.VMEM((ATTN_KV_HEADS, s, LANES), BF16),
            pltpu.VMEM((ATTN_KV_HEADS, s, LANES), BF16),
            pltpu.VMEM((ATTN_KV_HEADS, HEAD_DIM + _BF16_ROWS, s), BF16),
            pltpu.VMEM((ATTN_KV_HEADS, 2, _KEYS, _STACK), F32),
        ],
        compiler_params=pltpu.CompilerParams(
            dimension_semantics=("arbitrary",), vmem_limit_bytes=VMEM_LIMIT_BYTES),
        name="attention",
    )(aq, akv, slope_cols, sink_cols)


_FFN_SUB_TILES = 2


def _layer_norm(z, gain, bias):
    mu = jnp.mean(z, axis=-1, keepdims=True)
    dlt = z - mu
    var = jnp.mean(dlt * dlt, axis=-1, keepdims=True)
    return dlt * lax.rsqrt(var + LN_EPS) * gain + bias


def _ffn_kernel(x_ref, yr_ref, ya_ref, p_ref, wo_ref, g1_ref, b1_ref, wg_ref, wu_ref, wd_ref,
                wpe_ref, wpg_ref, g2_ref, b2_ref, o_ref, *, alpha):
    tm = x_ref.shape[0]
    rows = [pl.ds(k * (tm // _FFN_SUB_TILES), tm // _FFN_SUB_TILES) for k in range(_FFN_SUB_TILES)]
    mix = [_dot(yr_ref[r, :], wo_ref[0:RET_WIDTH, :]) + _dot(ya_ref[r, :], wo_ref[RET_WIDTH:, :])
           for r in rows]
    h = [_layer_norm(alpha * x_ref[r, :] + m, g1_ref[...], b1_ref[...]) for r, m in zip(rows, mix)]
    z = []
    for r, hk in zip(rows, h):
        hb = hk.astype(BF16)
        ple = _dot(p_ref[r, :].astype(BF16), wpe_ref[...])
        ple_gate = jax.nn.sigmoid(_dot(hb, wpg_ref[...]))
        rest = alpha * hk + ple_gate * ple
        g = _dot(hb, wg_ref[...])
        up = _dot(hb, wu_ref[...])
        act = (g * jax.nn.sigmoid(g) * up).astype(BF16)
        z.append(rest + _dot(act, wd_ref[...]))
    for r, zk in zip(rows, z):
        o_ref[r, :] = _layer_norm(zk, g2_ref[...], b2_ref[...])


def _ffn(x, yr, ya, p, wo, g1, b1, wg, wu, wd, wpe, wpg, g2, b2, *, alpha, tm):
    t, d = x.shape

    def rows(width):
        return pl.BlockSpec((tm, width), lambda i: (i, 0))

    def resident(arr):
        return pl.BlockSpec(arr.shape, lambda i: (0, 0), pipeline_mode=pl.Buffered(1))

    return pl.pallas_call(
        functools.partial(_ffn_kernel, alpha=alpha),
        grid=(t // tm,),
        in_specs=[rows(d), rows(RET_WIDTH), rows(ATTN_WIDTH), rows(PLE_DIM),
                  resident(wo), resident(g1), resident(b1), resident(wg), resident(wu),
                  resident(wd), resident(wpe), resident(wpg), resident(g2), resident(b2)],
        out_specs=rows(d),
        out_shape=jax.ShapeDtypeStruct((t, d), F32),
        compiler_params=pltpu.CompilerParams(
            dimension_semantics=("arbitrary",), vmem_limit_bytes=VMEM_LIMIT_BYTES),
        name="ffn",
    )(x, yr, ya, p, wo, g1, b1, wg, wu, wd, wpe, wpg, g2, b2)


def _stacked_head_cols(per_head):
    v = per_head.astype(F32).reshape(ATTN_KV_HEADS, 2, 2)
    v = jnp.transpose(v, (0, 2, 1))
    v = jnp.repeat(v, BLOCK, axis=2)
    return jnp.broadcast_to(v[:, :, None, :], v.shape[:2] + (_SUBLANES, _STACK))


def kernel(x, p, w_in, ret_decay_fwd, ret_decay_bwd, ret_gn_gain, attn_sink, w_out, ln1_gain, ln1_bias,
           w_ffn_gate, w_ffn_up, w_ffn_down, w_ple_proj, w_ple_gate, ln2_gain, ln2_bias):
    depth = w_in.shape[0]
    b, s, d = x.shape
    alpha = (2.0 * depth) ** 0.25
    slopes = 2.0 ** (-8.0 * (jnp.arange(ATTN_HEADS, dtype=F32) + 1.0) / ATTN_HEADS)
    slope_cols = _stacked_head_cols(slopes)

    def lane_rep(v):
        v = v.astype(F32).reshape(RET_PAIRS, HEADS_PER_LANE_BLOCK, 1)
        return jnp.broadcast_to(v, (RET_PAIRS, HEADS_PER_LANE_BLOCK, LANES))

    row = lambda v: v.astype(F32).reshape(1, -1)
    h = x.astype(F32)
    for i in range(depth):
        (ret, aq, akv), (wo, wg, wu, wd, wpe, wpg) = _in_proj(
            h, w_in, i, (w_out, w_ffn_gate, w_ffn_up, w_ffn_down, w_ple_proj, w_ple_gate), tm=IN_PROJ_TILE)
        y_ret = _retention(ret, lane_rep(ret_decay_fwd[i]), lane_rep(ret_decay_bwd[i]),
                           ret_gn_gain[i].astype(F32).reshape(RET_PAIRS, 1, LANES))
        y_att = _attention(aq, akv, slope_cols, _stacked_head_cols(attn_sink[i]))
        h = _ffn(h.reshape(b * s, d), y_ret.reshape(b * s, RET_WIDTH), y_att.reshape(b * s, ATTN_WIDTH),
                 p[i].reshape(b * s, PLE_DIM),
                 wo, row(ln1_gain[i]), row(ln1_bias[i]), wg, wu, wd, wpe, wpg,
                 row(ln2_gain[i]), row(ln2_bias[i]), alpha=alpha, tm=FFN_TILE).reshape(b, s, d)
    return h.astype(x.dtype)
```

```python
import functools

import jax
import jax.numpy as jnp
from jax import lax
from jax.experimental import pallas as pl
from jax.experimental.pallas import tpu as pltpu

D_MODEL = 1024
HEAD_DIM = 64
RET_HEADS = 8
ATTN_HEADS = 8
ATTN_KV_HEADS = 2
ATTN_GROUP = ATTN_HEADS // ATTN_KV_HEADS
RET_WIDTH = RET_HEADS * HEAD_DIM
ATTN_WIDTH = ATTN_HEADS * HEAD_DIM
KV_WIDTH = ATTN_KV_HEADS * HEAD_DIM
IN_PROJ_WIDTH = 4 * RET_WIDTH + ATTN_WIDTH + 2 * KV_WIDTH
CHUNK = 128
WINDOW = 128
BLOCK = 128
PLE_DIM = 256
FFN_HIDDEN = ((8 * D_MODEL + 767) // 768) * 256
LN_EPS = 1e-5
GN_EPS = 1e-5
NEG_INF = -1e30
QK_SCALE = HEAD_DIM ** -0.5

LANES = 128
HEADS_PER_LANE_BLOCK = LANES // HEAD_DIM
RET_PAIRS = RET_HEADS // HEADS_PER_LANE_BLOCK
VMEM_LIMIT_BYTES = 56 * 1024 * 1024
IN_PROJ_TILE = 1024
FFN_TILE = 512

F32 = jnp.float32
BF16 = jnp.bfloat16


def _dot(a, b):
    return jnp.dot(a, b, preferred_element_type=F32)


def _dot_nt(a, b):
    return lax.dot_general(a, b, (((1,), (1,)), ((), ())), preferred_element_type=F32)


def _dot_tn(a, b):
    return lax.dot_general(a, b, (((0,), (0,)), ((), ())), preferred_element_type=F32)


_IN_PROJ_COLS = 256
_RET_BLOCKS = 4 * RET_PAIRS
_KV_COL0 = 4 * RET_WIDTH + ATTN_WIDTH
_BF16_ROWS = 16


def _cast_rows_per_step(n_rows, n_steps):
    for rows in range(_BF16_ROWS, n_rows + 1, _BF16_ROWS):
        if n_rows % rows == 0 and n_rows // rows <= n_steps:
            return rows
    raise ValueError(f"no bf16-tile row split of {n_rows} rows over {n_steps} steps")


def _in_proj_kernel(x_ref, w_ref, *refs, n_cast):
    cast_in, (ret_ref, aq_ref, akv_ref) = refs[:n_cast], refs[n_cast:n_cast + 3]
    cast_out, wb_ref = refs[n_cast + 3:2 * n_cast + 3], refs[2 * n_cast + 3]

    @pl.when((pl.program_id(0) == 0) & (pl.program_id(1) == 0))
    def _():
        wb_ref[:, 0:_KV_COL0] = w_ref[:, 0:_KV_COL0].astype(BF16)
        low = lax.broadcasted_iota(jnp.int32, (1, LANES), 1) < HEAD_DIM
        ak = w_ref[:, _KV_COL0:_KV_COL0 + KV_WIDTH]
        av = w_ref[:, _KV_COL0 + KV_WIDTH:_KV_COL0 + 2 * KV_WIDTH]
        wb_ref[:, _KV_COL0:_KV_COL0 + LANES] = jnp.where(
            low, ak, pltpu.roll(av, HEAD_DIM, 1)).astype(BF16)
        wb_ref[:, _KV_COL0 + LANES:_KV_COL0 + 2 * LANES] = jnp.where(
            low, pltpu.roll(ak, HEAD_DIM, 1), av).astype(BF16)

    for src, dst in zip(cast_in, cast_out):
        dst[...] = src[...].astype(BF16)

    xb = x_ref[0].astype(BF16)
    for c in range(IN_PROJ_WIDTH // _IN_PROJ_COLS):
        u = _dot(xb, wb_ref[:, c * _IN_PROJ_COLS:(c + 1) * _IN_PROJ_COLS]).astype(BF16)
        if 2 * c < _RET_BLOCKS:
            ret_ref[0, 2 * c] = u[:, :LANES]
            ret_ref[0, 2 * c + 1] = u[:, LANES:]
        elif c < _RET_BLOCKS // 2 + ATTN_KV_HEADS:
            aq_ref[0, c - _RET_BLOCKS // 2] = u
        else:
            akv_ref[0, 0] = u[:, :LANES]
            akv_ref[0, 1] = u[:, LANES:]


def _cast_specs(cast_weights, layer, n_steps, step_of):
    in_specs, out_specs, out_shapes = [], [], []
    for w in cast_weights:
        rows = _cast_rows_per_step(w.shape[1], n_steps)
        last = w.shape[1] // rows - 1
        in_specs.append(pl.BlockSpec(
            (None, rows, w.shape[2]),
            lambda *idx, last=last: (layer, jnp.minimum(step_of(*idx), last), 0)))
        out_specs.append(pl.BlockSpec(
            (rows, w.shape[2]), lambda *idx, last=last: (jnp.minimum(step_of(*idx), last), 0)))
        out_shapes.append(jax.ShapeDtypeStruct(w.shape[1:], BF16))
    return in_specs, out_specs, out_shapes


def _in_proj(x, w_in, layer, cast_weights, tm):
    b, s, d = x.shape
    n_j = s // tm
    cast_in_specs, cast_out_specs, cast_out_shapes = _cast_specs(
        cast_weights, layer, b * n_j, lambda i, j: i * n_j + j)

    outs = pl.pallas_call(
        functools.partial(_in_proj_kernel, n_cast=len(cast_weights)),
        grid=(b, n_j),
        in_specs=[
            pl.BlockSpec((1, tm, d), lambda i, j: (i, j, 0)),
            pl.BlockSpec((None, d, IN_PROJ_WIDTH), lambda i, j: (layer, 0, 0),
                         pipeline_mode=pl.Buffered(1)),
        ] + cast_in_specs,
        out_specs=[
            pl.BlockSpec((1, _RET_BLOCKS, tm, LANES), lambda i, j: (i, 0, j, 0)),
            pl.BlockSpec((1, ATTN_KV_HEADS, tm, ATTN_GROUP * HEAD_DIM), lambda i, j: (i, 0, j, 0)),
            pl.BlockSpec((1, ATTN_KV_HEADS, tm, LANES), lambda i, j: (i, 0, j, 0)),
        ] + cast_out_specs,
        out_shape=[
            jax.ShapeDtypeStruct((b, _RET_BLOCKS, s, LANES), BF16),
            jax.ShapeDtypeStruct((b, ATTN_KV_HEADS, s, ATTN_GROUP * HEAD_DIM), BF16),
            jax.ShapeDtypeStruct((b, ATTN_KV_HEADS, s, LANES), BF16),
        ] + cast_out_shapes,
        scratch_shapes=[pltpu.VMEM((d, IN_PROJ_WIDTH), BF16)],
        compiler_params=pltpu.CompilerParams(
            dimension_semantics=("arbitrary", "arbitrary"), vmem_limit_bytes=VMEM_LIMIT_BYTES),
        name="in_proj",
    )(x, w_in, *cast_weights)
    return outs[:3], outs[3:]


def _ret_body(q_ref, k_ref, v_ref, g_ref, decf_ref, decb_ref, gain_ref, o_ref, kv_ref, st_ref,
              p_ref, y_ref, *, seq, pp):
    n_chunks = seq // CHUNK
    kv_ref, st_ref, p_ref, y_ref = kv_ref.at[pp], st_ref.at[pp], p_ref.at[pp], y_ref.at[pp]
    out_lanes = pl.ds(pp * LANES, LANES)
    lane = lax.broadcasted_iota(jnp.int32, (1, LANES), 1)
    head0 = lane < HEAD_DIM

    def log_gamma(dec):
        return jnp.log1p(-jnp.exp2(dec))

    lgf0, lgf1 = log_gamma(decf_ref[pp, 0:1, :]), log_gamma(decf_ref[pp, 1:2, :])
    lgb0, lgb1 = log_gamma(decb_ref[pp, 0:1, :]), log_gamma(decb_ref[pp, 1:2, :])
    lgf = jnp.where(head0, lgf0, lgf1)
    lgb = jnp.where(head0, lgb0, lgb1)

    idx = lax.broadcasted_iota(jnp.int32, (CHUNK, LANES), 0).astype(F32)
    qdf = jnp.exp(lgf * (idx + 1.0))
    kdf = jnp.exp(lgf * (CHUNK - 1.0 - idx)) * QK_SCALE
    qdb = jnp.exp(lgb * (CHUNK - idx))
    kdb = jnp.exp(lgb * idx) * QK_SCALE
    cdf = jnp.exp(lgf * CHUNK)
    cdb = jnp.exp(lgb * CHUNK)

    ri = lax.broadcasted_iota(jnp.int32, (CHUNK, CHUNK), 0)
    ci = lax.broadcasted_iota(jnp.int32, (CHUNK, CHUNK), 1)
    diff = (ri - ci).astype(F32)

    def intra_mask(lf, lb):
        fwd = jnp.exp(lf * jnp.maximum(diff, 0.0))
        bwd = jnp.exp(lb * jnp.maximum(-diff, 0.0))
        return jnp.where(diff > 0, fwd, jnp.where(diff < 0, bwd, fwd + bwd))

    mask2 = jnp.concatenate([intra_mask(lgf0, lgb0), intra_mask(lgf1, lgb1)], axis=0) * QK_SCALE
    same_head = (ri < HEAD_DIM) == (ci < HEAD_DIM)
    gn_avg = jnp.where(same_head, 1.0 / HEAD_DIM, 0.0).astype(BF16)

    for n in range(n_chunks):
        sl = pl.ds(n * CHUNK, CHUNK)
        kc = k_ref[0, pp, sl, :].astype(F32)
        kd = jnp.concatenate([(kc * kdf).astype(BF16), (kc * kdb).astype(BF16)], axis=1)
        kv = _dot_tn(kd, v_ref[0, pp, sl, :])
        kv_ref[n] = kv
        yield

    zero_state = jnp.zeros((LANES, LANES), BF16)
    state = jnp.zeros((LANES, LANES), F32)
    for n in range(n_chunks):
        st_ref[n, 0:LANES, :] = jnp.where(same_head, state.astype(BF16), zero_state)
        state = state * cdf + kv_ref[n, 0:LANES, :]
    state = jnp.zeros((LANES, LANES), F32)
    for n in reversed(range(n_chunks)):
        st_ref[n, LANES:2 * LANES, :] = jnp.where(same_head, state.astype(BF16), zero_state)
        state = state * cdb + kv_ref[n, LANES:2 * LANES, :]

    zero = jnp.zeros((CHUNK, LANES), BF16)
    for n in range(n_chunks):
        sl = pl.ds(n * CHUNK, CHUNK)
        qb = q_ref[0, pp, sl, :]
        kb = k_ref[0, pp, sl, :]
        qs = jnp.concatenate([jnp.where(head0, qb, zero), jnp.where(head0, zero, qb)], axis=0)
        p = (_dot_nt(qs, kb) * mask2).astype(BF16)
        p_ref[n, :, 0:CHUNK] = p[0:CHUNK]
        p_ref[n, :, CHUNK:2 * CHUNK] = p[CHUNK:2 * CHUNK]
        yield

    for n in range(n_chunks):
        sl = pl.ds(n * CHUNK, CHUNK)
        vb = v_ref[0, pp, sl, :]
        qc = q_ref[0, pp, sl, :].astype(F32)
        lhs = jnp.concatenate([p_ref[n], (qc * qdf).astype(BF16), (qc * qdb).astype(BF16)], axis=1)
        rhs = jnp.concatenate([jnp.where(head0, vb, zero), jnp.where(head0, zero, vb), st_ref[n]],
                              axis=0)
        y_ref[sl, :] = _dot(lhs, rhs)
        yield

    def group_mean(z):
        return _dot(z.astype(BF16), gn_avg)

    for n in range(n_chunks):
        sl = pl.ds(n * CHUNK, CHUNK)
        y = y_ref[sl, :]
        y_ref[sl, :] = y - group_mean(y)
        yield

    gain = gain_ref[pp]
    for n in range(n_chunks):
        sl = pl.ds(n * CHUNK, CHUNK)
        dlt = y_ref[sl, :]
        var = group_mean(dlt * dlt)
        gate = g_ref[0, pp, sl, :].astype(F32)
        o_ref[0, sl, out_lanes] = (dlt * lax.rsqrt(var + GN_EPS) * gain
                                   * (gate * jax.nn.sigmoid(gate))).astype(BF16)
        yield


_RET_PAIRS_PER_STEP = 4


def _ret_kernel(*refs, seq):
    bodies = [_ret_body(*refs, seq=seq, pp=pp) for pp in range(_RET_PAIRS_PER_STEP)]
    alive = list(bodies)
    while alive:
        for gen in list(alive):
            try:
                next(gen)
            except StopIteration:
                alive.remove(gen)


def _retention(ret, decf, decb, gain):
    b, _, s, _ = ret.shape
    n_chunks = s // CHUNK
    pps = _RET_PAIRS_PER_STEP
    steps = RET_PAIRS // pps

    def part(kind):
        return pl.BlockSpec((1, pps, s, LANES), lambda i, j, kind=kind: (i, kind * steps + j, 0, 0))

    return pl.pallas_call(
        functools.partial(_ret_kernel, seq=s),
        grid=(b, steps),
        in_specs=[
            part(0), part(1), part(2), part(3),
            pl.BlockSpec((pps, HEADS_PER_LANE_BLOCK, LANES), lambda i, j: (j, 0, 0)),
            pl.BlockSpec((pps, HEADS_PER_LANE_BLOCK, LANES), lambda i, j: (j, 0, 0)),
            pl.BlockSpec((pps, 1, LANES), lambda i, j: (j, 0, 0)),
        ],
        out_specs=pl.BlockSpec((1, s, pps * LANES), lambda i, j: (i, 0, j)),
        out_shape=jax.ShapeDtypeStruct((b, s, RET_WIDTH), BF16),
        scratch_shapes=[
            pltpu.VMEM((pps, n_chunks, 2 * LANES, LANES), F32),
            pltpu.VMEM((pps, n_chunks, 2 * LANES, LANES), BF16),
            pltpu.VMEM((pps, n_chunks, CHUNK, 2 * CHUNK), BF16),
            pltpu.VMEM((pps, s, LANES), F32),
        ],
        compiler_params=pltpu.CompilerParams(
            dimension_semantics=("arbitrary", "arbitrary"), vmem_limit_bytes=VMEM_LIMIT_BYTES),
        name="retention",
    )(ret, ret, ret, ret, decf, decb, gain)


_STACK = 2 * BLOCK
_KEYS = 3 * BLOCK
_SUBLANES = 8
_LOG2E = 1.4426950408889634
_ATTN_STAGE_LAG = (0, 2, 4, 5)


def _attn_kernel(q_ref, kv_ref, slope_ref, sink_ref, o_ref, ks_ref, kss_ref, vt_ref, bias_ref,
                 *, seq):
    n_blocks = seq // BLOCK
    lane = lax.broadcasted_iota(jnp.int32, (1, LANES), 1)
    low = lane < HEAD_DIM

    for hk in range(ATTN_KV_HEADS):
        kv = kv_ref[0, hk].astype(F32)
        ks = kv * (QK_SCALE * _LOG2E)
        ks_ref[hk] = ks.astype(BF16)
        kss_ref[hk] = pltpu.roll(ks, HEAD_DIM, 1).astype(BF16)
        kv_t = jnp.transpose(kv)
        vt_ref[hk] = jnp.concatenate(
            [kv_t[HEAD_DIM:], jnp.ones((_BF16_ROWS, seq), F32)], axis=0).astype(BF16)

    kj = lax.broadcasted_iota(jnp.int32, (_KEYS, _STACK), 0)
    c = lax.broadcasted_iota(jnp.int32, (_KEYS, _STACK), 1)
    qi = jnp.where(c < BLOCK, c, c - BLOCK)
    dist = jnp.abs(kj - BLOCK - qi)
    for hk in range(ATTN_KV_HEADS):
        for par in range(2):
            slope = slope_ref[hk, par, 0:1, :]
            bias_ref[hk, par] = jnp.where(dist <= WINDOW, -slope * dist.astype(F32), NEG_INF) * _LOG2E

    zero = jnp.zeros((BLOCK, LANES), BF16)

    def key_range(n):
        lo_blk, hi_blk = max(n - 1, 0), min(n + 2, n_blocks)
        return pl.ds(lo_blk * BLOCK, (hi_blk - lo_blk) * BLOCK), (lo_blk - (n - 1)) * BLOCK, \
            (hi_blk - lo_blk) * BLOCK

    units = [(hk, n, par) for hk in range(ATTN_KV_HEADS) for n in range(n_blocks) for par in range(2)]
    live = [dict() for _ in units]

    def stage_scores(u):
        hk, n, par = units[u]
        keys, c0, width = key_range(n)
        qblk = q_ref[0, hk, pl.ds(n * BLOCK, BLOCK), :]
        qa, qb = qblk[:, :LANES], qblk[:, LANES:]
        if par == 0:
            qs = jnp.concatenate([jnp.where(low, qa, zero), jnp.where(low, qb, zero)], axis=0)
            kmat = ks_ref[hk, keys, :]
        else:
            qs = jnp.concatenate([jnp.where(low, zero, qa), jnp.where(low, zero, qb)], axis=0)
            kmat = kss_ref[hk, keys, :]
        live[u]["s"] = _dot_nt(kmat, qs) + bias_ref[hk, par, c0:c0 + width, :]

    def stage_softmax(u):
        hk, _, par = units[u]
        st = live[u]
        s = st.pop("s")
        sink = sink_ref[hk, par, 0:1, :] * _LOG2E
        m = jnp.maximum(jnp.max(s, axis=0, keepdims=True), sink)
        st["e"] = jnp.exp2(s - m).astype(BF16)
        st["esink"] = jnp.exp2(sink - m)

    def stage_values(u):
        hk, n, _ = units[u]
        keys, _, _ = key_range(n)
        live[u]["pv"] = _dot(vt_ref[hk, :, keys], live[u].pop("e"))

    def stage_out(u):
        hk, n, par = units[u]
        st = live[u]
        pv = st.pop("pv")
        st["out"] = pv[:HEAD_DIM] / (pv[HEAD_DIM:HEAD_DIM + 1] + st.pop("esink"))
        if par == 1:
            even, odd = live[u - 1].pop("out"), st.pop("out")
            y_t = jnp.concatenate(
                [even[:, :BLOCK], odd[:, :BLOCK], even[:, BLOCK:], odd[:, BLOCK:]], axis=0)
            o_ref[0, pl.ds(n * BLOCK, BLOCK), pl.ds(hk * _STACK, _STACK)] = \
                jnp.transpose(y_t).astype(BF16)

    stages = (stage_scores, stage_softmax, stage_values, stage_out)
    for i in range(len(units) + _ATTN_STAGE_LAG[-1]):
        for lag, stage in zip(_ATTN_STAGE_LAG, stages):
            if 0 <= i - lag < len(units):
                stage(i - lag)


def _attention(aq, akv, slope_cols, sink_cols):
    b, _, s, qw = aq.shape
    return pl.pallas_call(
        functools.partial(_attn_kernel, seq=s),
        grid=(b,),
        in_specs=[
            pl.BlockSpec((1, ATTN_KV_HEADS, s, qw), lambda i: (i, 0, 0, 0)),
            pl.BlockSpec((1, ATTN_KV_HEADS, s, LANES), lambda i: (i, 0, 0, 0)),
            pl.BlockSpec((ATTN_KV_HEADS, 2, _SUBLANES, _STACK), lambda i: (0, 0, 0, 0)),
            pl.BlockSpec((ATTN_KV_HEADS, 2, _SUBLANES, _STACK), lambda i: (0, 0, 0, 0)),
        ],
        out_specs=pl.BlockSpec((1, s, ATTN_WIDTH), lambda i: (i, 0, 0)),
        out_shape=jax.ShapeDtypeStruct((b, s, ATTN_WIDTH), BF16),
        scratch_shapes=[
            pltpu.VMEM((ATTN_KV_HEADS, s, LANES), BF16),
            pltpu.VMEM((ATTN_KV_HEADS, s, LANES), BF16),
            pltpu.VMEM((ATTN_KV_HEADS, HEAD_DIM + _BF16_ROWS, s), BF16),
            pltpu.VMEM((ATTN_KV_HEADS, 2, _KEYS, _STACK), F32),
        ],
        compiler_params=pltpu.CompilerParams(
            dimension_semantics=("arbitrary",), vmem_limit_bytes=VMEM_LIMIT_BYTES),
        name="attention",
    )(aq, akv, slope_cols, sink_cols)


_FFN_SUB_TILES = 2
_FFN_HIDDEN_CHUNK = 256


def _layer_norm(z, gain, bias):
    mu = jnp.mean(z, axis=-1, keepdims=True)
    dlt = z - mu
    var = jnp.mean(dlt * dlt, axis=-1, keepdims=True)
    return dlt * lax.rsqrt(var + LN_EPS) * gain + bias


def _ffn_kernel(x_ref, yr_ref, ya_ref, p_ref, wo_ref, g1_ref, b1_ref, wg_ref, wu_ref, wd_ref,
                wpe_ref, wpg_ref, g2_ref, b2_ref, o_ref, *, alpha):
    tm = x_ref.shape[0]
    rows = [pl.ds(k * (tm // _FFN_SUB_TILES), tm // _FFN_SUB_TILES) for k in range(_FFN_SUB_TILES)]
    mix = [_dot(yr_ref[r, :], wo_ref[0:RET_WIDTH, :]) + _dot(ya_ref[r, :], wo_ref[RET_WIDTH:, :])
           for r in rows]
    h = [_layer_norm(alpha * x_ref[r, :] + m, g1_ref[...], b1_ref[...]) for r, m in zip(rows, mix)]
    z = []
    for r, hk in zip(rows, h):
        hb = hk.astype(BF16)
        ple = _dot(p_ref[r, :].astype(BF16), wpe_ref[...])
        ple_gate = jax.nn.sigmoid(_dot(hb, wpg_ref[...]))
        rest = alpha * hk + ple_gate * ple
        acts, acc = [], rest
        for c in range(FFN_HIDDEN // _FFN_HIDDEN_CHUNK + 1):
            if c < FFN_HIDDEN // _FFN_HIDDEN_CHUNK:
                cols = pl.ds(c * _FFN_HIDDEN_CHUNK, _FFN_HIDDEN_CHUNK)
                g = _dot(hb, wg_ref[:, cols])
                up = _dot(hb, wu_ref[:, cols])
                acts.append((g * jax.nn.sigmoid(g) * up).astype(BF16))
            if c >= 1:
                acc = acc + _dot(acts[c - 1], wd_ref[pl.ds((c - 1) * _FFN_HIDDEN_CHUNK, _FFN_HIDDEN_CHUNK), :])
        z.append(acc)
    for r, zk in zip(rows, z):
        o_ref[r, :] = _layer_norm(zk, g2_ref[...], b2_ref[...])


def _ffn(x, yr, ya, p, wo, g1, b1, wg, wu, wd, wpe, wpg, g2, b2, *, alpha, tm):
    t, d = x.shape

    def rows(width):
        return pl.BlockSpec((tm, width), lambda i: (i, 0))

    def resident(arr):
        return pl.BlockSpec(arr.shape, lambda i: (0, 0), pipeline_mode=pl.Buffered(1))

    return pl.pallas_call(
        functools.partial(_ffn_kernel, alpha=alpha),
        grid=(t // tm,),
        in_specs=[rows(d), rows(RET_WIDTH), rows(ATTN_WIDTH), rows(PLE_DIM),
                  resident(wo), resident(g1), resident(b1), resident(wg), resident(wu),
                  resident(wd), resident(wpe), resident(wpg), resident(g2), resident(b2)],
        out_specs=rows(d),
        out_shape=jax.ShapeDtypeStruct((t, d), F32),
        compiler_params=pltpu.CompilerParams(
            dimension_semantics=("arbitrary",), vmem_limit_bytes=VMEM_LIMIT_BYTES),
        name="ffn",
    )(x, yr, ya, p, wo, g1, b1, wg, wu, wd, wpe, wpg, g2, b2)


def _stacked_head_cols(per_head):
    v = per_head.astype(F32).reshape(ATTN_KV_HEADS, 2, 2)
    v = jnp.transpose(v, (0, 2, 1))
    v = jnp.repeat(v, BLOCK, axis=2)
    return jnp.broadcast_to(v[:, :, None, :], v.shape[:2] + (_SUBLANES, _STACK))


def kernel(x, p, w_in, ret_decay_fwd, ret_decay_bwd, ret_gn_gain, attn_sink, w_out, ln1_gain, ln1_bias,
           w_ffn_gate, w_ffn_up, w_ffn_down, w_ple_proj, w_ple_gate, ln2_gain, ln2_bias):
    depth = w_in.shape[0]
    b, s, d = x.shape
    alpha = (2.0 * depth) ** 0.25
    slopes = 2.0 ** (-8.0 * (jnp.arange(ATTN_HEADS, dtype=F32) + 1.0) / ATTN_HEADS)
    slope_cols = _stacked_head_cols(slopes)

    def lane_rep(v):
        v = v.astype(F32).reshape(RET_PAIRS, HEADS_PER_LANE_BLOCK, 1)
        return jnp.broadcast_to(v, (RET_PAIRS, HEADS_PER_LANE_BLOCK, LANES))

    row = lambda v: v.astype(F32).reshape(1, -1)
    h = x.astype(F32)
    for i in range(depth):
        (ret, aq, akv), (wo, wg, wu, wd, wpe, wpg) = _in_proj(
            h, w_in, i, (w_out, w_ffn_gate, w_ffn_up, w_ffn_down, w_ple_proj, w_ple_gate), tm=IN_PROJ_TILE)
        y_ret = _retention(ret, lane_rep(ret_decay_fwd[i]), lane_rep(ret_decay_bwd[i]),
                           ret_gn_gain[i].astype(F32).reshape(RET_PAIRS, 1, LANES))
        y_att = _attention(aq, akv, slope_cols, _stacked_head_cols(attn_sink[i]))
        h = _ffn(h.reshape(b * s, d), y_ret.reshape(b * s, RET_WIDTH), y_att.reshape(b * s, ATTN_WIDTH),
                 p[i].reshape(b * s, PLE_DIM),
                 wo, row(ln1_gain[i]), row(ln1_bias[i]), wg, wu, wd, wpe, wpg,
                 row(ln2_gain[i]), row(ln2_bias[i]), alpha=alpha, tm=FFN_TILE).reshape(b, s, d)
    return h.astype(x.dtype)
```

```python
import functools

import jax
import jax.numpy as jnp
from jax import lax
from jax.experimental import pallas as pl
from jax.experimental.pallas import tpu as pltpu

D_MODEL = 1024
HEAD_DIM = 64
RET_HEADS = 8
ATTN_HEADS = 8
ATTN_KV_HEADS = 2
ATTN_GROUP = ATTN_HEADS // ATTN_KV_HEADS
RET_WIDTH = RET_HEADS * HEAD_DIM
ATTN_WIDTH = ATTN_HEADS * HEAD_DIM
KV_WIDTH = ATTN_KV_HEADS * HEAD_DIM
IN_PROJ_WIDTH = 4 * RET_WIDTH + ATTN_WIDTH + 2 * KV_WIDTH
CHUNK = 128
WINDOW = 128
BLOCK = 128
PLE_DIM = 256
FFN_HIDDEN = ((8 * D_MODEL + 767) // 768) * 256
LN_EPS = 1e-5
GN_EPS = 1e-5
NEG_INF = -1e30
QK_SCALE = HEAD_DIM ** -0.5

LANES = 128
HEADS_PER_LANE_BLOCK = LANES // HEAD_DIM
RET_PAIRS = RET_HEADS // HEADS_PER_LANE_BLOCK
VMEM_LIMIT_BYTES = 56 * 1024 * 1024
IN_PROJ_TILE = 1024
FFN_TILE = 512

F32 = jnp.float32
BF16 = jnp.bfloat16


def _dot(a, b):
    return jnp.dot(a, b, preferred_element_type=F32)


def _dot_nt(a, b):
    return lax.dot_general(a, b, (((1,), (1,)), ((), ())), preferred_element_type=F32)


def _dot_tn(a, b):
    return lax.dot_general(a, b, (((0,), (0,)), ((), ())), preferred_element_type=F32)


_IN_PROJ_COLS = 256
_RET_BLOCKS = 4 * RET_PAIRS
_KV_COL0 = 4 * RET_WIDTH + ATTN_WIDTH
_BF16_ROWS = 16


def _cast_rows_per_step(n_rows, n_steps):
    for rows in range(_BF16_ROWS, n_rows + 1, _BF16_ROWS):
        if n_rows % rows == 0 and n_rows // rows <= n_steps:
            return rows
    raise ValueError(f"no bf16-tile row split of {n_rows} rows over {n_steps} steps")


def _in_proj_kernel(x_ref, w_ref, *refs, n_cast):
    cast_in, (ret_ref, aq_ref, akv_ref) = refs[:n_cast], refs[n_cast:n_cast + 3]
    cast_out, wb_ref = refs[n_cast + 3:2 * n_cast + 3], refs[2 * n_cast + 3]

    @pl.when((pl.program_id(0) == 0) & (pl.program_id(1) == 0))
    def _():
        wb_ref[:, 0:_KV_COL0] = w_ref[:, 0:_KV_COL0].astype(BF16)
        low = lax.broadcasted_iota(jnp.int32, (1, LANES), 1) < HEAD_DIM
        ak = w_ref[:, _KV_COL0:_KV_COL0 + KV_WIDTH]
        av = w_ref[:, _KV_COL0 + KV_WIDTH:_KV_COL0 + 2 * KV_WIDTH]
        wb_ref[:, _KV_COL0:_KV_COL0 + LANES] = jnp.where(
            low, ak, pltpu.roll(av, HEAD_DIM, 1)).astype(BF16)
        wb_ref[:, _KV_COL0 + LANES:_KV_COL0 + 2 * LANES] = jnp.where(
            low, pltpu.roll(ak, HEAD_DIM, 1), av).astype(BF16)

    for src, dst in zip(cast_in, cast_out):
        dst[...] = src[...].astype(BF16)

    xb = x_ref[0].astype(BF16)
    for c in range(IN_PROJ_WIDTH // _IN_PROJ_COLS):
        u = _dot(xb, wb_ref[:, c * _IN_PROJ_COLS:(c + 1) * _IN_PROJ_COLS]).astype(BF16)
        if 2 * c < _RET_BLOCKS:
            ret_ref[0, 2 * c] = u[:, :LANES]
            ret_ref[0, 2 * c + 1] = u[:, LANES:]
        elif c < _RET_BLOCKS // 2 + ATTN_KV_HEADS:
            aq_ref[0, c - _RET_BLOCKS // 2] = u
        else:
            akv_ref[0, 0] = u[:, :LANES]
            akv_ref[0, 1] = u[:, LANES:]


def _cast_specs(cast_weights, layer, n_steps, step_of):
    in_specs, out_specs, out_shapes = [], [], []
    for w in cast_weights:
        rows = _cast_rows_per_step(w.shape[1], n_steps)
        last = w.shape[1] // rows - 1
        in_specs.append(pl.BlockSpec(
            (None, rows, w.shape[2]),
            lambda *idx, last=last: (layer, jnp.minimum(step_of(*idx), last), 0)))
        out_specs.append(pl.BlockSpec(
            (rows, w.shape[2]), lambda *idx, last=last: (jnp.minimum(step_of(*idx), last), 0)))
        out_shapes.append(jax.ShapeDtypeStruct(w.shape[1:], BF16))
    return in_specs, out_specs, out_shapes


def _in_proj(x, w_in, layer, cast_weights, tm):
    b, s, d = x.shape
    n_j = s // tm
    cast_in_specs, cast_out_specs, cast_out_shapes = _cast_specs(
        cast_weights, layer, b * n_j, lambda i, j: i * n_j + j)

    outs = pl.pallas_call(
        functools.partial(_in_proj_kernel, n_cast=len(cast_weights)),
        grid=(b, n_j),
        in_specs=[
            pl.BlockSpec((1, tm, d), lambda i, j: (i, j, 0)),
            pl.BlockSpec((None, d, IN_PROJ_WIDTH), lambda i, j: (layer, 0, 0),
                         pipeline_mode=pl.Buffered(1)),
        ] + cast_in_specs,
        out_specs=[
            pl.BlockSpec((1, _RET_BLOCKS, tm, LANES), lambda i, j: (i, 0, j, 0)),
            pl.BlockSpec((1, ATTN_KV_HEADS, tm, ATTN_GROUP * HEAD_DIM), lambda i, j: (i, 0, j, 0)),
            pl.BlockSpec((1, ATTN_KV_HEADS, tm, LANES), lambda i, j: (i, 0, j, 0)),
        ] + cast_out_specs,
        out_shape=[
            jax.ShapeDtypeStruct((b, _RET_BLOCKS, s, LANES), BF16),
            jax.ShapeDtypeStruct((b, ATTN_KV_HEADS, s, ATTN_GROUP * HEAD_DIM), BF16),
            jax.ShapeDtypeStruct((b, ATTN_KV_HEADS, s, LANES), BF16),
        ] + cast_out_shapes,
        scratch_shapes=[pltpu.VMEM((d, IN_PROJ_WIDTH), BF16)],
        compiler_params=pltpu.CompilerParams(
            dimension_semantics=("arbitrary", "arbitrary"), vmem_limit_bytes=VMEM_LIMIT_BYTES),
        name="in_proj",
    )(x, w_in, *cast_weights)
    return outs[:3], outs[3:]


def _ret_body(q_ref, k_ref, v_ref, g_ref, decf_ref, decb_ref, gain_ref, o_ref, kv_ref, st_ref,
              p_ref, y_ref, *, seq, pp):
    n_chunks = seq // CHUNK
    kv_ref, st_ref, p_ref, y_ref = kv_ref.at[pp], st_ref.at[pp], p_ref.at[pp], y_ref.at[pp]
    out_lanes = pl.ds(pp * LANES, LANES)
    lane = lax.broadcasted_iota(jnp.int32, (1, LANES), 1)
    head0 = lane < HEAD_DIM

    def log_gamma(dec):
        return jnp.log1p(-jnp.exp2(dec))

    lgf0, lgf1 = log_gamma(decf_ref[pp, 0:1, :]), log_gamma(decf_ref[pp, 1:2, :])
    lgb0, lgb1 = log_gamma(decb_ref[pp, 0:1, :]), log_gamma(decb_ref[pp, 1:2, :])
    lgf = jnp.where(head0, lgf0, lgf1)
    lgb = jnp.where(head0, lgb0, lgb1)

    idx = lax.broadcasted_iota(jnp.int32, (CHUNK, LANES), 0).astype(F32)
    qdf = jnp.exp(lgf * (idx + 1.0))
    kdf = jnp.exp(lgf * (CHUNK - 1.0 - idx)) * QK_SCALE
    qdb = jnp.exp(lgb * (CHUNK - idx))
    kdb = jnp.exp(lgb * idx) * QK_SCALE
    cdf = jnp.exp(lgf * CHUNK)
    cdb = jnp.exp(lgb * CHUNK)

    ri = lax.broadcasted_iota(jnp.int32, (CHUNK, CHUNK), 0)
    ci = lax.broadcasted_iota(jnp.int32, (CHUNK, CHUNK), 1)
    diff = (ri - ci).astype(F32)

    def intra_mask(lf, lb):
        fwd = jnp.exp(lf * jnp.maximum(diff, 0.0))
        bwd = jnp.exp(lb * jnp.maximum(-diff, 0.0))
        return jnp.where(diff > 0, fwd, jnp.where(diff < 0, bwd, fwd + bwd))

    mask2 = jnp.concatenate([intra_mask(lgf0, lgb0), intra_mask(lgf1, lgb1)], axis=0) * QK_SCALE
    same_head = (ri < HEAD_DIM) == (ci < HEAD_DIM)
    gn_avg = jnp.where(same_head, 1.0 / HEAD_DIM, 0.0).astype(BF16)

    for n in range(n_chunks):
        sl = pl.ds(n * CHUNK, CHUNK)
        kc = k_ref[0, pp, sl, :].astype(F32)
        kd = jnp.concatenate([(kc * kdf).astype(BF16), (kc * kdb).astype(BF16)], axis=1)
        kv = _dot_tn(kd, v_ref[0, pp, sl, :])
        kv_ref[n] = kv
        yield

    zero_state = jnp.zeros((LANES, LANES), BF16)
    state = jnp.zeros((LANES, LANES), F32)
    for n in range(n_chunks):
        st_ref[n, 0:LANES, :] = jnp.where(same_head, state.astype(BF16), zero_state)
        state = state * cdf + kv_ref[n, 0:LANES, :]
    state = jnp.zeros((LANES, LANES), F32)
    for n in reversed(range(n_chunks)):
        st_ref[n, LANES:2 * LANES, :] = jnp.where(same_head, state.astype(BF16), zero_state)
        state = state * cdb + kv_ref[n, LANES:2 * LANES, :]

    zero = jnp.zeros((CHUNK, LANES), BF16)
    for n in range(n_chunks):
        sl = pl.ds(n * CHUNK, CHUNK)
        qb = q_ref[0, pp, sl, :]
        kb = k_ref[0, pp, sl, :]
        qs = jnp.concatenate([jnp.where(head0, qb, zero), jnp.where(head0, zero, qb)], axis=0)
        p = (_dot_nt(qs, kb) * mask2).astype(BF16)
        p_ref[n, :, 0:CHUNK] = p[0:CHUNK]
        p_ref[n, :, CHUNK:2 * CHUNK] = p[CHUNK:2 * CHUNK]
        yield

    for n in range(n_chunks):
        sl = pl.ds(n * CHUNK, CHUNK)
        vb = v_ref[0, pp, sl, :]
        qc = q_ref[0, pp, sl, :].astype(F32)
        lhs = jnp.concatenate([p_ref[n], (qc * qdf).astype(BF16), (qc * qdb).astype(BF16)], axis=1)
        rhs = jnp.concatenate([jnp.where(head0, vb, zero), jnp.where(head0, zero, vb), st_ref[n]],
                              axis=0)
        y_ref[sl, :] = _dot(lhs, rhs)
        yield

    def group_mean(z):
        return _dot(z.astype(BF16), gn_avg)

    for n in range(n_chunks):
        sl = pl.ds(n * CHUNK, CHUNK)
        y = y_ref[sl, :]
        y_ref[sl, :] = y - group_mean(y)
        yield

    gain = gain_ref[pp]
    for n in range(n_chunks):
        sl = pl.ds(n * CHUNK, CHUNK)
        dlt = y_ref[sl, :]
        var = group_mean(dlt * dlt)
        gate = g_ref[0, pp, sl, :].astype(F32)
        o_ref[0, sl, out_lanes] = (dlt * lax.rsqrt(var + GN_EPS) * gain
                                   * (gate * jax.nn.sigmoid(gate))).astype(BF16)
        yield


_RET_PAIRS_PER_STEP = 4


def _ret_kernel(*refs, seq):
    bodies = [_ret_body(*refs, seq=seq, pp=pp) for pp in range(_RET_PAIRS_PER_STEP)]
    alive = list(bodies)
    while alive:
        for gen in list(alive):
            try:
                next(gen)
            except StopIteration:
                alive.remove(gen)


def _retention(ret, decf, decb, gain):
    b, _, s, _ = ret.shape
    n_chunks = s // CHUNK
    pps = _RET_PAIRS_PER_STEP
    steps = RET_PAIRS // pps

    def part(kind):
        return pl.BlockSpec((1, pps, s, LANES), lambda i, j, kind=kind: (i, kind * steps + j, 0, 0))

    return pl.pallas_call(
        functools.partial(_ret_kernel, seq=s),
        grid=(b, steps),
        in_specs=[
            part(0), part(1), part(2), part(3),
            pl.BlockSpec((pps, HEADS_PER_LANE_BLOCK, LANES), lambda i, j: (j, 0, 0)),
            pl.BlockSpec((pps, HEADS_PER_LANE_BLOCK, LANES), lambda i, j: (j, 0, 0)),
            pl.BlockSpec((pps, 1, LANES), lambda i, j: (j, 0, 0)),
        ],
        out_specs=pl.BlockSpec((1, s, pps * LANES), lambda i, j: (i, 0, j)),
        out_shape=jax.ShapeDtypeStruct((b, s, RET_WIDTH), BF16),
        scratch_shapes=[
            pltpu.VMEM((pps, n_chunks, 2 * LANES, LANES), F32),
            pltpu.VMEM((pps, n_chunks, 2 * LANES, LANES), BF16),
            pltpu.VMEM((pps, n_chunks, CHUNK, 2 * CHUNK), BF16),
            pltpu.VMEM((pps, s, LANES), F32),
        ],
        compiler_params=pltpu.CompilerParams(
            dimension_semantics=("arbitrary", "arbitrary"), vmem_limit_bytes=VMEM_LIMIT_BYTES),
        name="retention",
    )(ret, ret, ret, ret, decf, decb, gain)


_STACK = 2 * BLOCK
_KEYS = 3 * BLOCK
_SUBLANES = 8
_LOG2E = 1.4426950408889634
_ATTN_STAGE_LAG = (0, 2, 4, 5)


def _attn_kernel(q_ref, kv_ref, slope_ref, sink_ref, o_ref, ks_ref, kss_ref, vt_ref, bias_ref,
                 *, seq):
    n_blocks = seq // BLOCK
    lane = lax.broadcasted_iota(jnp.int32, (1, LANES), 1)
    low = lane < HEAD_DIM

    for hk in range(ATTN_KV_HEADS):
        kv = kv_ref[0, hk].astype(F32)
        ks = kv * (QK_SCALE * _LOG2E)
        ks_ref[hk] = ks.astype(BF16)
        kss_ref[hk] = pltpu.roll(ks, HEAD_DIM, 1).astype(BF16)
        kv_t = jnp.transpose(kv)
        vt_ref[hk] = jnp.concatenate(
            [kv_t[HEAD_DIM:], jnp.ones((_BF16_ROWS, seq), F32)], axis=0).astype(BF16)

    kj = lax.broadcasted_iota(jnp.int32, (_KEYS, _STACK), 0)
    c = lax.broadcasted_iota(jnp.int32, (_KEYS, _STACK), 1)
    qi = jnp.where(c < BLOCK, c, c - BLOCK)
    dist = jnp.abs(kj - BLOCK - qi)
    for hk in range(ATTN_KV_HEADS):
        for par in range(2):
            slope = slope_ref[hk, par, 0:1, :]
            bias_ref[hk, par] = jnp.where(dist <= WINDOW, -slope * dist.astype(F32), NEG_INF) * _LOG2E

    zero = jnp.zeros((BLOCK, LANES), BF16)

    def key_range(n):
        lo_blk, hi_blk = max(n - 1, 0), min(n + 2, n_blocks)
        return pl.ds(lo_blk * BLOCK, (hi_blk - lo_blk) * BLOCK), (lo_blk - (n - 1)) * BLOCK, \
            (hi_blk - lo_blk) * BLOCK

    units = [(hk, n, par) for hk in range(ATTN_KV_HEADS) for n in range(n_blocks) for par in range(2)]
    live = [dict() for _ in units]

    def stage_scores(u):
        hk, n, par = units[u]
        keys, c0, width = key_range(n)
        qblk = q_ref[0, hk, pl.ds(n * BLOCK, BLOCK), :]
        qa, qb = qblk[:, :LANES], qblk[:, LANES:]
        if par == 0:
            qs = jnp.concatenate([jnp.where(low, qa, zero), jnp.where(low, qb, zero)], axis=0)
            kmat = ks_ref[hk, keys, :]
        else:
            qs = jnp.concatenate([jnp.where(low, zero, qa), jnp.where(low, zero, qb)], axis=0)
            kmat = kss_ref[hk, keys, :]
        live[u]["s"] = _dot_nt(kmat, qs) + bias_ref[hk, par, c0:c0 + width, :]

    def stage_softmax(u):
        hk, _, par = units[u]
        st = live[u]
        s = st.pop("s")
        sink = sink_ref[hk, par, 0:1, :] * _LOG2E
        m = jnp.maximum(jnp.max(s, axis=0, keepdims=True), sink)
        st["e"] = jnp.exp2(s - m).astype(BF16)
        st["esink"] = jnp.exp2(sink - m)

    def stage_values(u):
        hk, n, _ = units[u]
        keys, _, _ = key_range(n)
        live[u]["pv"] = _dot(vt_ref[hk, :, keys], live[u].pop("e"))

    def stage_out(u):
        hk, n, par = units[u]
        st = live[u]
        pv = st.pop("pv")
        st["out"] = pv[:HEAD_DIM] / (pv[HEAD_DIM:HEAD_DIM + 1] + st.pop("esink"))
        if par == 1:
            even, odd = live[u - 1].pop("out"), st.pop("out")
            y_t = jnp.concatenate(
                [even[:, :BLOCK], odd[:, :BLOCK], even[:, BLOCK:], odd[:, BLOCK:]], axis=0)
            o_ref[0, pl.ds(n * BLOCK, BLOCK), pl.ds(hk * _STACK, _STACK)] = \
                jnp.transpose(y_t).astype(BF16)

    stages = (stage_scores, stage_softmax, stage_values, stage_out)
    for i in range(len(units) + _ATTN_STAGE_LAG[-1]):
        for lag, stage in zip(_ATTN_STAGE_LAG, stages):
            if 0 <= i - lag < len(units):
                stage(i - lag)


def _attention(aq, akv, slope_cols, sink_cols):
    b, _, s, qw = aq.shape
    return pl.pallas_call(
        functools.partial(_attn_kernel, seq=s),
        grid=(b,),
        in_specs=[
            pl.BlockSpec((1, ATTN_KV_HEADS, s, qw), lambda i: (i, 0, 0, 0)),
            pl.BlockSpec((1, ATTN_KV_HEADS, s, LANES), lambda i: (i, 0, 0, 0)),
            pl.BlockSpec((ATTN_KV_HEADS, 2, _SUBLANES, _STACK), lambda i: (0, 0, 0, 0)),
            pl.BlockSpec((ATTN_KV_HEADS, 2, _SUBLANES, _STACK), lambda i: (0, 0, 0, 0)),
        ],
        out_specs=pl.BlockSpec((1, s, ATTN_WIDTH), lambda i: (i, 0, 0)),
        out_shape=jax.ShapeDtypeStruct((b, s, ATTN_WIDTH), BF16),
        scratch_shapes=[
            pltpu.VMEM((ATTN_KV_HEADS, s, LANES), BF16),
            pltpu.VMEM((ATTN_KV_HEADS, s, LANES), BF16),
            pltpu.VMEM((ATTN_KV_HEADS, HEAD_DIM + _BF16_ROWS, s), BF16),
            pltpu.VMEM((ATTN_KV_HEADS, 2, _KEYS, _STACK), F32),
        ],
        compiler_params=pltpu.CompilerParams(
            dimension_semantics=("arbitrary",), vmem_limit_bytes=VMEM_LIMIT_BYTES),
        name="attention",
    )(aq, akv, slope_cols, sink_cols)


_FFN_SUB_TILES = 2
_FFN_HIDDEN_CHUNK = 256
assert FFN_HIDDEN % _FFN_HIDDEN_CHUNK == 0


def _layer_norm(z, gain, bias):
    mu = jnp.mean(z, axis=-1, keepdims=True)
    dlt = z - mu
    var = jnp.mean(dlt * dlt, axis=-1, keepdims=True)
    return dlt * lax.rsqrt(var + LN_EPS) * gain + bias


def _ffn_kernel(x_ref, yr_ref, ya_ref, p_ref, wo_ref, g1_ref, b1_ref, wg_ref, wu_ref, wd_ref,
                wpe_ref, wpg_ref, g2_ref, b2_ref, o_ref, *, alpha):
    tm = x_ref.shape[0]
    rows = [pl.ds(k * (tm // _FFN_SUB_TILES), tm // _FFN_SUB_TILES) for k in range(_FFN_SUB_TILES)]
    mix = [_dot(yr_ref[r, :], wo_ref[0:RET_WIDTH, :]) + _dot(ya_ref[r, :], wo_ref[RET_WIDTH:, :])
           for r in rows]
    h = [_layer_norm(alpha * x_ref[r, :] + m, g1_ref[...], b1_ref[...]) for r, m in zip(rows, mix)]
    z = []
    for r, hk in zip(rows, h):
        hb = hk.astype(BF16)
        ple = _dot(p_ref[r, :].astype(BF16), wpe_ref[...])
        ple_gate = jax.nn.sigmoid(_dot(hb, wpg_ref[...]))
        rest = alpha * hk + ple_gate * ple
        acts, acc = [], rest
        for c in range(FFN_HIDDEN // _FFN_HIDDEN_CHUNK + 1):
            if c < FFN_HIDDEN // _FFN_HIDDEN_CHUNK:
                cols = pl.ds(c * _FFN_HIDDEN_CHUNK, _FFN_HIDDEN_CHUNK)
                g = _dot(hb, wg_ref[:, cols])
                up = _dot(hb, wu_ref[:, cols])
                acts.append((g * jax.nn.sigmoid(g) * up).astype(BF16))
            if c >= 1:
                acc = acc + _dot(acts[c - 1], wd_ref[pl.ds((c - 1) * _FFN_HIDDEN_CHUNK, _FFN_HIDDEN_CHUNK), :])
        z.append(acc)
    for r, zk in zip(rows, z):
        o_ref[r, :] = _layer_norm(zk, g2_ref[...], b2_ref[...])


def _ffn(x, yr, ya, p, wo, g1, b1, wg, wu, wd, wpe, wpg, g2, b2, *, alpha, tm):
    t, d = x.shape

    def rows(width):
        return pl.BlockSpec((tm, width), lambda i: (i, 0))

    def resident(arr):
        return pl.BlockSpec(arr.shape, lambda i: (0, 0), pipeline_mode=pl.Buffered(1))

    return pl.pallas_call(
        functools.partial(_ffn_kernel, alpha=alpha),
        grid=(t // tm,),
        in_specs=[rows(d), rows(RET_WIDTH), rows(ATTN_WIDTH), rows(PLE_DIM),
                  resident(wo), resident(g1), resident(b1), resident(wg), resident(wu),
                  resident(wd), resident(wpe), resident(wpg), resident(g2), resident(b2)],
        out_specs=rows(d),
        out_shape=jax.ShapeDtypeStruct((t, d), F32),
        compiler_params=pltpu.CompilerParams(
            dimension_semantics=("arbitrary",), vmem_limit_bytes=VMEM_LIMIT_BYTES),
        name="ffn",
    )(x, yr, ya, p, wo, g1, b1, wg, wu, wd, wpe, wpg, g2, b2)


def _stacked_head_cols(per_head):
    v = per_head.astype(F32).reshape(ATTN_KV_HEADS, 2, 2)
    v = jnp.transpose(v, (0, 2, 1))
    v = jnp.repeat(v, BLOCK, axis=2)
    return jnp.broadcast_to(v[:, :, None, :], v.shape[:2] + (_SUBLANES, _STACK))


def kernel(x, p, w_in, ret_decay_fwd, ret_decay_bwd, ret_gn_gain, attn_sink, w_out, ln1_gain, ln1_bias,
           w_ffn_gate, w_ffn_up, w_ffn_down, w_ple_proj, w_ple_gate, ln2_gain, ln2_bias):
    depth = w_in.shape[0]
    b, s, d = x.shape
    alpha = (2.0 * depth) ** 0.25
    slopes = 2.0 ** (-8.0 * (jnp.arange(ATTN_HEADS, dtype=F32) + 1.0) / ATTN_HEADS)
    slope_cols = _stacked_head_cols(slopes)

    def lane_rep(v):
        v = v.astype(F32).reshape(RET_PAIRS, HEADS_PER_LANE_BLOCK, 1)
        return jnp.broadcast_to(v, (RET_PAIRS, HEADS_PER_LANE_BLOCK, LANES))

    row = lambda v: v.astype(F32).reshape(1, -1)
    h = x.astype(F32)
    for i in range(depth):
        (ret, aq, akv), (wo, wg, wu, wd, wpe, wpg) = _in_proj(
            h, w_in, i, (w_out, w_ffn_gate, w_ffn_up, w_ffn_down, w_ple_proj, w_ple_gate), tm=IN_PROJ_TILE)
        y_ret = _retention(ret, lane_rep(ret_decay_fwd[i]), lane_rep(ret_decay_bwd[i]),
                           ret_gn_gain[i].astype(F32).reshape(RET_PAIRS, 1, LANES))
        y_att = _attention(aq, akv, slope_cols, _stacked_head_cols(attn_sink[i]))
        h = _ffn(h.reshape(b * s, d), y_ret.reshape(b * s, RET_WIDTH), y_att.reshape(b * s, ATTN_WIDTH),
                 p[i].reshape(b * s, PLE_DIM),
                 wo, row(ln1_gain[i]), row(ln1_bias[i]), wg, wu, wd, wpe, wpg,
                 row(ln2_gain[i]), row(ln2_bias[i]), alpha=alpha, tm=FFN_TILE).reshape(b, s, d)
    return h.astype(x.dtype)
```

```python
import functools

import jax
import jax.numpy as jnp
from jax import lax
from jax.experimental import pallas as pl
from jax.experimental.pallas import tpu as pltpu

D_MODEL = 1024
HEAD_DIM = 64
RET_HEADS = 8
ATTN_HEADS = 8
ATTN_KV_HEADS = 2
ATTN_GROUP = ATTN_HEADS // ATTN_KV_HEADS
RET_WIDTH = RET_HEADS * HEAD_DIM
ATTN_WIDTH = ATTN_HEADS * HEAD_DIM
KV_WIDTH = ATTN_KV_HEADS * HEAD_DIM
IN_PROJ_WIDTH = 4 * RET_WIDTH + ATTN_WIDTH + 2 * KV_WIDTH
CHUNK = 128
WINDOW = 128
BLOCK = 128
PLE_DIM = 256
FFN_HIDDEN = ((8 * D_MODEL + 767) // 768) * 256
LN_EPS = 1e-5
GN_EPS = 1e-5
NEG_INF = -1e30
QK_SCALE = HEAD_DIM ** -0.5

LANES = 128
HEADS_PER_LANE_BLOCK = LANES // HEAD_DIM
RET_PAIRS = RET_HEADS // HEADS_PER_LANE_BLOCK
VMEM_LIMIT_BYTES = 56 * 1024 * 1024
IN_PROJ_TILE = 1024
FFN_TILE = 512

F32 = jnp.float32
BF16 = jnp.bfloat16


def _dot(a, b):
    return jnp.dot(a, b, preferred_element_type=F32)


def _dot_nt(a, b):
    return lax.dot_general(a, b, (((1,), (1,)), ((), ())), preferred_element_type=F32)


def _dot_tn(a, b):
    return lax.dot_general(a, b, (((0,), (0,)), ((), ())), preferred_element_type=F32)


def _round_robin(*bodies):
    alive = list(bodies)
    while alive:
        for body in list(alive):
            try:
                next(body)
                yield
            except StopIteration:
                alive.remove(body)


def _interleave(bodies):
    progress = [0] * len(bodies)
    alive = [True] * len(bodies)
    while any(alive):
        k = min((i for i in range(len(bodies)) if alive[i]),
                key=lambda i: progress[i] / max(bodies[i][1], 1))
        try:
            next(bodies[k][0])
            progress[k] += 1
        except StopIteration:
            alive[k] = False


_IN_PROJ_COLS = 256
_RET_BLOCKS = 4 * RET_PAIRS
_KV_COL0 = 4 * RET_WIDTH + ATTN_WIDTH
_BF16_ROWS = 16


def _cast_rows_per_step(n_rows, n_steps):
    for rows in range(_BF16_ROWS, n_rows + 1, _BF16_ROWS):
        if n_rows % rows == 0 and n_rows // rows <= n_steps:
            return rows
    raise ValueError(f"no bf16-tile row split of {n_rows} rows over {n_steps} steps")


def _in_proj_kernel(x_ref, w_ref, *refs, n_cast):
    cast_in, (ret_ref, aq_ref, akv_ref) = refs[:n_cast], refs[n_cast:n_cast + 3]
    cast_out, wb_ref = refs[n_cast + 3:2 * n_cast + 3], refs[2 * n_cast + 3]

    @pl.when((pl.program_id(0) == 0) & (pl.program_id(1) == 0))
    def _():
        wb_ref[:, 0:_KV_COL0] = w_ref[:, 0:_KV_COL0].astype(BF16)
        low = lax.broadcasted_iota(jnp.int32, (1, LANES), 1) < HEAD_DIM
        ak = w_ref[:, _KV_COL0:_KV_COL0 + KV_WIDTH]
        av = w_ref[:, _KV_COL0 + KV_WIDTH:_KV_COL0 + 2 * KV_WIDTH]
        wb_ref[:, _KV_COL0:_KV_COL0 + LANES] = jnp.where(
            low, ak, pltpu.roll(av, HEAD_DIM, 1)).astype(BF16)
        wb_ref[:, _KV_COL0 + LANES:_KV_COL0 + 2 * LANES] = jnp.where(
            low, pltpu.roll(ak, HEAD_DIM, 1), av).astype(BF16)

    for src, dst in zip(cast_in, cast_out):
        dst[...] = src[...].astype(BF16)

    xb = x_ref[0].astype(BF16)
    for c in range(IN_PROJ_WIDTH // _IN_PROJ_COLS):
        u = _dot(xb, wb_ref[:, c * _IN_PROJ_COLS:(c + 1) * _IN_PROJ_COLS]).astype(BF16)
        if 2 * c < _RET_BLOCKS:
            ret_ref[0, 2 * c] = u[:, :LANES]
            ret_ref[0, 2 * c + 1] = u[:, LANES:]
        elif c < _RET_BLOCKS // 2 + ATTN_KV_HEADS:
            aq_ref[0, c - _RET_BLOCKS // 2] = u
        else:
            akv_ref[0, 0] = u[:, :LANES]
            akv_ref[0, 1] = u[:, LANES:]


def _cast_specs(cast_weights, layer, n_steps, step_of):
    in_specs, out_specs, out_shapes = [], [], []
    for w in cast_weights:
        rows = _cast_rows_per_step(w.shape[1], n_steps)
        last = w.shape[1] // rows - 1
        in_specs.append(pl.BlockSpec(
            (None, rows, w.shape[2]),
            lambda *idx, last=last: (layer, jnp.minimum(step_of(*idx), last), 0)))
        out_specs.append(pl.BlockSpec(
            (rows, w.shape[2]), lambda *idx, last=last: (jnp.minimum(step_of(*idx), last), 0)))
        out_shapes.append(jax.ShapeDtypeStruct(w.shape[1:], BF16))
    return in_specs, out_specs, out_shapes


def _in_proj(x, w_in, layer, cast_weights, tm):
    b, s, d = x.shape
    n_j = s // tm
    cast_in_specs, cast_out_specs, cast_out_shapes = _cast_specs(
        cast_weights, layer, b * n_j, lambda i, j: i * n_j + j)

    outs = pl.pallas_call(
        functools.partial(_in_proj_kernel, n_cast=len(cast_weights)),
        grid=(b, n_j),
        in_specs=[
            pl.BlockSpec((1, tm, d), lambda i, j: (i, j, 0)),
            pl.BlockSpec((None, d, IN_PROJ_WIDTH), lambda i, j: (layer, 0, 0),
                         pipeline_mode=pl.Buffered(1)),
        ] + cast_in_specs,
        out_specs=[
            pl.BlockSpec((1, _RET_BLOCKS, tm, LANES), lambda i, j: (i, 0, j, 0)),
            pl.BlockSpec((1, ATTN_KV_HEADS, tm, ATTN_GROUP * HEAD_DIM), lambda i, j: (i, 0, j, 0)),
            pl.BlockSpec((1, ATTN_KV_HEADS, tm, LANES), lambda i, j: (i, 0, j, 0)),
        ] + cast_out_specs,
        out_shape=[
            jax.ShapeDtypeStruct((b, _RET_BLOCKS, s, LANES), BF16),
            jax.ShapeDtypeStruct((b, ATTN_KV_HEADS, s, ATTN_GROUP * HEAD_DIM), BF16),
            jax.ShapeDtypeStruct((b, ATTN_KV_HEADS, s, LANES), BF16),
        ] + cast_out_shapes,
        scratch_shapes=[pltpu.VMEM((d, IN_PROJ_WIDTH), BF16)],
        compiler_params=pltpu.CompilerParams(
            dimension_semantics=("arbitrary", "arbitrary"), vmem_limit_bytes=VMEM_LIMIT_BYTES),
        name="in_proj",
    )(x, w_in, *cast_weights)
    return outs[:3], outs[3:]


def _ret_body(q_ref, k_ref, v_ref, g_ref, decf_ref, decb_ref, gain_ref, o_ref, kv_ref, st_ref,
              p_ref, y_ref, *, seq, pp):
    n_chunks = seq // CHUNK
    kv_ref, st_ref, p_ref, y_ref = kv_ref.at[pp], st_ref.at[pp], p_ref.at[pp], y_ref.at[pp]
    out_lanes = pl.ds(pp * LANES, LANES)
    lane = lax.broadcasted_iota(jnp.int32, (1, LANES), 1)
    head0 = lane < HEAD_DIM

    def log_gamma(dec):
        return jnp.log1p(-jnp.exp2(dec))

    lgf0, lgf1 = log_gamma(decf_ref[pp, 0:1, :]), log_gamma(decf_ref[pp, 1:2, :])
    lgb0, lgb1 = log_gamma(decb_ref[pp, 0:1, :]), log_gamma(decb_ref[pp, 1:2, :])
    lgf = jnp.where(head0, lgf0, lgf1)
    lgb = jnp.where(head0, lgb0, lgb1)

    idx = lax.broadcasted_iota(jnp.int32, (CHUNK, LANES), 0).astype(F32)
    qdf = jnp.exp(lgf * (idx + 1.0))
    kdf = jnp.exp(lgf * (CHUNK - 1.0 - idx)) * QK_SCALE
    qdb = jnp.exp(lgb * (CHUNK - idx))
    kdb = jnp.exp(lgb * idx) * QK_SCALE
    cdf = jnp.exp(lgf * CHUNK)
    cdb = jnp.exp(lgb * CHUNK)

    ri = lax.broadcasted_iota(jnp.int32, (CHUNK, CHUNK), 0)
    ci = lax.broadcasted_iota(jnp.int32, (CHUNK, CHUNK), 1)
    diff = (ri - ci).astype(F32)

    def intra_mask(lf, lb):
        fwd = jnp.exp(lf * jnp.maximum(diff, 0.0))
        bwd = jnp.exp(lb * jnp.maximum(-diff, 0.0))
        return jnp.where(diff > 0, fwd, jnp.where(diff < 0, bwd, fwd + bwd))

    mask2 = jnp.concatenate([intra_mask(lgf0, lgb0), intra_mask(lgf1, lgb1)], axis=0) * QK_SCALE
    same_head = (ri < HEAD_DIM) == (ci < HEAD_DIM)
    gn_avg = jnp.where(same_head, 1.0 / HEAD_DIM, 0.0).astype(BF16)

    for n in range(n_chunks):
        sl = pl.ds(n * CHUNK, CHUNK)
        kc = k_ref[0, pp, sl, :].astype(F32)
        kd = jnp.concatenate([(kc * kdf).astype(BF16), (kc * kdb).astype(BF16)], axis=1)
        kv = _dot_tn(kd, v_ref[0, pp, sl, :])
        kv_ref[n] = kv
        yield

    zero_state = jnp.zeros((LANES, LANES), BF16)
    state = jnp.zeros((LANES, LANES), F32)
    for n in range(n_chunks):
        st_ref[n, 0:LANES, :] = jnp.where(same_head, state.astype(BF16), zero_state)
        state = state * cdf + kv_ref[n, 0:LANES, :]
    state = jnp.zeros((LANES, LANES), F32)
    for n in reversed(range(n_chunks)):
        st_ref[n, LANES:2 * LANES, :] = jnp.where(same_head, state.astype(BF16), zero_state)
        state = state * cdb + kv_ref[n, LANES:2 * LANES, :]

    zero = jnp.zeros((CHUNK, LANES), BF16)
    for n in range(n_chunks):
        sl = pl.ds(n * CHUNK, CHUNK)
        qb = q_ref[0, pp, sl, :]
        kb = k_ref[0, pp, sl, :]
        qs = jnp.concatenate([jnp.where(head0, qb, zero), jnp.where(head0, zero, qb)], axis=0)
        p = (_dot_nt(qs, kb) * mask2).astype(BF16)
        p_ref[n, :, 0:CHUNK] = p[0:CHUNK]
        p_ref[n, :, CHUNK:2 * CHUNK] = p[CHUNK:2 * CHUNK]
        yield

    for n in range(n_chunks):
        sl = pl.ds(n * CHUNK, CHUNK)
        vb = v_ref[0, pp, sl, :]
        qc = q_ref[0, pp, sl, :].astype(F32)
        lhs = jnp.concatenate([p_ref[n], (qc * qdf).astype(BF16), (qc * qdb).astype(BF16)], axis=1)
        rhs = jnp.concatenate([jnp.where(head0, vb, zero), jnp.where(head0, zero, vb), st_ref[n]],
                              axis=0)
        y_ref[sl, :] = _dot(lhs, rhs)
        yield

    def group_mean(z):
        return _dot(z.astype(BF16), gn_avg)

    for n in range(n_chunks):
        sl = pl.ds(n * CHUNK, CHUNK)
        y = y_ref[sl, :]
        y_ref[sl, :] = y - group_mean(y)
        yield

    gain = gain_ref[pp]
    for n in range(n_chunks):
        sl = pl.ds(n * CHUNK, CHUNK)
        dlt = y_ref[sl, :]
        var = group_mean(dlt * dlt)
        gate = g_ref[0, pp, sl, :].astype(F32)
        o_ref[0, sl, out_lanes] = (dlt * lax.rsqrt(var + GN_EPS) * gain
                                   * (gate * jax.nn.sigmoid(gate))).astype(BF16)
        yield


_RET_YIELDS_PER_PAIR = 5


_STACK = 2 * BLOCK
_KEYS = 3 * BLOCK
_SUBLANES = 8
_LOG2E = 1.4426950408889634
_ATTN_STAGE_LAG = (0, 2, 4, 5)


def _attn_body(q_ref, kv_ref, slope_ref, sink_ref, o_ref, ks_ref, kss_ref, vt_ref, bias_ref,
               *, seq):
    n_blocks = seq // BLOCK
    lane = lax.broadcasted_iota(jnp.int32, (1, LANES), 1)
    low = lane < HEAD_DIM

    for hk in range(ATTN_KV_HEADS):
        kv = kv_ref[0, hk].astype(F32)
        ks = kv * (QK_SCALE * _LOG2E)
        ks_ref[hk] = ks.astype(BF16)
        kss_ref[hk] = pltpu.roll(ks, HEAD_DIM, 1).astype(BF16)
        kv_t = jnp.transpose(kv)
        vt_ref[hk] = jnp.concatenate(
            [kv_t[HEAD_DIM:], jnp.ones((_BF16_ROWS, seq), F32)], axis=0).astype(BF16)

    kj = lax.broadcasted_iota(jnp.int32, (_KEYS, _STACK), 0)
    c = lax.broadcasted_iota(jnp.int32, (_KEYS, _STACK), 1)
    qi = jnp.where(c < BLOCK, c, c - BLOCK)
    dist = jnp.abs(kj - BLOCK - qi)
    for hk in range(ATTN_KV_HEADS):
        for par in range(2):
            slope = slope_ref[hk, par, 0:1, :]
            bias_ref[hk, par] = jnp.where(dist <= WINDOW, -slope * dist.astype(F32), NEG_INF) * _LOG2E
    yield

    zero = jnp.zeros((BLOCK, LANES), BF16)

    def key_range(n):
        lo_blk, hi_blk = max(n - 1, 0), min(n + 2, n_blocks)
        return pl.ds(lo_blk * BLOCK, (hi_blk - lo_blk) * BLOCK), (lo_blk - (n - 1)) * BLOCK, \
            (hi_blk - lo_blk) * BLOCK

    units = [(hk, n, par) for hk in range(ATTN_KV_HEADS) for n in range(n_blocks) for par in range(2)]
    live = [dict() for _ in units]

    def stage_scores(u):
        hk, n, par = units[u]
        keys, c0, width = key_range(n)
        qblk = q_ref[0, hk, pl.ds(n * BLOCK, BLOCK), :]
        qa, qb = qblk[:, :LANES], qblk[:, LANES:]
        if par == 0:
            qs = jnp.concatenate([jnp.where(low, qa, zero), jnp.where(low, qb, zero)], axis=0)
            kmat = ks_ref[hk, keys, :]
        else:
            qs = jnp.concatenate([jnp.where(low, zero, qa), jnp.where(low, zero, qb)], axis=0)
            kmat = kss_ref[hk, keys, :]
        live[u]["s"] = _dot_nt(kmat, qs) + bias_ref[hk, par, c0:c0 + width, :]

    def stage_softmax(u):
        hk, _, par = units[u]
        st = live[u]
        s = st.pop("s")
        sink = sink_ref[hk, par, 0:1, :] * _LOG2E
        m = jnp.maximum(jnp.max(s, axis=0, keepdims=True), sink)
        st["e"] = jnp.exp2(s - m).astype(BF16)
        st["esink"] = jnp.exp2(sink - m)

    def stage_values(u):
        hk, n, _ = units[u]
        keys, _, _ = key_range(n)
        live[u]["pv"] = _dot(vt_ref[hk, :, keys], live[u].pop("e"))

    def stage_out(u):
        hk, n, par = units[u]
        st = live[u]
        pv = st.pop("pv")
        st["out"] = pv[:HEAD_DIM] / (pv[HEAD_DIM:HEAD_DIM + 1] + st.pop("esink"))
        if par == 1:
            even, odd = live[u - 1].pop("out"), st.pop("out")
            y_t = jnp.concatenate(
                [even[:, :BLOCK], odd[:, :BLOCK], even[:, BLOCK:], odd[:, BLOCK:]], axis=0)
            o_ref[0, pl.ds(n * BLOCK, BLOCK), pl.ds(hk * _STACK, _STACK)] = \
                jnp.transpose(y_t).astype(BF16)

    stages = (stage_scores, stage_softmax, stage_values, stage_out)
    for i in range(len(units) + _ATTN_STAGE_LAG[-1]):
        for lag, stage in zip(_ATTN_STAGE_LAG, stages):
            if 0 <= i - lag < len(units):
                stage(i - lag)
        yield


def _mixers_kernel(q_ref, k_ref, v_ref, g_ref, decf_ref, decb_ref, gain_ref,
                   aq_ref, akv_ref, slope_ref, sink_ref, yr_ref, ya_ref,
                   kv_ref, st_ref, p_ref, y_ref, ks_ref, kss_ref, vt_ref, bias_ref, *, seq):
    ret_args = (q_ref, k_ref, v_ref, g_ref, decf_ref, decb_ref, gain_ref, yr_ref,
                kv_ref, st_ref, p_ref, y_ref)
    n_units = ATTN_KV_HEADS * (seq // BLOCK) * 2
    _interleave([
        (_round_robin(*[_ret_body(*ret_args, seq=seq, pp=pp) for pp in range(RET_PAIRS)]),
         RET_PAIRS * _RET_YIELDS_PER_PAIR * (seq // CHUNK)),
        (_attn_body(aq_ref, akv_ref, slope_ref, sink_ref, ya_ref, ks_ref, kss_ref, vt_ref, bias_ref,
                    seq=seq), 1 + n_units + _ATTN_STAGE_LAG[-1]),
    ])


def _mixers(ret, aq, akv, decf, decb, gain, slope_cols, sink_cols):
    b, _, s, _ = ret.shape
    n_chunks = s // CHUNK
    qw = aq.shape[3]

    def part(kind):
        return pl.BlockSpec((1, RET_PAIRS, s, LANES), lambda i, kind=kind: (i, kind, 0, 0))

    def whole(arr):
        return pl.BlockSpec(arr.shape, lambda i: (0,) * arr.ndim)

    return pl.pallas_call(
        functools.partial(_mixers_kernel, seq=s),
        grid=(b,),
        in_specs=[
            part(0), part(1), part(2), part(3), whole(decf), whole(decb), whole(gain),
            pl.BlockSpec((1, ATTN_KV_HEADS, s, qw), lambda i: (i, 0, 0, 0)),
            pl.BlockSpec((1, ATTN_KV_HEADS, s, LANES), lambda i: (i, 0, 0, 0)),
            whole(slope_cols), whole(sink_cols),
        ],
        out_specs=[pl.BlockSpec((1, s, RET_WIDTH), lambda i: (i, 0, 0)),
                   pl.BlockSpec((1, s, ATTN_WIDTH), lambda i: (i, 0, 0))],
        out_shape=[jax.ShapeDtypeStruct((b, s, RET_WIDTH), BF16),
                   jax.ShapeDtypeStruct((b, s, ATTN_WIDTH), BF16)],
        scratch_shapes=[
            pltpu.VMEM((RET_PAIRS, n_chunks, 2 * LANES, LANES), F32),
            pltpu.VMEM((RET_PAIRS, n_chunks, 2 * LANES, LANES), BF16),
            pltpu.VMEM((RET_PAIRS, n_chunks, CHUNK, 2 * CHUNK), BF16),
            pltpu.VMEM((RET_PAIRS, s, LANES), F32),
            pltpu.VMEM((ATTN_KV_HEADS, s, LANES), BF16),
            pltpu.VMEM((ATTN_KV_HEADS, s, LANES), BF16),
            pltpu.VMEM((ATTN_KV_HEADS, HEAD_DIM + _BF16_ROWS, s), BF16),
            pltpu.VMEM((ATTN_KV_HEADS, 2, _KEYS, _STACK), F32),
        ],
        compiler_params=pltpu.CompilerParams(
            dimension_semantics=("arbitrary",), vmem_limit_bytes=VMEM_LIMIT_BYTES),
        name="mixers",
    )(ret, ret, ret, ret, decf, decb, gain, aq, akv, slope_cols, sink_cols)


_FFN_SUB_TILES = 2
_FFN_HIDDEN_CHUNK = 256
assert FFN_HIDDEN % _FFN_HIDDEN_CHUNK == 0


def _layer_norm(z, gain, bias):
    mu = jnp.mean(z, axis=-1, keepdims=True)
    dlt = z - mu
    var = jnp.mean(dlt * dlt, axis=-1, keepdims=True)
    return dlt * lax.rsqrt(var + LN_EPS) * gain + bias


def _ffn_kernel(x_ref, yr_ref, ya_ref, p_ref, wo_ref, g1_ref, b1_ref, wg_ref, wu_ref, wd_ref,
                wpe_ref, wpg_ref, g2_ref, b2_ref, o_ref, *, alpha):
    tm = x_ref.shape[0]
    rows = [pl.ds(k * (tm // _FFN_SUB_TILES), tm // _FFN_SUB_TILES) for k in range(_FFN_SUB_TILES)]
    mix = [_dot(yr_ref[r, :], wo_ref[0:RET_WIDTH, :]) + _dot(ya_ref[r, :], wo_ref[RET_WIDTH:, :])
           for r in rows]
    h = [_layer_norm(alpha * x_ref[r, :] + m, g1_ref[...], b1_ref[...]) for r, m in zip(rows, mix)]
    z = []
    for r, hk in zip(rows, h):
        hb = hk.astype(BF16)
        ple = _dot(p_ref[r, :].astype(BF16), wpe_ref[...])
        ple_gate = jax.nn.sigmoid(_dot(hb, wpg_ref[...]))
        rest = alpha * hk + ple_gate * ple
        acts, acc = [], rest
        for c in range(FFN_HIDDEN // _FFN_HIDDEN_CHUNK + 1):
            if c < FFN_HIDDEN // _FFN_HIDDEN_CHUNK:
                cols = pl.ds(c * _FFN_HIDDEN_CHUNK, _FFN_HIDDEN_CHUNK)
                g = _dot(hb, wg_ref[:, cols])
                up = _dot(hb, wu_ref[:, cols])
                acts.append((g * jax.nn.sigmoid(g) * up).astype(BF16))
            if c >= 1:
                acc = acc + _dot(acts[c - 1], wd_ref[pl.ds((c - 1) * _FFN_HIDDEN_CHUNK, _FFN_HIDDEN_CHUNK), :])
        z.append(acc)
    for r, zk in zip(rows, z):
        o_ref[r, :] = _layer_norm(zk, g2_ref[...], b2_ref[...])


def _ffn(x, yr, ya, p, wo, g1, b1, wg, wu, wd, wpe, wpg, g2, b2, *, alpha, tm):
    t, d = x.shape

    def rows(width):
        return pl.BlockSpec((tm, width), lambda i: (i, 0))

    def resident(arr):
        return pl.BlockSpec(arr.shape, lambda i: (0, 0), pipeline_mode=pl.Buffered(1))

    return pl.pallas_call(
        functools.partial(_ffn_kernel, alpha=alpha),
        grid=(t // tm,),
        in_specs=[rows(d), rows(RET_WIDTH), rows(ATTN_WIDTH), rows(PLE_DIM),
                  resident(wo), resident(g1), resident(b1), resident(wg), resident(wu),
                  resident(wd), resident(wpe), resident(wpg), resident(g2), resident(b2)],
        out_specs=rows(d),
        out_shape=jax.ShapeDtypeStruct((t, d), F32),
        compiler_params=pltpu.CompilerParams(
            dimension_semantics=("arbitrary",), vmem_limit_bytes=VMEM_LIMIT_BYTES),
        name="ffn",
    )(x, yr, ya, p, wo, g1, b1, wg, wu, wd, wpe, wpg, g2, b2)


def _stacked_head_cols(per_head):
    v = per_head.astype(F32).reshape(ATTN_KV_HEADS, 2, 2)
    v = jnp.transpose(v, (0, 2, 1))
    v = jnp.repeat(v, BLOCK, axis=2)
    return jnp.broadcast_to(v[:, :, None, :], v.shape[:2] + (_SUBLANES, _STACK))


def kernel(x, p, w_in, ret_decay_fwd, ret_decay_bwd, ret_gn_gain, attn_sink, w_out, ln1_gain, ln1_bias,
           w_ffn_gate, w_ffn_up, w_ffn_down, w_ple_proj, w_ple_gate, ln2_gain, ln2_bias):
    depth = w_in.shape[0]
    b, s, d = x.shape
    alpha = (2.0 * depth) ** 0.25
    slopes = 2.0 ** (-8.0 * (jnp.arange(ATTN_HEADS, dtype=F32) + 1.0) / ATTN_HEADS)
    slope_cols = _stacked_head_cols(slopes)

    def lane_rep(v):
        v = v.astype(F32).reshape(RET_PAIRS, HEADS_PER_LANE_BLOCK, 1)
        return jnp.broadcast_to(v, (RET_PAIRS, HEADS_PER_LANE_BLOCK, LANES))

    row = lambda v: v.astype(F32).reshape(1, -1)
    h = x.astype(F32)
    for i in range(depth):
        (ret, aq, akv), (wo, wg, wu, wd, wpe, wpg) = _in_proj(
            h, w_in, i, (w_out, w_ffn_gate, w_ffn_up, w_ffn_down, w_ple_proj, w_ple_gate), tm=IN_PROJ_TILE)
        y_ret, y_att = _mixers(ret, aq, akv, lane_rep(ret_decay_fwd[i]), lane_rep(ret_decay_bwd[i]),
                               ret_gn_gain[i].astype(F32).reshape(RET_PAIRS, 1, LANES),
                               slope_cols, _stacked_head_cols(attn_sink[i]))
        h = _ffn(h.reshape(b * s, d), y_ret.reshape(b * s, RET_WIDTH), y_att.reshape(b * s, ATTN_WIDTH),
                 p[i].reshape(b * s, PLE_DIM),
                 wo, row(ln1_gain[i]), row(ln1_bias[i]), wg, wu, wd, wpe, wpg,
                 row(ln2_gain[i]), row(ln2_bias[i]), alpha=alpha, tm=FFN_TILE).reshape(b, s, d)
    return h.astype(x.dtype)
```

```python
import functools

import jax
import jax.numpy as jnp
from jax import lax
from jax.experimental import pallas as pl
from jax.experimental.pallas import tpu as pltpu

D_MODEL = 1024
HEAD_DIM = 64
RET_HEADS = 8
ATTN_HEADS = 8
ATTN_KV_HEADS = 2
ATTN_GROUP = ATTN_HEADS // ATTN_KV_HEADS
RET_WIDTH = RET_HEADS * HEAD_DIM
ATTN_WIDTH = ATTN_HEADS * HEAD_DIM
KV_WIDTH = ATTN_KV_HEADS * HEAD_DIM
IN_PROJ_WIDTH = 4 * RET_WIDTH + ATTN_WIDTH + 2 * KV_WIDTH
CHUNK = 128
WINDOW = 128
BLOCK = 128
PLE_DIM = 256
FFN_HIDDEN = ((8 * D_MODEL + 767) // 768) * 256
LN_EPS = 1e-5
GN_EPS = 1e-5
NEG_INF = -1e30
QK_SCALE = HEAD_DIM ** -0.5

LANES = 128
HEADS_PER_LANE_BLOCK = LANES // HEAD_DIM
RET_PAIRS = RET_HEADS // HEADS_PER_LANE_BLOCK
VMEM_LIMIT_BYTES = 56 * 1024 * 1024
IN_PROJ_TILE = 1024
FFN_TILE = 512

F32 = jnp.float32
BF16 = jnp.bfloat16


def _dot(a, b):
    return jnp.dot(a, b, preferred_element_type=F32)


def _dot_nt(a, b):
    return lax.dot_general(a, b, (((1,), (1,)), ((), ())), preferred_element_type=F32)


def _dot_tn(a, b):
    return lax.dot_general(a, b, (((0,), (0,)), ((), ())), preferred_element_type=F32)


def _round_robin(*bodies):
    alive = list(bodies)
    while alive:
        for body in list(alive):
            try:
                next(body)
                yield
            except StopIteration:
                alive.remove(body)


def _interleave(bodies):
    progress = [0] * len(bodies)
    alive = [True] * len(bodies)
    while any(alive):
        k = min((i for i in range(len(bodies)) if alive[i]),
                key=lambda i: progress[i] / max(bodies[i][1], 1))
        try:
            next(bodies[k][0])
            progress[k] += 1
        except StopIteration:
            alive[k] = False


_IN_PROJ_COLS = 256
_RET_BLOCKS = 4 * RET_PAIRS
_KV_COL0 = 4 * RET_WIDTH + ATTN_WIDTH
_BF16_ROWS = 16


def _cast_rows_per_step(n_rows, n_steps):
    for rows in range(_BF16_ROWS, n_rows + 1, _BF16_ROWS):
        if n_rows % rows == 0 and n_rows // rows <= n_steps:
            return rows
    raise ValueError(f"no bf16-tile row split of {n_rows} rows over {n_steps} steps")


_X_RING = 3


def _x_tile_copy(x_hbm, xbuf_ref, sem_ref, step, n_j, tm):
    slot = lax.rem(step, _X_RING)
    return pltpu.make_async_copy(
        x_hbm.at[lax.div(step, n_j), pl.ds(pl.multiple_of(lax.rem(step, n_j) * tm, tm), tm), :],
        xbuf_ref.at[slot], sem_ref.at[slot])


def _in_proj_kernel(x_hbm, w_ref, *refs, n_cast, n_steps, n_j, tm):
    cast_in, (ret_ref, aq_ref, akv_ref) = refs[:n_cast], refs[n_cast:n_cast + 3]
    cast_out, (wb_ref, xbuf_ref, sem_ref) = refs[n_cast + 3:2 * n_cast + 3], refs[2 * n_cast + 3:]
    step = pl.program_id(0) * n_j + pl.program_id(1)

    @pl.when(step == 0)
    def _():
        for s in range(min(_X_RING - 1, n_steps)):
            _x_tile_copy(x_hbm, xbuf_ref, sem_ref, jnp.int32(s), n_j, tm).start()

    @pl.when(step + (_X_RING - 1) < n_steps)
    def _():
        _x_tile_copy(x_hbm, xbuf_ref, sem_ref, step + (_X_RING - 1), n_j, tm).start()

    @pl.when((pl.program_id(0) == 0) & (pl.program_id(1) == 0))
    def _():
        wb_ref[:, 0:_KV_COL0] = w_ref[:, 0:_KV_COL0].astype(BF16)
        low = lax.broadcasted_iota(jnp.int32, (1, LANES), 1) < HEAD_DIM
        ak = w_ref[:, _KV_COL0:_KV_COL0 + KV_WIDTH]
        av = w_ref[:, _KV_COL0 + KV_WIDTH:_KV_COL0 + 2 * KV_WIDTH]
        wb_ref[:, _KV_COL0:_KV_COL0 + LANES] = jnp.where(
            low, ak, pltpu.roll(av, HEAD_DIM, 1)).astype(BF16)
        wb_ref[:, _KV_COL0 + LANES:_KV_COL0 + 2 * LANES] = jnp.where(
            low, pltpu.roll(ak, HEAD_DIM, 1), av).astype(BF16)

    for src, dst in zip(cast_in, cast_out):
        dst[...] = src[...].astype(BF16)

    _x_tile_copy(x_hbm, xbuf_ref, sem_ref, step, n_j, tm).wait()
    xb = xbuf_ref[lax.rem(step, _X_RING)].astype(BF16)
    for c in range(IN_PROJ_WIDTH // _IN_PROJ_COLS):
        u = _dot(xb, wb_ref[:, c * _IN_PROJ_COLS:(c + 1) * _IN_PROJ_COLS]).astype(BF16)
        if 2 * c < _RET_BLOCKS:
            ret_ref[0, 2 * c] = u[:, :LANES]
            ret_ref[0, 2 * c + 1] = u[:, LANES:]
        elif c < _RET_BLOCKS // 2 + ATTN_KV_HEADS:
            aq_ref[0, c - _RET_BLOCKS // 2] = u
        else:
            akv_ref[0, 0] = u[:, :LANES]
            akv_ref[0, 1] = u[:, LANES:]


def _cast_specs(cast_weights, layer, n_steps, step_of):
    in_specs, out_specs, out_shapes = [], [], []
    for w in cast_weights:
        rows = _cast_rows_per_step(w.shape[1], n_steps)
        last = w.shape[1] // rows - 1
        in_specs.append(pl.BlockSpec(
            (None, rows, w.shape[2]),
            lambda *idx, last=last: (layer, jnp.minimum(step_of(*idx), last), 0)))
        out_specs.append(pl.BlockSpec(
            (rows, w.shape[2]), lambda *idx, last=last: (jnp.minimum(step_of(*idx), last), 0)))
        out_shapes.append(jax.ShapeDtypeStruct(w.shape[1:], BF16))
    return in_specs, out_specs, out_shapes


def _in_proj(x, w_in, layer, cast_weights, tm):
    b, s, d = x.shape
    n_j = s // tm
    cast_in_specs, cast_out_specs, cast_out_shapes = _cast_specs(
        cast_weights, layer, b * n_j, lambda i, j: i * n_j + j)

    outs = pl.pallas_call(
        functools.partial(_in_proj_kernel, n_cast=len(cast_weights), n_steps=b * n_j, n_j=n_j, tm=tm),
        grid=(b, n_j),
        in_specs=[
            pl.BlockSpec(memory_space=pl.ANY),
            pl.BlockSpec((None, d, IN_PROJ_WIDTH), lambda i, j: (layer, 0, 0),
                         pipeline_mode=pl.Buffered(1)),
        ] + cast_in_specs,
        out_specs=[
            pl.BlockSpec((1, _RET_BLOCKS, tm, LANES), lambda i, j: (i, 0, j, 0)),
            pl.BlockSpec((1, ATTN_KV_HEADS, tm, ATTN_GROUP * HEAD_DIM), lambda i, j: (i, 0, j, 0)),
            pl.BlockSpec((1, ATTN_KV_HEADS, tm, LANES), lambda i, j: (i, 0, j, 0)),
        ] + cast_out_specs,
        out_shape=[
            jax.ShapeDtypeStruct((b, _RET_BLOCKS, s, LANES), BF16),
            jax.ShapeDtypeStruct((b, ATTN_KV_HEADS, s, ATTN_GROUP * HEAD_DIM), BF16),
            jax.ShapeDtypeStruct((b, ATTN_KV_HEADS, s, LANES), BF16),
        ] + cast_out_shapes,
        scratch_shapes=[pltpu.VMEM((d, IN_PROJ_WIDTH), BF16),
                        pltpu.VMEM((_X_RING, tm, d), F32),
                        pltpu.SemaphoreType.DMA((_X_RING,))],
        compiler_params=pltpu.CompilerParams(
            dimension_semantics=("arbitrary", "arbitrary"), vmem_limit_bytes=VMEM_LIMIT_BYTES),
        name="in_proj",
    )(x, w_in, *cast_weights)
    return outs[:3], outs[3:]


def _ret_body(q_ref, k_ref, v_ref, g_ref, decf_ref, decb_ref, gain_ref, o_ref, kv_ref, st_ref,
              p_ref, y_ref, *, seq, pp):
    n_chunks = seq // CHUNK
    kv_ref, st_ref, p_ref, y_ref = kv_ref.at[pp], st_ref.at[pp], p_ref.at[pp], y_ref.at[pp]
    out_lanes = pl.ds(pp * LANES, LANES)
    lane = lax.broadcasted_iota(jnp.int32, (1, LANES), 1)
    head0 = lane < HEAD_DIM

    def log_gamma(dec):
        return jnp.log1p(-jnp.exp2(dec))

    lgf0, lgf1 = log_gamma(decf_ref[pp, 0:1, :]), log_gamma(decf_ref[pp, 1:2, :])
    lgb0, lgb1 = log_gamma(decb_ref[pp, 0:1, :]), log_gamma(decb_ref[pp, 1:2, :])
    lgf = jnp.where(head0, lgf0, lgf1)
    lgb = jnp.where(head0, lgb0, lgb1)

    idx = lax.broadcasted_iota(jnp.int32, (CHUNK, LANES), 0).astype(F32)
    qdf = jnp.exp(lgf * (idx + 1.0))
    kdf = jnp.exp(lgf * (CHUNK - 1.0 - idx)) * QK_SCALE
    qdb = jnp.exp(lgb * (CHUNK - idx))
    kdb = jnp.exp(lgb * idx) * QK_SCALE
    cdf = jnp.exp(lgf * CHUNK)
    cdb = jnp.exp(lgb * CHUNK)

    ri = lax.broadcasted_iota(jnp.int32, (CHUNK, CHUNK), 0)
    ci = lax.broadcasted_iota(jnp.int32, (CHUNK, CHUNK), 1)
    diff = (ri - ci).astype(F32)

    def intra_mask(lf, lb):
        fwd = jnp.exp(lf * jnp.maximum(diff, 0.0))
        bwd = jnp.exp(lb * jnp.maximum(-diff, 0.0))
        return jnp.where(diff > 0, fwd, jnp.where(diff < 0, bwd, fwd + bwd))

    mask2 = jnp.concatenate([intra_mask(lgf0, lgb0), intra_mask(lgf1, lgb1)], axis=0) * QK_SCALE
    same_head = (ri < HEAD_DIM) == (ci < HEAD_DIM)
    gn_avg = jnp.where(same_head, 1.0 / HEAD_DIM, 0.0).astype(BF16)

    for n in range(n_chunks):
        sl = pl.ds(n * CHUNK, CHUNK)
        kc = k_ref[0, pp, sl, :].astype(F32)
        kd = jnp.concatenate([(kc * kdf).astype(BF16), (kc * kdb).astype(BF16)], axis=1)
        kv = _dot_tn(kd, v_ref[0, pp, sl, :])
        kv_ref[n] = kv
        yield

    zero_state = jnp.zeros((LANES, LANES), BF16)
    state = jnp.zeros((LANES, LANES), F32)
    for n in range(n_chunks):
        st_ref[n, 0:LANES, :] = jnp.where(same_head, state.astype(BF16), zero_state)
        state = state * cdf + kv_ref[n, 0:LANES, :]
    state = jnp.zeros((LANES, LANES), F32)
    for n in reversed(range(n_chunks)):
        st_ref[n, LANES:2 * LANES, :] = jnp.where(same_head, state.astype(BF16), zero_state)
        state = state * cdb + kv_ref[n, LANES:2 * LANES, :]

    zero = jnp.zeros((CHUNK, LANES), BF16)
    for n in range(n_chunks):
        sl = pl.ds(n * CHUNK, CHUNK)
        qb = q_ref[0, pp, sl, :]
        kb = k_ref[0, pp, sl, :]
        qs = jnp.concatenate([jnp.where(head0, qb, zero), jnp.where(head0, zero, qb)], axis=0)
        p = (_dot_nt(qs, kb) * mask2).astype(BF16)
        p_ref[n, :, 0:CHUNK] = p[0:CHUNK]
        p_ref[n, :, CHUNK:2 * CHUNK] = p[CHUNK:2 * CHUNK]
        yield

    for n in range(n_chunks):
        sl = pl.ds(n * CHUNK, CHUNK)
        vb = v_ref[0, pp, sl, :]
        qc = q_ref[0, pp, sl, :].astype(F32)
        lhs = jnp.concatenate([p_ref[n], (qc * qdf).astype(BF16), (qc * qdb).astype(BF16)], axis=1)
        rhs = jnp.concatenate([jnp.where(head0, vb, zero), jnp.where(head0, zero, vb), st_ref[n]],
                              axis=0)
        y_ref[sl, :] = _dot(lhs, rhs)
        yield

    def group_mean(z):
        return _dot(z.astype(BF16), gn_avg)

    for n in range(n_chunks):
        sl = pl.ds(n * CHUNK, CHUNK)
        y = y_ref[sl, :]
        y_ref[sl, :] = y - group_mean(y)
        yield

    gain = gain_ref[pp]
    for n in range(n_chunks):
        sl = pl.ds(n * CHUNK, CHUNK)
        dlt = y_ref[sl, :]
        var = group_mean(dlt * dlt)
        gate = g_ref[0, pp, sl, :].astype(F32)
        o_ref[0, sl, out_lanes] = (dlt * lax.rsqrt(var + GN_EPS) * gain
                                   * (gate * jax.nn.sigmoid(gate))).astype(BF16)
        yield


_RET_YIELDS_PER_PAIR = 5


_STACK = 2 * BLOCK
_KEYS = 3 * BLOCK
_SUBLANES = 8
_LOG2E = 1.4426950408889634
_ATTN_STAGE_LAG = (0, 2, 4, 5)


def _attn_body(q_ref, kv_ref, slope_ref, sink_ref, o_ref, ks_ref, kss_ref, vt_ref, bias_ref,
               *, seq):
    n_blocks = seq // BLOCK
    lane = lax.broadcasted_iota(jnp.int32, (1, LANES), 1)
    low = lane < HEAD_DIM

    for hk in range(ATTN_KV_HEADS):
        kv = kv_ref[0, hk].astype(F32)
        ks = kv * (QK_SCALE * _LOG2E)
        ks_ref[hk] = ks.astype(BF16)
        kss_ref[hk] = pltpu.roll(ks, HEAD_DIM, 1).astype(BF16)
        kv_t = jnp.transpose(kv)
        vt_ref[hk] = jnp.concatenate(
            [kv_t[HEAD_DIM:], jnp.ones((_BF16_ROWS, seq), F32)], axis=0).astype(BF16)

    kj = lax.broadcasted_iota(jnp.int32, (_KEYS, _STACK), 0)
    c = lax.broadcasted_iota(jnp.int32, (_KEYS, _STACK), 1)
    qi = jnp.where(c < BLOCK, c, c - BLOCK)
    dist = jnp.abs(kj - BLOCK - qi)
    for hk in range(ATTN_KV_HEADS):
        for par in range(2):
            slope = slope_ref[hk, par, 0:1, :]
            bias_ref[hk, par] = jnp.where(dist <= WINDOW, -slope * dist.astype(F32), NEG_INF) * _LOG2E
    yield

    zero = jnp.zeros((BLOCK, LANES), BF16)

    def key_range(n):
        lo_blk, hi_blk = max(n - 1, 0), min(n + 2, n_blocks)
        return pl.ds(lo_blk * BLOCK, (hi_blk - lo_blk) * BLOCK), (lo_blk - (n - 1)) * BLOCK, \
            (hi_blk - lo_blk) * BLOCK

    units = [(hk, n, par) for hk in range(ATTN_KV_HEADS) for n in range(n_blocks) for par in range(2)]
    live = [dict() for _ in units]

    def stage_scores(u):
        hk, n, par = units[u]
        keys, c0, width = key_range(n)
        qblk = q_ref[0, hk, pl.ds(n * BLOCK, BLOCK), :]
        qa, qb = qblk[:, :LANES], qblk[:, LANES:]
        if par == 0:
            qs = jnp.concatenate([jnp.where(low, qa, zero), jnp.where(low, qb, zero)], axis=0)
            kmat = ks_ref[hk, keys, :]
        else:
            qs = jnp.concatenate([jnp.where(low, zero, qa), jnp.where(low, zero, qb)], axis=0)
            kmat = kss_ref[hk, keys, :]
        live[u]["s"] = _dot_nt(kmat, qs) + bias_ref[hk, par, c0:c0 + width, :]

    def stage_softmax(u):
        hk, _, par = units[u]
        st = live[u]
        s = st.pop("s")
        sink = sink_ref[hk, par, 0:1, :] * _LOG2E
        m = jnp.maximum(jnp.max(s, axis=0, keepdims=True), sink)
        st["e"] = jnp.exp2(s - m).astype(BF16)
        st["esink"] = jnp.exp2(sink - m)

    def stage_values(u):
        hk, n, _ = units[u]
        keys, _, _ = key_range(n)
        live[u]["pv"] = _dot(vt_ref[hk, :, keys], live[u].pop("e"))

    def stage_out(u):
        hk, n, par = units[u]
        st = live[u]
        pv = st.pop("pv")
        st["out"] = pv[:HEAD_DIM] / (pv[HEAD_DIM:HEAD_DIM + 1] + st.pop("esink"))
        if par == 1:
            even, odd = live[u - 1].pop("out"), st.pop("out")
            y_t = jnp.concatenate(
                [even[:, :BLOCK], odd[:, :BLOCK], even[:, BLOCK:], odd[:, BLOCK:]], axis=0)
            o_ref[0, pl.ds(n * BLOCK, BLOCK), pl.ds(hk * _STACK, _STACK)] = \
                jnp.transpose(y_t).astype(BF16)

    stages = (stage_scores, stage_softmax, stage_values, stage_out)
    for i in range(len(units) + _ATTN_STAGE_LAG[-1]):
        for lag, stage in zip(_ATTN_STAGE_LAG, stages):
            if 0 <= i - lag < len(units):
                stage(i - lag)
        yield


def _mixers_kernel(q_ref, k_ref, v_ref, g_ref, decf_ref, decb_ref, gain_ref,
                   aq_ref, akv_ref, slope_ref, sink_ref, yr_ref, ya_ref,
                   kv_ref, st_ref, p_ref, y_ref, ks_ref, kss_ref, vt_ref, bias_ref, *, seq):
    ret_args = (q_ref, k_ref, v_ref, g_ref, decf_ref, decb_ref, gain_ref, yr_ref,
                kv_ref, st_ref, p_ref, y_ref)
    n_units = ATTN_KV_HEADS * (seq // BLOCK) * 2
    _interleave([
        (_round_robin(*[_ret_body(*ret_args, seq=seq, pp=pp) for pp in range(RET_PAIRS)]),
         RET_PAIRS * _RET_YIELDS_PER_PAIR * (seq // CHUNK)),
        (_attn_body(aq_ref, akv_ref, slope_ref, sink_ref, ya_ref, ks_ref, kss_ref, vt_ref, bias_ref,
                    seq=seq), 1 + n_units + _ATTN_STAGE_LAG[-1]),
    ])


def _mixers(ret, aq, akv, decf, decb, gain, slope_cols, sink_cols):
    b, _, s, _ = ret.shape
    n_chunks = s // CHUNK
    qw = aq.shape[3]

    def part(kind):
        return pl.BlockSpec((1, RET_PAIRS, s, LANES), lambda i, kind=kind: (i, kind, 0, 0))

    def whole(arr):
        return pl.BlockSpec(arr.shape, lambda i: (0,) * arr.ndim)

    return pl.pallas_call(
        functools.partial(_mixers_kernel, seq=s),
        grid=(b,),
        in_specs=[
            part(0), part(1), part(2), part(3), whole(decf), whole(decb), whole(gain),
            pl.BlockSpec((1, ATTN_KV_HEADS, s, qw), lambda i: (i, 0, 0, 0)),
            pl.BlockSpec((1, ATTN_KV_HEADS, s, LANES), lambda i: (i, 0, 0, 0)),
            whole(slope_cols), whole(sink_cols),
        ],
        out_specs=[pl.BlockSpec((1, s, RET_WIDTH), lambda i: (i, 0, 0)),
                   pl.BlockSpec((1, s, ATTN_WIDTH), lambda i: (i, 0, 0))],
        out_shape=[jax.ShapeDtypeStruct((b, s, RET_WIDTH), BF16),
                   jax.ShapeDtypeStruct((b, s, ATTN_WIDTH), BF16)],
        scratch_shapes=[
            pltpu.VMEM((RET_PAIRS, n_chunks, 2 * LANES, LANES), F32),
            pltpu.VMEM((RET_PAIRS, n_chunks, 2 * LANES, LANES), BF16),
            pltpu.VMEM((RET_PAIRS, n_chunks, CHUNK, 2 * CHUNK), BF16),
            pltpu.VMEM((RET_PAIRS, s, LANES), F32),
            pltpu.VMEM((ATTN_KV_HEADS, s, LANES), BF16),
            pltpu.VMEM((ATTN_KV_HEADS, s, LANES), BF16),
            pltpu.VMEM((ATTN_KV_HEADS, HEAD_DIM + _BF16_ROWS, s), BF16),
            pltpu.VMEM((ATTN_KV_HEADS, 2, _KEYS, _STACK), F32),
        ],
        compiler_params=pltpu.CompilerParams(
            dimension_semantics=("arbitrary",), vmem_limit_bytes=VMEM_LIMIT_BYTES),
        name="mixers",
    )(ret, ret, ret, ret, decf, decb, gain, aq, akv, slope_cols, sink_cols)


_FFN_SUB_TILES = 2
_FFN_HIDDEN_CHUNK = 256
assert FFN_HIDDEN % _FFN_HIDDEN_CHUNK == 0


def _layer_norm(z, gain, bias):
    mu = jnp.mean(z, axis=-1, keepdims=True)
    dlt = z - mu
    var = jnp.mean(dlt * dlt, axis=-1, keepdims=True)
    return dlt * lax.rsqrt(var + LN_EPS) * gain + bias


def _ffn_kernel(x_ref, yr_ref, ya_ref, p_ref, wo_ref, g1_ref, b1_ref, wg_ref, wu_ref, wd_ref,
                wpe_ref, wpg_ref, g2_ref, b2_ref, o_ref, *, alpha):
    tm = x_ref.shape[0]
    rows = [pl.ds(k * (tm // _FFN_SUB_TILES), tm // _FFN_SUB_TILES) for k in range(_FFN_SUB_TILES)]
    mix = [_dot(yr_ref[r, :], wo_ref[0:RET_WIDTH, :]) + _dot(ya_ref[r, :], wo_ref[RET_WIDTH:, :])
           for r in rows]
    h = [_layer_norm(alpha * x_ref[r, :] + m, g1_ref[...], b1_ref[...]) for r, m in zip(rows, mix)]
    z = []
    for r, hk in zip(rows, h):
        hb = hk.astype(BF16)
        ple = _dot(p_ref[r, :].astype(BF16), wpe_ref[...])
        ple_gate = jax.nn.sigmoid(_dot(hb, wpg_ref[...]))
        rest = alpha * hk + ple_gate * ple
        acts, acc = [], rest
        for c in range(FFN_HIDDEN // _FFN_HIDDEN_CHUNK + 1):
            if c < FFN_HIDDEN // _FFN_HIDDEN_CHUNK:
                cols = pl.ds(c * _FFN_HIDDEN_CHUNK, _FFN_HIDDEN_CHUNK)
                g = _dot(hb, wg_ref[:, cols])
                up = _dot(hb, wu_ref[:, cols])
                acts.append((g * jax.nn.sigmoid(g) * up).astype(BF16))
            if c >= 1:
                acc = acc + _dot(acts[c - 1], wd_ref[pl.ds((c - 1) * _FFN_HIDDEN_CHUNK, _FFN_HIDDEN_CHUNK), :])
        z.append(acc)
    for r, zk in zip(rows, z):
        o_ref[r, :] = _layer_norm(zk, g2_ref[...], b2_ref[...])


def _ffn(x, yr, ya, p, wo, g1, b1, wg, wu, wd, wpe, wpg, g2, b2, *, alpha, tm):
    t, d = x.shape

    def rows(width):
        return pl.BlockSpec((tm, width), lambda i: (i, 0))

    def resident(arr):
        return pl.BlockSpec(arr.shape, lambda i: (0, 0), pipeline_mode=pl.Buffered(1))

    return pl.pallas_call(
        functools.partial(_ffn_kernel, alpha=alpha),
        grid=(t // tm,),
        in_specs=[rows(d), rows(RET_WIDTH), rows(ATTN_WIDTH), rows(PLE_DIM),
                  resident(wo), resident(g1), resident(b1), resident(wg), resident(wu),
                  resident(wd), resident(wpe), resident(wpg), resident(g2), resident(b2)],
        out_specs=rows(d),
        out_shape=jax.ShapeDtypeStruct((t, d), F32),
        compiler_params=pltpu.CompilerParams(
            dimension_semantics=("arbitrary",), vmem_limit_bytes=VMEM_LIMIT_BYTES),
        name="ffn",
    )(x, yr, ya, p, wo, g1, b1, wg, wu, wd, wpe, wpg, g2, b2)


def _stacked_head_cols(per_head):
    v = per_head.astype(F32).reshape(ATTN_KV_HEADS, 2, 2)
    v = jnp.transpose(v, (0, 2, 1))
    v = jnp.repeat(v, BLOCK, axis=2)
    return jnp.broadcast_to(v[:, :, None, :], v.shape[:2] + (_SUBLANES, _STACK))


def kernel(x, p, w_in, ret_decay_fwd, ret_decay_bwd, ret_gn_gain, attn_sink, w_out, ln1_gain, ln1_bias,
           w_ffn_gate, w_ffn_up, w_ffn_down, w_ple_proj, w_ple_gate, ln2_gain, ln2_bias):
    depth = w_in.shape[0]
    b, s, d = x.shape
    alpha = (2.0 * depth) ** 0.25
    slopes = 2.0 ** (-8.0 * (jnp.arange(ATTN_HEADS, dtype=F32) + 1.0) / ATTN_HEADS)
    slope_cols = _stacked_head_cols(slopes)

    def lane_rep(v):
        v = v.astype(F32).reshape(RET_PAIRS, HEADS_PER_LANE_BLOCK, 1)
        return jnp.broadcast_to(v, (RET_PAIRS, HEADS_PER_LANE_BLOCK, LANES))

    row = lambda v: v.astype(F32).reshape(1, -1)
    h = x.astype(F32)
    for i in range(depth):
        (ret, aq, akv), (wo, wg, wu, wd, wpe, wpg) = _in_proj(
            h, w_in, i, (w_out, w_ffn_gate, w_ffn_up, w_ffn_down, w_ple_proj, w_ple_gate), tm=IN_PROJ_TILE)
        y_ret, y_att = _mixers(ret, aq, akv, lane_rep(ret_decay_fwd[i]), lane_rep(ret_decay_bwd[i]),
                               ret_gn_gain[i].astype(F32).reshape(RET_PAIRS, 1, LANES),
                               slope_cols, _stacked_head_cols(attn_sink[i]))
        h = _ffn(h.reshape(b * s, d), y_ret.reshape(b * s, RET_WIDTH), y_att.reshape(b * s, ATTN_WIDTH),
                 p[i].reshape(b * s, PLE_DIM),
                 wo, row(ln1_gain[i]), row(ln1_bias[i]), wg, wu, wd, wpe, wpg,
                 row(ln2_gain[i]), row(ln2_bias[i]), alpha=alpha, tm=FFN_TILE).reshape(b, s, d)
    return h.astype(x.dtype)
```

```python
import functools

import jax
import jax.numpy as jnp
import numpy as np
from jax import lax
from jax.experimental import pallas as pl
from jax.experimental.pallas import tpu as pltpu

D_MODEL = 1024
HEAD_DIM = 64
RET_HEADS = 8
ATTN_HEADS = 8
ATTN_KV_HEADS = 2
ATTN_GROUP = ATTN_HEADS // ATTN_KV_HEADS
RET_WIDTH = RET_HEADS * HEAD_DIM
ATTN_WIDTH = ATTN_HEADS * HEAD_DIM
KV_WIDTH = ATTN_KV_HEADS * HEAD_DIM
IN_PROJ_WIDTH = 4 * RET_WIDTH + ATTN_WIDTH + 2 * KV_WIDTH
CHUNK = 128
WINDOW = 128
BLOCK = 128
PLE_DIM = 256
FFN_HIDDEN = ((8 * D_MODEL + 767) // 768) * 256
LN_EPS = 1e-5
GN_EPS = 1e-5
NEG_INF = -1e30
QK_SCALE = HEAD_DIM ** -0.5

LANES = 128
HEADS_PER_LANE_BLOCK = LANES // HEAD_DIM
RET_PAIRS = RET_HEADS // HEADS_PER_LANE_BLOCK
VMEM_LIMIT_BYTES = 56 * 1024 * 1024
IN_PROJ_TILE = 1024
FFN_TILE = 512

F32 = jnp.float32
BF16 = jnp.bfloat16


def _dot(a, b):
    return jnp.dot(a, b, preferred_element_type=F32)


def _dot_nt(a, b):
    return lax.dot_general(a, b, (((1,), (1,)), ((), ())), preferred_element_type=F32)


def _dot_tn(a, b):
    return lax.dot_general(a, b, (((0,), (0,)), ((), ())), preferred_element_type=F32)


def _round_robin(*bodies):
    alive = list(bodies)
    while alive:
        for body in list(alive):
            try:
                next(body)
                yield
            except StopIteration:
                alive.remove(body)


def _interleave(bodies):
    progress = [0] * len(bodies)
    alive = [True] * len(bodies)
    while any(alive):
        k = min((i for i in range(len(bodies)) if alive[i]),
                key=lambda i: progress[i] / max(bodies[i][1], 1))
        try:
            next(bodies[k][0])
            progress[k] += 1
        except StopIteration:
            alive[k] = False


_IN_PROJ_COLS = 256
_RET_BLOCKS = 4 * RET_PAIRS
_KV_COL0 = 4 * RET_WIDTH + ATTN_WIDTH
_BF16_ROWS = 16


def _cast_rows_per_step(n_rows, n_steps):
    for rows in range(_BF16_ROWS, n_rows + 1, _BF16_ROWS):
        if n_rows % rows == 0 and n_rows // rows <= n_steps:
            return rows
    raise ValueError(f"no bf16-tile row split of {n_rows} rows over {n_steps} steps")


_X_RING = 3


def _x_tile_copy(x_hbm, xbuf_ref, sem_ref, step, n_j, tm):
    slot = lax.rem(step, _X_RING)
    return pltpu.make_async_copy(
        x_hbm.at[lax.div(step, n_j), pl.ds(pl.multiple_of(lax.rem(step, n_j) * tm, tm), tm), :],
        xbuf_ref.at[slot], sem_ref.at[slot])


def _in_proj_kernel(x_hbm, w_ref, *refs, n_cast, n_steps, n_j, tm):
    cast_in, (ret_ref, aq_ref, akv_ref) = refs[:n_cast], refs[n_cast:n_cast + 3]
    cast_out, (wb_ref, xbuf_ref, sem_ref) = refs[n_cast + 3:2 * n_cast + 3], refs[2 * n_cast + 3:]
    step = pl.program_id(0) * n_j + pl.program_id(1)

    @pl.when(step == 0)
    def _():
        for s in range(min(_X_RING - 1, n_steps)):
            _x_tile_copy(x_hbm, xbuf_ref, sem_ref, jnp.int32(s), n_j, tm).start()

    @pl.when(step + (_X_RING - 1) < n_steps)
    def _():
        _x_tile_copy(x_hbm, xbuf_ref, sem_ref, step + (_X_RING - 1), n_j, tm).start()

    @pl.when((pl.program_id(0) == 0) & (pl.program_id(1) == 0))
    def _():
        wb_ref[:, 0:_KV_COL0] = w_ref[:, 0:_KV_COL0].astype(BF16)
        low = lax.broadcasted_iota(jnp.int32, (1, LANES), 1) < HEAD_DIM
        ak = w_ref[:, _KV_COL0:_KV_COL0 + KV_WIDTH]
        av = w_ref[:, _KV_COL0 + KV_WIDTH:_KV_COL0 + 2 * KV_WIDTH]
        wb_ref[:, _KV_COL0:_KV_COL0 + LANES] = jnp.where(
            low, ak, pltpu.roll(av, HEAD_DIM, 1)).astype(BF16)
        wb_ref[:, _KV_COL0 + LANES:_KV_COL0 + 2 * LANES] = jnp.where(
            low, pltpu.roll(ak, HEAD_DIM, 1), av).astype(BF16)

    for src, dst in zip(cast_in, cast_out):
        dst[...] = src[...].astype(BF16)

    _x_tile_copy(x_hbm, xbuf_ref, sem_ref, step, n_j, tm).wait()
    xb = xbuf_ref[lax.rem(step, _X_RING)].astype(BF16)
    for c in range(IN_PROJ_WIDTH // _IN_PROJ_COLS):
        u = _dot(xb, wb_ref[:, c * _IN_PROJ_COLS:(c + 1) * _IN_PROJ_COLS]).astype(BF16)
        if 2 * c < _RET_BLOCKS:
            ret_ref[0, 2 * c] = u[:, :LANES]
            ret_ref[0, 2 * c + 1] = u[:, LANES:]
        elif c < _RET_BLOCKS // 2 + ATTN_KV_HEADS:
            aq_ref[0, c - _RET_BLOCKS // 2] = u
        else:
            akv_ref[0, 0] = u[:, :LANES]
            akv_ref[0, 1] = u[:, LANES:]


def _cast_specs(cast_weights, layer, n_steps, step_of):
    in_specs, out_specs, out_shapes = [], [], []
    for w in cast_weights:
        rows = _cast_rows_per_step(w.shape[1], n_steps)
        last = w.shape[1] // rows - 1
        in_specs.append(pl.BlockSpec(
            (None, rows, w.shape[2]),
            lambda *idx, last=last: (layer, jnp.minimum(step_of(*idx), last), 0)))
        out_specs.append(pl.BlockSpec(
            (rows, w.shape[2]), lambda *idx, last=last: (jnp.minimum(step_of(*idx), last), 0)))
        out_shapes.append(jax.ShapeDtypeStruct(w.shape[1:], BF16))
    return in_specs, out_specs, out_shapes


def _in_proj(x, w_in, layer, cast_weights, tm):
    b, s, d = x.shape
    n_j = s // tm
    cast_in_specs, cast_out_specs, cast_out_shapes = _cast_specs(
        cast_weights, layer, b * n_j, lambda i, j: i * n_j + j)

    outs = pl.pallas_call(
        functools.partial(_in_proj_kernel, n_cast=len(cast_weights), n_steps=b * n_j, n_j=n_j, tm=tm),
        grid=(b, n_j),
        in_specs=[
            pl.BlockSpec(memory_space=pl.ANY),
            pl.BlockSpec((None, d, IN_PROJ_WIDTH), lambda i, j: (layer, 0, 0),
                         pipeline_mode=pl.Buffered(1)),
        ] + cast_in_specs,
        out_specs=[
            pl.BlockSpec((1, _RET_BLOCKS, tm, LANES), lambda i, j: (i, 0, j, 0)),
            pl.BlockSpec((1, ATTN_KV_HEADS, tm, ATTN_GROUP * HEAD_DIM), lambda i, j: (i, 0, j, 0)),
            pl.BlockSpec((1, ATTN_KV_HEADS, tm, LANES), lambda i, j: (i, 0, j, 0)),
        ] + cast_out_specs,
        out_shape=[
            jax.ShapeDtypeStruct((b, _RET_BLOCKS, s, LANES), BF16),
            jax.ShapeDtypeStruct((b, ATTN_KV_HEADS, s, ATTN_GROUP * HEAD_DIM), BF16),
            jax.ShapeDtypeStruct((b, ATTN_KV_HEADS, s, LANES), BF16),
        ] + cast_out_shapes,
        scratch_shapes=[pltpu.VMEM((d, IN_PROJ_WIDTH), BF16),
                        pltpu.VMEM((_X_RING, tm, d), F32),
                        pltpu.SemaphoreType.DMA((_X_RING,))],
        compiler_params=pltpu.CompilerParams(
            dimension_semantics=("arbitrary", "arbitrary"), vmem_limit_bytes=VMEM_LIMIT_BYTES),
        name="in_proj",
    )(x, w_in, *cast_weights)
    return outs[:3], outs[3:]


def _ret_body(q_ref, k_ref, v_ref, g_ref, decf_ref, decb_ref, gain_ref, o_ref, kv_ref, st_ref,
              p_ref, y_ref, *, seq, pp):
    n_chunks = seq // CHUNK
    kv_ref, st_ref, p_ref, y_ref = kv_ref.at[pp], st_ref.at[pp], p_ref.at[pp], y_ref.at[pp]
    out_lanes = pl.ds(pp * LANES, LANES)
    lane = lax.broadcasted_iota(jnp.int32, (1, LANES), 1)
    head0 = lane < HEAD_DIM

    def log_gamma(dec):
        return jnp.log1p(-jnp.exp2(dec))

    lgf0, lgf1 = log_gamma(decf_ref[pp, 0:1, :]), log_gamma(decf_ref[pp, 1:2, :])
    lgb0, lgb1 = log_gamma(decb_ref[pp, 0:1, :]), log_gamma(decb_ref[pp, 1:2, :])
    lgf = jnp.where(head0, lgf0, lgf1)
    lgb = jnp.where(head0, lgb0, lgb1)

    idx = lax.broadcasted_iota(jnp.int32, (CHUNK, LANES), 0).astype(F32)
    qdf = jnp.exp(lgf * (idx + 1.0))
    kdf = jnp.exp(lgf * (CHUNK - 1.0 - idx)) * QK_SCALE
    qdb = jnp.exp(lgb * (CHUNK - idx))
    kdb = jnp.exp(lgb * idx) * QK_SCALE
    cdf = jnp.exp(lgf * CHUNK)
    cdb = jnp.exp(lgb * CHUNK)

    ri = lax.broadcasted_iota(jnp.int32, (CHUNK, CHUNK), 0)
    ci = lax.broadcasted_iota(jnp.int32, (CHUNK, CHUNK), 1)
    diff = (ri - ci).astype(F32)

    def intra_mask(lf, lb):
        fwd = jnp.exp(lf * jnp.maximum(diff, 0.0))
        bwd = jnp.exp(lb * jnp.maximum(-diff, 0.0))
        return jnp.where(diff > 0, fwd, jnp.where(diff < 0, bwd, fwd + bwd))

    mask2 = jnp.concatenate([intra_mask(lgf0, lgb0), intra_mask(lgf1, lgb1)], axis=0) * QK_SCALE
    same_head = (ri < HEAD_DIM) == (ci < HEAD_DIM)
    gn_avg = jnp.where(same_head, 1.0 / HEAD_DIM, 0.0).astype(BF16)

    for n in range(n_chunks):
        sl = pl.ds(n * CHUNK, CHUNK)
        kc = k_ref[0, pp, sl, :].astype(F32)
        kd = jnp.concatenate([(kc * kdf).astype(BF16), (kc * kdb).astype(BF16)], axis=1)
        kv = _dot_tn(kd, v_ref[0, pp, sl, :])
        kv_ref[n] = kv
        yield

    zero_state = jnp.zeros((LANES, LANES), BF16)
    state = jnp.zeros((LANES, LANES), F32)
    for n in range(n_chunks):
        st_ref[n, 0:LANES, :] = jnp.where(same_head, state.astype(BF16), zero_state)
        state = state * cdf + kv_ref[n, 0:LANES, :]
    state = jnp.zeros((LANES, LANES), F32)
    for n in reversed(range(n_chunks)):
        st_ref[n, LANES:2 * LANES, :] = jnp.where(same_head, state.astype(BF16), zero_state)
        state = state * cdb + kv_ref[n, LANES:2 * LANES, :]

    zero = jnp.zeros((CHUNK, LANES), BF16)
    for n in range(n_chunks):
        sl = pl.ds(n * CHUNK, CHUNK)
        qb = q_ref[0, pp, sl, :]
        kb = k_ref[0, pp, sl, :]
        qs = jnp.concatenate([jnp.where(head0, qb, zero), jnp.where(head0, zero, qb)], axis=0)
        p = (_dot_nt(qs, kb) * mask2).astype(BF16)
        p_ref[n, :, 0:CHUNK] = p[0:CHUNK]
        p_ref[n, :, CHUNK:2 * CHUNK] = p[CHUNK:2 * CHUNK]
        yield

    for n in range(n_chunks):
        sl = pl.ds(n * CHUNK, CHUNK)
        vb = v_ref[0, pp, sl, :]
        qc = q_ref[0, pp, sl, :].astype(F32)
        lhs = jnp.concatenate([p_ref[n], (qc * qdf).astype(BF16), (qc * qdb).astype(BF16)], axis=1)
        rhs = jnp.concatenate([jnp.where(head0, vb, zero), jnp.where(head0, zero, vb), st_ref[n]],
                              axis=0)
        y_ref[sl, :] = _dot(lhs, rhs)
        yield

    def group_mean(z):
        return _dot(z.astype(BF16), gn_avg)

    for n in range(n_chunks):
        sl = pl.ds(n * CHUNK, CHUNK)
        y = y_ref[sl, :]
        y_ref[sl, :] = y - group_mean(y)
        yield

    gain = gain_ref[pp]
    for n in range(n_chunks):
        sl = pl.ds(n * CHUNK, CHUNK)
        dlt = y_ref[sl, :]
        var = group_mean(dlt * dlt)
        gate = g_ref[0, pp, sl, :].astype(F32)
        o_ref[0, sl, out_lanes] = (dlt * lax.rsqrt(var + GN_EPS) * gain
                                   * (gate * jax.nn.sigmoid(gate))).astype(BF16)
        yield


_RET_YIELDS_PER_PAIR = 5


_STACK = 2 * BLOCK
_KEYS = 3 * BLOCK
_SUBLANES = 8
_LOG2E = 1.4426950408889634
_ATTN_STAGE_LAG = (0, 2, 4, 5)


def _attn_body(q_ref, kv_ref, slope_ref, sink_ref, o_ref, ks_ref, kss_ref, vt_ref, bias_ref,
               *, seq):
    n_blocks = seq // BLOCK
    lane = lax.broadcasted_iota(jnp.int32, (1, LANES), 1)
    low = lane < HEAD_DIM

    for hk in range(ATTN_KV_HEADS):
        kv = kv_ref[0, hk].astype(F32)
        ks = kv * (QK_SCALE * _LOG2E)
        ks_ref[hk] = ks.astype(BF16)
        kss_ref[hk] = pltpu.roll(ks, HEAD_DIM, 1).astype(BF16)
        kv_t = jnp.transpose(kv)
        vt_ref[hk] = jnp.concatenate(
            [kv_t[HEAD_DIM:], jnp.ones((_BF16_ROWS, seq), F32)], axis=0).astype(BF16)

    kj = lax.broadcasted_iota(jnp.int32, (_KEYS, _STACK), 0)
    c = lax.broadcasted_iota(jnp.int32, (_KEYS, _STACK), 1)
    qi = jnp.where(c < BLOCK, c, c - BLOCK)
    dist = jnp.abs(kj - BLOCK - qi)
    for hk in range(ATTN_KV_HEADS):
        for par in range(2):
            slope = slope_ref[hk, par, 0:1, :]
            bias_ref[hk, par] = jnp.where(dist <= WINDOW, -slope * dist.astype(F32), NEG_INF) * _LOG2E
    yield

    zero = jnp.zeros((BLOCK, LANES), BF16)

    def key_range(n):
        lo_blk, hi_blk = max(n - 1, 0), min(n + 2, n_blocks)
        return pl.ds(lo_blk * BLOCK, (hi_blk - lo_blk) * BLOCK), (lo_blk - (n - 1)) * BLOCK, \
            (hi_blk - lo_blk) * BLOCK

    units = [(hk, n, par) for hk in range(ATTN_KV_HEADS) for n in range(n_blocks) for par in range(2)]
    live = [dict() for _ in units]

    def stage_scores(u):
        hk, n, par = units[u]
        keys, c0, width = key_range(n)
        qblk = q_ref[0, hk, pl.ds(n * BLOCK, BLOCK), :]
        qa, qb = qblk[:, :LANES], qblk[:, LANES:]
        if par == 0:
            qs = jnp.concatenate([jnp.where(low, qa, zero), jnp.where(low, qb, zero)], axis=0)
            kmat = ks_ref[hk, keys, :]
        else:
            qs = jnp.concatenate([jnp.where(low, zero, qa), jnp.where(low, zero, qb)], axis=0)
            kmat = kss_ref[hk, keys, :]
        live[u]["s"] = _dot_nt(kmat, qs) + bias_ref[hk, par, c0:c0 + width, :]

    def stage_softmax(u):
        hk, _, par = units[u]
        st = live[u]
        s = st.pop("s")
        sink = sink_ref[hk, par, 0:1, :] * _LOG2E
        m = jnp.maximum(jnp.max(s, axis=0, keepdims=True), sink)
        st["e"] = jnp.exp2(s - m).astype(BF16)
        st["esink"] = jnp.exp2(sink - m)

    def stage_values(u):
        hk, n, _ = units[u]
        keys, _, _ = key_range(n)
        live[u]["pv"] = _dot(vt_ref[hk, :, keys], live[u].pop("e"))

    def stage_out(u):
        hk, n, par = units[u]
        st = live[u]
        pv = st.pop("pv")
        st["out"] = pv[:HEAD_DIM] / (pv[HEAD_DIM:HEAD_DIM + 1] + st.pop("esink"))
        if par == 1:
            even, odd = live[u - 1].pop("out"), st.pop("out")
            y_t = jnp.concatenate(
                [even[:, :BLOCK], odd[:, :BLOCK], even[:, BLOCK:], odd[:, BLOCK:]], axis=0)
            o_ref[0, pl.ds(n * BLOCK, BLOCK), pl.ds(hk * _STACK, _STACK)] = \
                jnp.transpose(y_t).astype(BF16)

    stages = (stage_scores, stage_softmax, stage_values, stage_out)
    for i in range(len(units) + _ATTN_STAGE_LAG[-1]):
        for lag, stage in zip(_ATTN_STAGE_LAG, stages):
            if 0 <= i - lag < len(units):
                stage(i - lag)
        yield


def _mixers_kernel(q_ref, k_ref, v_ref, g_ref, decf_ref, decb_ref, gain_ref,
                   aq_ref, akv_ref, slope_ref, sink_ref, yr_ref, ya_ref,
                   kv_ref, st_ref, p_ref, y_ref, ks_ref, kss_ref, vt_ref, bias_ref, *, seq):
    ret_args = (q_ref, k_ref, v_ref, g_ref, decf_ref, decb_ref, gain_ref, yr_ref,
                kv_ref, st_ref, p_ref, y_ref)
    n_units = ATTN_KV_HEADS * (seq // BLOCK) * 2
    _interleave([
        (_round_robin(*[_ret_body(*ret_args, seq=seq, pp=pp) for pp in range(RET_PAIRS)]),
         RET_PAIRS * _RET_YIELDS_PER_PAIR * (seq // CHUNK)),
        (_attn_body(aq_ref, akv_ref, slope_ref, sink_ref, ya_ref, ks_ref, kss_ref, vt_ref, bias_ref,
                    seq=seq), 1 + n_units + _ATTN_STAGE_LAG[-1]),
    ])


def _mixers(ret, aq, akv, decf, decb, gain, slope_cols, sink_cols):
    b, _, s, _ = ret.shape
    n_chunks = s // CHUNK
    qw = aq.shape[3]

    def part(kind):
        return pl.BlockSpec((1, RET_PAIRS, s, LANES), lambda i, kind=kind: (i, kind, 0, 0))

    def whole(arr):
        return pl.BlockSpec(arr.shape, lambda i: (0,) * arr.ndim)

    return pl.pallas_call(
        functools.partial(_mixers_kernel, seq=s),
        grid=(b,),
        in_specs=[
            part(0), part(1), part(2), part(3), whole(decf), whole(decb), whole(gain),
            pl.BlockSpec((1, ATTN_KV_HEADS, s, qw), lambda i: (i, 0, 0, 0)),
            pl.BlockSpec((1, ATTN_KV_HEADS, s, LANES), lambda i: (i, 0, 0, 0)),
            whole(slope_cols), whole(sink_cols),
        ],
        out_specs=[pl.BlockSpec((1, s, RET_WIDTH), lambda i: (i, 0, 0)),
                   pl.BlockSpec((1, s, ATTN_WIDTH), lambda i: (i, 0, 0))],
        out_shape=[jax.ShapeDtypeStruct((b, s, RET_WIDTH), BF16),
                   jax.ShapeDtypeStruct((b, s, ATTN_WIDTH), BF16)],
        scratch_shapes=[
            pltpu.VMEM((RET_PAIRS, n_chunks, 2 * LANES, LANES), F32),
            pltpu.VMEM((RET_PAIRS, n_chunks, 2 * LANES, LANES), BF16),
            pltpu.VMEM((RET_PAIRS, n_chunks, CHUNK, 2 * CHUNK), BF16),
            pltpu.VMEM((RET_PAIRS, s, LANES), F32),
            pltpu.VMEM((ATTN_KV_HEADS, s, LANES), BF16),
            pltpu.VMEM((ATTN_KV_HEADS, s, LANES), BF16),
            pltpu.VMEM((ATTN_KV_HEADS, HEAD_DIM + _BF16_ROWS, s), BF16),
            pltpu.VMEM((ATTN_KV_HEADS, 2, _KEYS, _STACK), F32),
        ],
        compiler_params=pltpu.CompilerParams(
            dimension_semantics=("arbitrary",), vmem_limit_bytes=VMEM_LIMIT_BYTES),
        name="mixers",
    )(ret, ret, ret, ret, decf, decb, gain, aq, akv, slope_cols, sink_cols)


_FFN_SUB_TILES = 2
_FFN_HIDDEN_CHUNK = 256
assert FFN_HIDDEN % _FFN_HIDDEN_CHUNK == 0


def _layer_norm(z, gain, bias):
    mu = jnp.mean(z, axis=-1, keepdims=True)
    dlt = z - mu
    var = jnp.mean(dlt * dlt, axis=-1, keepdims=True)
    return dlt * lax.rsqrt(var + LN_EPS) * gain + bias


def _ffn_kernel(x_ref, yr_ref, ya_ref, p_ref, wo_ref, g1_ref, b1_ref, wg_ref, wu_ref, wd_ref,
                wpe_ref, wpg_ref, g2_ref, b2_ref, o_ref, *, alpha):
    tm = x_ref.shape[0]
    rows = [pl.ds(k * (tm // _FFN_SUB_TILES), tm // _FFN_SUB_TILES) for k in range(_FFN_SUB_TILES)]
    mix = [_dot(yr_ref[r, :], wo_ref[0:RET_WIDTH, :]) + _dot(ya_ref[r, :], wo_ref[RET_WIDTH:, :])
           for r in rows]
    h = [_layer_norm(alpha * x_ref[r, :] + m, g1_ref[...], b1_ref[...]) for r, m in zip(rows, mix)]
    z = []
    for r, hk in zip(rows, h):
        hb = hk.astype(BF16)
        ple = _dot(p_ref[r, :].astype(BF16), wpe_ref[...])
        ple_gate = jax.nn.sigmoid(_dot(hb, wpg_ref[...]))
        rest = alpha * hk + ple_gate * ple
        acts, acc = [], rest
        for c in range(FFN_HIDDEN // _FFN_HIDDEN_CHUNK + 1):
            if c < FFN_HIDDEN // _FFN_HIDDEN_CHUNK:
                cols = pl.ds(c * _FFN_HIDDEN_CHUNK, _FFN_HIDDEN_CHUNK)
                g = _dot(hb, wg_ref[:, cols])
                up = _dot(hb, wu_ref[:, cols])
                acts.append((g * jax.nn.sigmoid(g) * up).astype(BF16))
            if c >= 1:
                acc = acc + _dot(acts[c - 1], wd_ref[pl.ds((c - 1) * _FFN_HIDDEN_CHUNK, _FFN_HIDDEN_CHUNK), :])
        z.append(acc)
    for r, zk in zip(rows, z):
        o_ref[r, :] = _layer_norm(zk, g2_ref[...], b2_ref[...])


def _ffn(x, yr, ya, p, wo, g1, b1, wg, wu, wd, wpe, wpg, g2, b2, *, alpha, tm):
    t, d = x.shape

    def rows(width):
        return pl.BlockSpec((tm, width), lambda i: (i, 0))

    def resident(arr):
        return pl.BlockSpec(arr.shape, lambda i: (0, 0), pipeline_mode=pl.Buffered(1))

    return pl.pallas_call(
        functools.partial(_ffn_kernel, alpha=alpha),
        grid=(t // tm,),
        in_specs=[rows(d), rows(RET_WIDTH), rows(ATTN_WIDTH), rows(PLE_DIM),
                  resident(wo), resident(g1), resident(b1), resident(wg), resident(wu),
                  resident(wd), resident(wpe), resident(wpg), resident(g2), resident(b2)],
        out_specs=rows(d),
        out_shape=jax.ShapeDtypeStruct((t, d), F32),
        compiler_params=pltpu.CompilerParams(
            dimension_semantics=("arbitrary",), vmem_limit_bytes=VMEM_LIMIT_BYTES),
        name="ffn",
    )(x, yr, ya, p, wo, g1, b1, wg, wu, wd, wpe, wpg, g2, b2)


def _stacked_head_cols(per_head):
    xp = np if isinstance(per_head, np.ndarray) else jnp
    v = per_head.astype(xp.float32).reshape(ATTN_KV_HEADS, 2, 2)
    v = xp.transpose(v, (0, 2, 1))
    v = xp.broadcast_to(v[:, :, None, :, None], (ATTN_KV_HEADS, 2, _SUBLANES, 2, BLOCK))
    return v.reshape(ATTN_KV_HEADS, 2, _SUBLANES, _STACK)


def kernel(x, p, w_in, ret_decay_fwd, ret_decay_bwd, ret_gn_gain, attn_sink, w_out, ln1_gain, ln1_bias,
           w_ffn_gate, w_ffn_up, w_ffn_down, w_ple_proj, w_ple_gate, ln2_gain, ln2_bias):
    depth = w_in.shape[0]
    b, s, d = x.shape
    alpha = (2.0 * depth) ** 0.25
    slopes = np.float32(2.0) ** (-8.0 * (np.arange(ATTN_HEADS, dtype=np.float32) + 1.0) / ATTN_HEADS)
    slope_cols = jnp.asarray(_stacked_head_cols(slopes))

    def lane_rep(v):
        v = v.astype(F32).reshape(RET_PAIRS, HEADS_PER_LANE_BLOCK, 1)
        return jnp.broadcast_to(v, (RET_PAIRS, HEADS_PER_LANE_BLOCK, LANES))

    row = lambda v: v.astype(F32).reshape(1, -1)
    h = x.astype(F32)
    for i in range(depth):
        (ret, aq, akv), (wo, wg, wu, wd, wpe, wpg) = _in_proj(
            h, w_in, i, (w_out, w_ffn_gate, w_ffn_up, w_ffn_down, w_ple_proj, w_ple_gate), tm=IN_PROJ_TILE)
        y_ret, y_att = _mixers(ret, aq, akv, lane_rep(ret_decay_fwd[i]), lane_rep(ret_decay_bwd[i]),
                               ret_gn_gain[i].astype(F32).reshape(RET_PAIRS, 1, LANES),
                               slope_cols, _stacked_head_cols(attn_sink[i]))
        h = _ffn(h.reshape(b * s, d), y_ret.reshape(b * s, RET_WIDTH), y_att.reshape(b * s, ATTN_WIDTH),
                 p[i].reshape(b * s, PLE_DIM),
                 wo, row(ln1_gain[i]), row(ln1_bias[i]), wg, wu, wd, wpe, wpg,
                 row(ln2_gain[i]), row(ln2_bias[i]), alpha=alpha, tm=FFN_TILE).reshape(b, s, d)
    return h.astype(x.dtype)
```

```python
import functools

import jax
import jax.numpy as jnp
from jax import lax
from jax.experimental import pallas as pl
from jax.experimental.pallas import tpu as pltpu

D_MODEL = 1024
HEAD_DIM = 64
RET_HEADS = 8
ATTN_HEADS = 8
ATTN_KV_HEADS = 2
ATTN_GROUP = ATTN_HEADS // ATTN_KV_HEADS
RET_WIDTH = RET_HEADS * HEAD_DIM
ATTN_WIDTH = ATTN_HEADS * HEAD_DIM
KV_WIDTH = ATTN_KV_HEADS * HEAD_DIM
IN_PROJ_WIDTH = 4 * RET_WIDTH + ATTN_WIDTH + 2 * KV_WIDTH
CHUNK = 128
WINDOW = 128
BLOCK = 128
PLE_DIM = 256
FFN_HIDDEN = ((8 * D_MODEL + 767) // 768) * 256
LN_EPS = 1e-5
GN_EPS = 1e-5
NEG_INF = -1e30
QK_SCALE = HEAD_DIM ** -0.5

LANES = 128
HEADS_PER_LANE_BLOCK = LANES // HEAD_DIM
RET_PAIRS = RET_HEADS // HEADS_PER_LANE_BLOCK
VMEM_LIMIT_BYTES = 56 * 1024 * 1024
IN_PROJ_TILE = 1024
FFN_TILE = 512

F32 = jnp.float32
BF16 = jnp.bfloat16


def _dot(a, b):
    return jnp.dot(a, b, preferred_element_type=F32)


def _dot_nt(a, b):
    return lax.dot_general(a, b, (((1,), (1,)), ((), ())), preferred_element_type=F32)


def _dot_tn(a, b):
    return lax.dot_general(a, b, (((0,), (0,)), ((), ())), preferred_element_type=F32)


def _round_robin(*bodies):
    alive = list(bodies)
    while alive:
        for body in list(alive):
            try:
                next(body)
                yield
            except StopIteration:
                alive.remove(body)


def _interleave(bodies):
    progress = [0] * len(bodies)
    alive = [True] * len(bodies)
    while any(alive):
        k = min((i for i in range(len(bodies)) if alive[i]),
                key=lambda i: progress[i] / max(bodies[i][1], 1))
        try:
            next(bodies[k][0])
            progress[k] += 1
        except StopIteration:
            alive[k] = False


_IN_PROJ_COLS = 256
_RET_BLOCKS = 4 * RET_PAIRS
_KV_COL0 = 4 * RET_WIDTH + ATTN_WIDTH
_BF16_ROWS = 16


def _cast_rows_per_step(n_rows, n_steps):
    for rows in range(_BF16_ROWS, n_rows + 1, _BF16_ROWS):
        if n_rows % rows == 0 and n_rows // rows <= n_steps:
            return rows
    raise ValueError(f"no bf16-tile row split of {n_rows} rows over {n_steps} steps")


_X_RING = 3
_X_DMA_PRIORITY = 1


def _x_tile_copy(x_hbm, xbuf_ref, sem_ref, step, n_j, tm):
    slot = lax.rem(step, _X_RING)
    return pltpu.make_async_copy(
        x_hbm.at[lax.div(step, n_j), pl.ds(pl.multiple_of(lax.rem(step, n_j) * tm, tm), tm), :],
        xbuf_ref.at[slot], sem_ref.at[slot])


def _in_proj_kernel(x_hbm, w_ref, *refs, n_cast, n_steps, n_j, tm):
    cast_in, (ret_ref, aq_ref, akv_ref) = refs[:n_cast], refs[n_cast:n_cast + 3]
    cast_out, (wb_ref, xbuf_ref, sem_ref) = refs[n_cast + 3:2 * n_cast + 3], refs[2 * n_cast + 3:]
    step = pl.program_id(0) * n_j + pl.program_id(1)

    @pl.when(step == 0)
    def _():
        for s in range(min(_X_RING - 1, n_steps)):
            _x_tile_copy(x_hbm, xbuf_ref, sem_ref, jnp.int32(s), n_j, tm).start(priority=_X_DMA_PRIORITY)

    @pl.when(step + (_X_RING - 1) < n_steps)
    def _():
        _x_tile_copy(x_hbm, xbuf_ref, sem_ref, step + (_X_RING - 1), n_j, tm).start(
            priority=_X_DMA_PRIORITY)

    @pl.when((pl.program_id(0) == 0) & (pl.program_id(1) == 0))
    def _():
        wb_ref[:, 0:_KV_COL0] = w_ref[:, 0:_KV_COL0].astype(BF16)
        low = lax.broadcasted_iota(jnp.int32, (1, LANES), 1) < HEAD_DIM
        ak = w_ref[:, _KV_COL0:_KV_COL0 + KV_WIDTH]
        av = w_ref[:, _KV_COL0 + KV_WIDTH:_KV_COL0 + 2 * KV_WIDTH]
        wb_ref[:, _KV_COL0:_KV_COL0 + LANES] = jnp.where(
            low, ak, pltpu.roll(av, HEAD_DIM, 1)).astype(BF16)
        wb_ref[:, _KV_COL0 + LANES:_KV_COL0 + 2 * LANES] = jnp.where(
            low, pltpu.roll(ak, HEAD_DIM, 1), av).astype(BF16)

    for src, dst in zip(cast_in, cast_out):
        dst[...] = src[...].astype(BF16)

    _x_tile_copy(x_hbm, xbuf_ref, sem_ref, step, n_j, tm).wait()
    xb = xbuf_ref[lax.rem(step, _X_RING)].astype(BF16)
    for c in range(IN_PROJ_WIDTH // _IN_PROJ_COLS):
        u = _dot(xb, wb_ref[:, c * _IN_PROJ_COLS:(c + 1) * _IN_PROJ_COLS]).astype(BF16)
        if 2 * c < _RET_BLOCKS:
            ret_ref[0, 2 * c] = u[:, :LANES]
            ret_ref[0, 2 * c + 1] = u[:, LANES:]
        elif c < _RET_BLOCKS // 2 + ATTN_KV_HEADS:
            aq_ref[0, c - _RET_BLOCKS // 2] = u
        else:
            akv_ref[0, 0] = u[:, :LANES]
            akv_ref[0, 1] = u[:, LANES:]


def _cast_specs(cast_weights, layer, n_steps, step_of):
    in_specs, out_specs, out_shapes = [], [], []
    for w in cast_weights:
        rows = _cast_rows_per_step(w.shape[1], n_steps)
        last = w.shape[1] // rows - 1
        in_specs.append(pl.BlockSpec(
            (None, rows, w.shape[2]),
            lambda *idx, last=last: (layer, jnp.minimum(step_of(*idx), last), 0)))
        out_specs.append(pl.BlockSpec(
            (rows, w.shape[2]), lambda *idx, last=last: (jnp.minimum(step_of(*idx), last), 0)))
        out_shapes.append(jax.ShapeDtypeStruct(w.shape[1:], BF16))
    return in_specs, out_specs, out_shapes


def _in_proj(x, w_in, layer, cast_weights, tm):
    b, s, d = x.shape
    n_j = s // tm
    cast_in_specs, cast_out_specs, cast_out_shapes = _cast_specs(
        cast_weights, layer, b * n_j, lambda i, j: i * n_j + j)

    outs = pl.pallas_call(
        functools.partial(_in_proj_kernel, n_cast=len(cast_weights), n_steps=b * n_j, n_j=n_j, tm=tm),
        grid=(b, n_j),
        in_specs=[
            pl.BlockSpec(memory_space=pl.ANY),
            pl.BlockSpec((None, d, IN_PROJ_WIDTH), lambda i, j: (layer, 0, 0),
                         pipeline_mode=pl.Buffered(1)),
        ] + cast_in_specs,
        out_specs=[
            pl.BlockSpec((1, _RET_BLOCKS, tm, LANES), lambda i, j: (i, 0, j, 0)),
            pl.BlockSpec((1, ATTN_KV_HEADS, tm, ATTN_GROUP * HEAD_DIM), lambda i, j: (i, 0, j, 0)),
            pl.BlockSpec((1, ATTN_KV_HEADS, tm, LANES), lambda i, j: (i, 0, j, 0)),
        ] + cast_out_specs,
        out_shape=[
            jax.ShapeDtypeStruct((b, _RET_BLOCKS, s, LANES), BF16),
            jax.ShapeDtypeStruct((b, ATTN_KV_HEADS, s, ATTN_GROUP * HEAD_DIM), BF16),
            jax.ShapeDtypeStruct((b, ATTN_KV_HEADS, s, LANES), BF16),
        ] + cast_out_shapes,
        scratch_shapes=[pltpu.VMEM((d, IN_PROJ_WIDTH), BF16),
                        pltpu.VMEM((_X_RING, tm, d), F32),
                        pltpu.SemaphoreType.DMA((_X_RING,))],
        compiler_params=pltpu.CompilerParams(
            dimension_semantics=("arbitrary", "arbitrary"), vmem_limit_bytes=VMEM_LIMIT_BYTES),
        name="in_proj",
    )(x, w_in, *cast_weights)
    return outs[:3], outs[3:]


def _ret_body(q_ref, k_ref, v_ref, g_ref, decf_ref, decb_ref, gain_ref, o_ref, kv_ref, st_ref,
              p_ref, y_ref, *, seq, pp):
    n_chunks = seq // CHUNK
    kv_ref, st_ref, p_ref, y_ref = kv_ref.at[pp], st_ref.at[pp], p_ref.at[pp], y_ref.at[pp]
    out_lanes = pl.ds(pp * LANES, LANES)
    lane = lax.broadcasted_iota(jnp.int32, (1, LANES), 1)
    head0 = lane < HEAD_DIM

    def log_gamma(dec):
        return jnp.log1p(-jnp.exp2(dec))

    lgf0, lgf1 = log_gamma(decf_ref[pp, 0:1, :]), log_gamma(decf_ref[pp, 1:2, :])
    lgb0, lgb1 = log_gamma(decb_ref[pp, 0:1, :]), log_gamma(decb_ref[pp, 1:2, :])
    lgf = jnp.where(head0, lgf0, lgf1)
    lgb = jnp.where(head0, lgb0, lgb1)

    idx = lax.broadcasted_iota(jnp.int32, (CHUNK, LANES), 0).astype(F32)
    qdf = jnp.exp(lgf * (idx + 1.0))
    kdf = jnp.exp(lgf * (CHUNK - 1.0 - idx)) * QK_SCALE
    qdb = jnp.exp(lgb * (CHUNK - idx))
    kdb = jnp.exp(lgb * idx) * QK_SCALE
    cdf = jnp.exp(lgf * CHUNK)
    cdb = jnp.exp(lgb * CHUNK)

    ri = lax.broadcasted_iota(jnp.int32, (CHUNK, CHUNK), 0)
    ci = lax.broadcasted_iota(jnp.int32, (CHUNK, CHUNK), 1)
    diff = (ri - ci).astype(F32)

    def intra_mask(lf, lb):
        fwd = jnp.exp(lf * jnp.maximum(diff, 0.0))
        bwd = jnp.exp(lb * jnp.maximum(-diff, 0.0))
        return jnp.where(diff > 0, fwd, jnp.where(diff < 0, bwd, fwd + bwd))

    mask2 = jnp.concatenate([intra_mask(lgf0, lgb0), intra_mask(lgf1, lgb1)], axis=0) * QK_SCALE
    same_head = (ri < HEAD_DIM) == (ci < HEAD_DIM)
    gn_avg = jnp.where(same_head, 1.0 / HEAD_DIM, 0.0).astype(BF16)

    for n in range(n_chunks):
        sl = pl.ds(n * CHUNK, CHUNK)
        kc = k_ref[0, pp, sl, :].astype(F32)
        kd = jnp.concatenate([(kc * kdf).astype(BF16), (kc * kdb).astype(BF16)], axis=1)
        kv = _dot_tn(kd, v_ref[0, pp, sl, :])
        kv_ref[n] = kv
        yield

    zero_state = jnp.zeros((LANES, LANES), BF16)
    state = jnp.zeros((LANES, LANES), F32)
    for n in range(n_chunks):
        st_ref[n, 0:LANES, :] = jnp.where(same_head, state.astype(BF16), zero_state)
        state = state * cdf + kv_ref[n, 0:LANES, :]
    state = jnp.zeros((LANES, LANES), F32)
    for n in reversed(range(n_chunks)):
        st_ref[n, LANES:2 * LANES, :] = jnp.where(same_head, state.astype(BF16), zero_state)
        state = state * cdb + kv_ref[n, LANES:2 * LANES, :]

    zero = jnp.zeros((CHUNK, LANES), BF16)
    for n in range(n_chunks):
        sl = pl.ds(n * CHUNK, CHUNK)
        qb = q_ref[0, pp, sl, :]
        kb = k_ref[0, pp, sl, :]
        qs = jnp.concatenate([jnp.where(head0, qb, zero), jnp.where(head0, zero, qb)], axis=0)
        p = (_dot_nt(qs, kb) * mask2).astype(BF16)
        p_ref[n, :, 0:CHUNK] = p[0:CHUNK]
        p_ref[n, :, CHUNK:2 * CHUNK] = p[CHUNK:2 * CHUNK]
        yield

    for n in range(n_chunks):
        sl = pl.ds(n * CHUNK, CHUNK)
        vb = v_ref[0, pp, sl, :]
        qc = q_ref[0, pp, sl, :].astype(F32)
        lhs = jnp.concatenate([p_ref[n], (qc * qdf).astype(BF16), (qc * qdb).astype(BF16)], axis=1)
        rhs = jnp.concatenate([jnp.where(head0, vb, zero), jnp.where(head0, zero, vb), st_ref[n]],
                              axis=0)
        y_ref[sl, :] = _dot(lhs, rhs)
        yield

    def group_mean(z):
        return _dot(z.astype(BF16), gn_avg)

    for n in range(n_chunks):
        sl = pl.ds(n * CHUNK, CHUNK)
        y = y_ref[sl, :]
        y_ref[sl, :] = y - group_mean(y)
        yield

    gain = gain_ref[pp]
    for n in range(n_chunks):
        sl = pl.ds(n * CHUNK, CHUNK)
        dlt = y_ref[sl, :]
        var = group_mean(dlt * dlt)
        gate = g_ref[0, pp, sl, :].astype(F32)
        o_ref[0, sl, out_lanes] = (dlt * lax.rsqrt(var + GN_EPS) * gain
                                   * (gate * jax.nn.sigmoid(gate))).astype(BF16)
        yield


_RET_YIELDS_PER_PAIR = 5


_STACK = 2 * BLOCK
_KEYS = 3 * BLOCK
_SUBLANES = 8
_LOG2E = 1.4426950408889634
_ATTN_STAGE_LAG = (0, 2, 4, 5)


def _attn_body(q_ref, kv_ref, slope_ref, sink_ref, o_ref, ks_ref, kss_ref, vt_ref, bias_ref,
               *, seq):
    n_blocks = seq // BLOCK
    lane = lax.broadcasted_iota(jnp.int32, (1, LANES), 1)
    low = lane < HEAD_DIM

    for hk in range(ATTN_KV_HEADS):
        kv = kv_ref[0, hk].astype(F32)
        ks = kv * (QK_SCALE * _LOG2E)
        ks_ref[hk] = ks.astype(BF16)
        kss_ref[hk] = pltpu.roll(ks, HEAD_DIM, 1).astype(BF16)
        kv_t = jnp.transpose(kv)
        vt_ref[hk] = jnp.concatenate(
            [kv_t[HEAD_DIM:], jnp.ones((_BF16_ROWS, seq), F32)], axis=0).astype(BF16)

    kj = lax.broadcasted_iota(jnp.int32, (_KEYS, _STACK), 0)
    c = lax.broadcasted_iota(jnp.int32, (_KEYS, _STACK), 1)
    qi = jnp.where(c < BLOCK, c, c - BLOCK)
    dist = jnp.abs(kj - BLOCK - qi)
    for hk in range(ATTN_KV_HEADS):
        for par in range(2):
            slope = slope_ref[hk, par, 0:1, :]
            bias_ref[hk, par] = jnp.where(dist <= WINDOW, -slope * dist.astype(F32), NEG_INF) * _LOG2E
    yield

    zero = jnp.zeros((BLOCK, LANES), BF16)

    def key_range(n):
        lo_blk, hi_blk = max(n - 1, 0), min(n + 2, n_blocks)
        return pl.ds(lo_blk * BLOCK, (hi_blk - lo_blk) * BLOCK), (lo_blk - (n - 1)) * BLOCK, \
            (hi_blk - lo_blk) * BLOCK

    units = [(hk, n, par) for hk in range(ATTN_KV_HEADS) for n in range(n_blocks) for par in range(2)]
    live = [dict() for _ in units]

    def stage_scores(u):
        hk, n, par = units[u]
        keys, c0, width = key_range(n)
        qblk = q_ref[0, hk, pl.ds(n * BLOCK, BLOCK), :]
        qa, qb = qblk[:, :LANES], qblk[:, LANES:]
        if par == 0:
            qs = jnp.concatenate([jnp.where(low, qa, zero), jnp.where(low, qb, zero)], axis=0)
            kmat = ks_ref[hk, keys, :]
        else:
            qs = jnp.concatenate([jnp.where(low, zero, qa), jnp.where(low, zero, qb)], axis=0)
            kmat = kss_ref[hk, keys, :]
        live[u]["s"] = _dot_nt(kmat, qs) + bias_ref[hk, par, c0:c0 + width, :]

    def stage_softmax(u):
        hk, _, par = units[u]
        st = live[u]
        s = st.pop("s")
        sink = sink_ref[hk, par, 0:1, :] * _LOG2E
        m = jnp.maximum(jnp.max(s, axis=0, keepdims=True), sink)
        st["e"] = jnp.exp2(s - m).astype(BF16)
        st["esink"] = jnp.exp2(sink - m)

    def stage_values(u):
        hk, n, _ = units[u]
        keys, _, _ = key_range(n)
        live[u]["pv"] = _dot(vt_ref[hk, :, keys], live[u].pop("e"))

    def stage_out(u):
        hk, n, par = units[u]
        st = live[u]
        pv = st.pop("pv")
        st["out"] = pv[:HEAD_DIM] / (pv[HEAD_DIM:HEAD_DIM + 1] + st.pop("esink"))
        if par == 1:
            even, odd = live[u - 1].pop("out"), st.pop("out")
            y_t = jnp.concatenate(
                [even[:, :BLOCK], odd[:, :BLOCK], even[:, BLOCK:], odd[:, BLOCK:]], axis=0)
            o_ref[0, pl.ds(n * BLOCK, BLOCK), pl.ds(hk * _STACK, _STACK)] = \
                jnp.transpose(y_t).astype(BF16)

    stages = (stage_scores, stage_softmax, stage_values, stage_out)
    for i in range(len(units) + _ATTN_STAGE_LAG[-1]):
        for lag, stage in zip(_ATTN_STAGE_LAG, stages):
            if 0 <= i - lag < len(units):
                stage(i - lag)
        yield


def _mixers_kernel(q_ref, k_ref, v_ref, g_ref, decf_ref, decb_ref, gain_ref,
                   aq_ref, akv_ref, slope_ref, sink_ref, yr_ref, ya_ref,
                   kv_ref, st_ref, p_ref, y_ref, ks_ref, kss_ref, vt_ref, bias_ref, *, seq):
    ret_args = (q_ref, k_ref, v_ref, g_ref, decf_ref, decb_ref, gain_ref, yr_ref,
                kv_ref, st_ref, p_ref, y_ref)
    n_units = ATTN_KV_HEADS * (seq // BLOCK) * 2
    _interleave([
        (_round_robin(*[_ret_body(*ret_args, seq=seq, pp=pp) for pp in range(RET_PAIRS)]),
         RET_PAIRS * _RET_YIELDS_PER_PAIR * (seq // CHUNK)),
        (_attn_body(aq_ref, akv_ref, slope_ref, sink_ref, ya_ref, ks_ref, kss_ref, vt_ref, bias_ref,
                    seq=seq), 1 + n_units + _ATTN_STAGE_LAG[-1]),
    ])


def _mixers(ret, aq, akv, decf, decb, gain, slope_cols, sink_cols):
    b, _, s, _ = ret.shape
    n_chunks = s // CHUNK
    qw = aq.shape[3]

    def part(kind):
        return pl.BlockSpec((1, RET_PAIRS, s, LANES), lambda i, kind=kind: (i, kind, 0, 0))

    def whole(arr):
        return pl.BlockSpec(arr.shape, lambda i: (0,) * arr.ndim)

    return pl.pallas_call(
        functools.partial(_mixers_kernel, seq=s),
        grid=(b,),
        in_specs=[
            part(0), part(1), part(2), part(3), whole(decf), whole(decb), whole(gain),
            pl.BlockSpec((1, ATTN_KV_HEADS, s, qw), lambda i: (i, 0, 0, 0)),
            pl.BlockSpec((1, ATTN_KV_HEADS, s, LANES), lambda i: (i, 0, 0, 0)),
            whole(slope_cols), whole(sink_cols),
        ],
        out_specs=[pl.BlockSpec((1, s, RET_WIDTH), lambda i: (i, 0, 0)),
                   pl.BlockSpec((1, s, ATTN_WIDTH), lambda i: (i, 0, 0))],
        out_shape=[jax.ShapeDtypeStruct((b, s, RET_WIDTH), BF16),
                   jax.ShapeDtypeStruct((b, s, ATTN_WIDTH), BF16)],
        scratch_shapes=[
            pltpu.VMEM((RET_PAIRS, n_chunks, 2 * LANES, LANES), F32),
            pltpu.VMEM((RET_PAIRS, n_chunks, 2 * LANES, LANES), BF16),
            pltpu.VMEM((RET_PAIRS, n_chunks, CHUNK, 2 * CHUNK), BF16),
            pltpu.VMEM((RET_PAIRS, s, LANES), F32),
            pltpu.VMEM((ATTN_KV_HEADS, s, LANES), BF16),
            pltpu.VMEM((ATTN_KV_HEADS, s, LANES), BF16),
            pltpu.VMEM((ATTN_KV_HEADS, HEAD_DIM + _BF16_ROWS, s), BF16),
            pltpu.VMEM((ATTN_KV_HEADS, 2, _KEYS, _STACK), F32),
        ],
        compiler_params=pltpu.CompilerParams(
            dimension_semantics=("arbitrary",), vmem_limit_bytes=VMEM_LIMIT_BYTES),
        name="mixers",
    )(ret, ret, ret, ret, decf, decb, gain, aq, akv, slope_cols, sink_cols)


_FFN_SUB_TILES = 2
_FFN_HIDDEN_CHUNK = 256
assert FFN_HIDDEN % _FFN_HIDDEN_CHUNK == 0


def _layer_norm(z, gain, bias):
    mu = jnp.mean(z, axis=-1, keepdims=True)
    dlt = z - mu
    var = jnp.mean(dlt * dlt, axis=-1, keepdims=True)
    return dlt * lax.rsqrt(var + LN_EPS) * gain + bias


def _ffn_kernel(x_ref, yr_ref, ya_ref, p_ref, wo_ref, g1_ref, b1_ref, wg_ref, wu_ref, wd_ref,
                wpe_ref, wpg_ref, g2_ref, b2_ref, o_ref, *, alpha):
    tm = x_ref.shape[0]
    rows = [pl.ds(k * (tm // _FFN_SUB_TILES), tm // _FFN_SUB_TILES) for k in range(_FFN_SUB_TILES)]
    mix = [_dot(yr_ref[r, :], wo_ref[0:RET_WIDTH, :]) + _dot(ya_ref[r, :], wo_ref[RET_WIDTH:, :])
           for r in rows]
    h = [_layer_norm(alpha * x_ref[r, :] + m, g1_ref[...], b1_ref[...]) for r, m in zip(rows, mix)]
    z = []
    for r, hk in zip(rows, h):
        hb = hk.astype(BF16)
        ple = _dot(p_ref[r, :].astype(BF16), wpe_ref[...])
        ple_gate = jax.nn.sigmoid(_dot(hb, wpg_ref[...]))
        rest = alpha * hk + ple_gate * ple
        acts, acc = [], rest
        for c in range(FFN_HIDDEN // _FFN_HIDDEN_CHUNK + 1):
            if c < FFN_HIDDEN // _FFN_HIDDEN_CHUNK:
                cols = pl.ds(c * _FFN_HIDDEN_CHUNK, _FFN_HIDDEN_CHUNK)
                g = _dot(hb, wg_ref[:, cols])
                up = _dot(hb, wu_ref[:, cols])
                acts.append((g * jax.nn.sigmoid(g) * up).astype(BF16))
            if c >= 1:
                acc = acc + _dot(acts[c - 1], wd_ref[pl.ds((c - 1) * _FFN_HIDDEN_CHUNK, _FFN_HIDDEN_CHUNK), :])
        z.append(acc)
    for r, zk in zip(rows, z):
        o_ref[r, :] = _layer_norm(zk, g2_ref[...], b2_ref[...])


def _ffn(x, yr, ya, p, wo, g1, b1, wg, wu, wd, wpe, wpg, g2, b2, *, alpha, tm):
    t, d = x.shape

    def rows(width):
        return pl.BlockSpec((tm, width), lambda i: (i, 0))

    def resident(arr):
        return pl.BlockSpec(arr.shape, lambda i: (0, 0), pipeline_mode=pl.Buffered(1))

    return pl.pallas_call(
        functools.partial(_ffn_kernel, alpha=alpha),
        grid=(t // tm,),
        in_specs=[rows(d), rows(RET_WIDTH), rows(ATTN_WIDTH), rows(PLE_DIM),
                  resident(wo), resident(g1), resident(b1), resident(wg), resident(wu),
                  resident(wd), resident(wpe), resident(wpg), resident(g2), resident(b2)],
        out_specs=rows(d),
        out_shape=jax.ShapeDtypeStruct((t, d), F32),
        compiler_params=pltpu.CompilerParams(
            dimension_semantics=("arbitrary",), vmem_limit_bytes=VMEM_LIMIT_BYTES),
        name="ffn",
    )(x, yr, ya, p, wo, g1, b1, wg, wu, wd, wpe, wpg, g2, b2)


def _stacked_head_cols(per_head):
    v = per_head.astype(F32).reshape(ATTN_KV_HEADS, 2, 2)
    v = jnp.transpose(v, (0, 2, 1))
    v = jnp.repeat(v, BLOCK, axis=2)
    return jnp.broadcast_to(v[:, :, None, :], v.shape[:2] + (_SUBLANES, _STACK))


def kernel(x, p, w_in, ret_decay_fwd, ret_decay_bwd, ret_gn_gain, attn_sink, w_out, ln1_gain, ln1_bias,
           w_ffn_gate, w_ffn_up, w_ffn_down, w_ple_proj, w_ple_gate, ln2_gain, ln2_bias):
    depth = w_in.shape[0]
    b, s, d = x.shape
    alpha = (2.0 * depth) ** 0.25
    slopes = 2.0 ** (-8.0 * (jnp.arange(ATTN_HEADS, dtype=F32) + 1.0) / ATTN_HEADS)
    slope_cols = _stacked_head_cols(slopes)

    def lane_rep(v):
        v = v.astype(F32).reshape(RET_PAIRS, HEADS_PER_LANE_BLOCK, 1)
        return jnp.broadcast_to(v, (RET_PAIRS, HEADS_PER_LANE_BLOCK, LANES))

    row = lambda v: v.astype(F32).reshape(1, -1)
    h = x.astype(F32)
    for i in range(depth):
        (ret, aq, akv), (wo, wg, wu, wd, wpe, wpg) = _in_proj(
            h, w_in, i, (w_out, w_ffn_gate, w_ffn_up, w_ffn_down, w_ple_proj, w_ple_gate), tm=IN_PROJ_TILE)
        y_ret, y_att = _mixers(ret, aq, akv, lane_rep(ret_decay_fwd[i]), lane_rep(ret_decay_bwd[i]),
                               ret_gn_gain[i].astype(F32).reshape(RET_PAIRS, 1, LANES),
                               slope_cols, _stacked_head_cols(attn_sink[i]))
        h = _ffn(h.reshape(b * s, d), y_ret.reshape(b * s, RET_WIDTH), y_att.reshape(b * s, ATTN_WIDTH),
                 p[i].reshape(b * s, PLE_DIM),
                 wo, row(ln1_gain[i]), row(ln1_bias[i]), wg, wu, wd, wpe, wpg,
                 row(ln2_gain[i]), row(ln2_bias[i]), alpha=alpha, tm=FFN_TILE).reshape(b, s, d)
    return h.astype(x.dtype)
```
